```python
import jax, jax.numpy as jnp
from jax import lax
import numpy as np


D_MODEL = 1024
BATCH = 2
SEQ = 8192
DEPTH = 4

N_MIXERS = 3
GRID_W = 64
D_FF = -(-8 * D_MODEL // (3 * 256)) * 256
NORM_EPS = 1e-6
NEG_INF = -1e30
QBLK = 128

A_GROUPS = ((128, 1), (512, 4), (2048, 16))
A_NG = len(A_GROUPS)
A_HEAD_DIM = 64
A_HEADS = D_MODEL // A_HEAD_DIM
A_QKV = 3 * A_NG * A_HEADS * A_HEAD_DIM

B_HEAD_DIM = 128
B_HEADS = D_MODEL // B_HEAD_DIM
B_KV_HEADS = 2
B_ROPE_THETA = 10000.0
B_QKV = (B_HEADS + 2 * B_KV_HEADS) * B_HEAD_DIM

C_HEADS = 16
C_Q_LORA = 384
C_KV_LORA = 256
C_NOPE = 64
C_ROPE = 32
C_V = 64
C_ROPE_THETA = 10000.0
C_IN = C_Q_LORA + C_KV_LORA + C_ROPE

N_A = (DEPTH + 2) // 3
N_B = (DEPTH + 1) // 3
N_C = DEPTH // 3

kernel_name = 'hybrid_dilated_axial_mla_encoder'


def rms_norm(x, gain):
    xf = x.astype(jnp.float32)
    y = xf * lax.rsqrt(jnp.mean(xf * xf, axis=-1, keepdims=True) + NORM_EPS)
    return (y * gain.astype(jnp.float32)).astype(x.dtype)


def rope(x, pos, theta):
    dim = x.shape[-1]
    half = dim // 2
    freqs = jnp.power(jnp.float32(theta), -jnp.arange(half, dtype=jnp.float32) * 2.0 / dim)
    ang = pos[:, None] * freqs[None, :]
    cos = jnp.cos(ang)[:, None, :]
    sin = jnp.sin(ang)[:, None, :]
    x1 = x[..., :half].astype(jnp.float32)
    x2 = x[..., half:].astype(jnp.float32)
    return jnp.concatenate([x1 * cos - x2 * sin, x1 * sin + x2 * cos], axis=-1).astype(x.dtype)


def axial_rope(x, row, col, theta):
    half = x.shape[-1] // 2
    return jnp.concatenate([rope(x[..., :half], row, theta), rope(x[..., half:], col, theta)], axis=-1)


def alibi_slopes(n):
    return jnp.asarray(2.0 ** (-8.0 * np.arange(1, n + 1) / n), dtype=jnp.float32)


def dense_block_attention(q, k, v):
    B, S, KVH, G, dq = q.shape
    nblk = S // QBLK
    qb = q.reshape(B, nblk, QBLK, KVH, G, dq).transpose(1, 0, 2, 3, 4, 5)
    scale = dq ** -0.5

    def attend(qi):
        s = jnp.einsum('bqkgd,bskd->bkgqs', qi, k).astype(jnp.float32) * scale
        p = jax.nn.softmax(s, axis=-1)
        return jnp.einsum('bkgqs,bske->bqkge', p.astype(v.dtype), v)

    o = lax.map(attend, qb)
    return o.transpose(1, 0, 2, 3, 4, 5).reshape(B, S, -1)


def dilated_window_attention(q, k, v, window, dilation, slopes):
    B, S, H, dh = q.shape
    d = dilation
    radius = window // (2 * dilation)
    wb = radius
    L = S // d
    nb = -(-L // wb)
    Lp = nb * wb

    def by_residue(a):
        return a.reshape(B, L, d, H, dh).transpose(0, 2, 1, 3, 4)

    qb = jnp.pad(by_residue(q), ((0, 0), (0, 0), (0, Lp - L), (0, 0), (0, 0))).reshape(B, d, nb, wb, H, dh)

    def key_blocks(a):
        ap = jnp.pad(by_residue(a), ((0, 0), (0, 0), (wb, Lp - L + wb), (0, 0), (0, 0)))
        ap = ap.reshape(B, d, nb + 2, wb, H, dh)
        return jnp.concatenate([ap[:, :, :-2], ap[:, :, 1:-1], ap[:, :, 2:]], axis=3)

    kb = key_blocks(k)
    vb = key_blocks(v)
    s = jnp.einsum('bdnqhe,bdnche->bdnhqc', qb, kb).astype(jnp.float32) * (dh ** -0.5)
    a_idx = jnp.arange(wb)
    c_idx = jnp.arange(3 * wb)
    diff = c_idx[None, :] - wb - a_idx[:, None]
    j_idx = (jnp.arange(nb) * wb)[:, None] - wb + c_idx[None, :]
    valid = (jnp.abs(diff) <= radius)[None] & ((j_idx >= 0) & (j_idx < L))[:, None, :]
    dist = (jnp.abs(diff) * d).astype(jnp.float32)
    bias = -slopes[:, None, None] * dist[None]
    s = s + bias[None, None, None]
    s = jnp.where(valid[None, None, :, None], s, NEG_INF)
    lse = jax.nn.logsumexp(s, axis=-1)
    p = jnp.exp(s - lse[..., None])
    o = jnp.einsum('bdnhqc,bdnche->bdnqhe', p.astype(vb.dtype), vb)
    o = o.reshape(B, d, Lp, H, dh)[:, :, :L].transpose(0, 2, 1, 3, 4).reshape(B, S, H, dh)
    lse = lse.transpose(0, 1, 2, 4, 3).reshape(B, d, Lp, H)[:, :, :L].transpose(0, 2, 1, 3).reshape(B, S, H)
    return o, lse


def mixer_a(x, wqkv, wo):
    B, S, _ = x.shape
    qkv = (x @ wqkv).reshape(B, S, 3, A_NG, A_HEADS, A_HEAD_DIM)
    slopes = alibi_slopes(A_NG * A_HEADS).reshape(A_NG, A_HEADS)
    outs = []
    lses = []
    for g, (window, dilation) in enumerate(A_GROUPS):
        o, l = dilated_window_attention(qkv[:, :, 0, g], qkv[:, :, 1, g], qkv[:, :, 2, g], window, dilation, slopes[g])
        outs.append(o)
        lses.append(l)
    wgt = jax.nn.softmax(jnp.stack(lses, axis=0), axis=0)
    o = jnp.sum(wgt[..., None] * jnp.stack(outs, axis=0).astype(jnp.float32), axis=0)
    return o.reshape(B, S, -1).astype(x.dtype) @ wo


def mixer_b(x, wqkv, qnorm, knorm, wo):
    B, S, _ = x.shape
    qkv = x @ wqkv
    nq = B_HEADS * B_HEAD_DIM
    nk = B_KV_HEADS * B_HEAD_DIM
    q = qkv[..., :nq].reshape(B, S, B_HEADS, B_HEAD_DIM)
    k = qkv[..., nq:nq + nk].reshape(B, S, B_KV_HEADS, B_HEAD_DIM)
    v = qkv[..., nq + nk:].reshape(B, S, B_KV_HEADS, B_HEAD_DIM)
    q = rms_norm(q, qnorm)
    k = rms_norm(k, knorm)
    rows = S // GRID_W
    row = jnp.repeat(jnp.arange(rows, dtype=jnp.float32), GRID_W)
    col = jnp.tile(jnp.arange(GRID_W, dtype=jnp.float32), rows)
    q = axial_rope(q, row, col, B_ROPE_THETA)
    k = axial_rope(k, row, col, B_ROPE_THETA)
    q = q.reshape(B, S, B_KV_HEADS, B_HEADS // B_KV_HEADS, B_HEAD_DIM)
    return dense_block_attention(q, k, v) @ wo


def mixer_c(x, win, qnorm, kvnorm, wuq, wukv, wo):
    B, S, _ = x.shape
    c = x @ win
    cq = c[..., :C_Q_LORA]
    ckv = c[..., C_Q_LORA:C_Q_LORA + C_KV_LORA]
    kr = c[..., C_Q_LORA + C_KV_LORA:]
    q = (rms_norm(cq, qnorm) @ wuq).reshape(B, S, C_HEADS, C_NOPE + C_ROPE)
    kv = (rms_norm(ckv, kvnorm) @ wukv).reshape(B, S, C_HEADS, C_NOPE + C_V)
    pos = jnp.arange(S, dtype=jnp.float32)
    q = jnp.concatenate([q[..., :C_NOPE], rope(q[..., C_NOPE:], pos, C_ROPE_THETA)], axis=-1)
    kr = rope(kr[:, :, None, :], pos, C_ROPE_THETA)
    k = jnp.concatenate([kv[..., :C_NOPE], jnp.broadcast_to(kr, (B, S, C_HEADS, C_ROPE))], axis=-1)
    v = kv[..., C_NOPE:]
    return dense_block_attention(q[:, :, :, None, :], k, v) @ wo


def swiglu(x, wg, wu, wd):
    return (jax.nn.silu(x @ wg) * (x @ wu)) @ wd


def setup_inputs(seed: int = 0) -> dict:
    key = jax.random.key(seed)
    ks = jax.random.split(key, 20)
    f32 = jnp.float32

    def w(k, shape, fan_in):
        return jax.random.normal(k, shape, f32) * (fan_in ** -0.5)

    def g(k, shape):
        return 1.0 + 0.05 * jax.random.normal(k, shape, f32)

    return {
        'x': jax.random.normal(ks[0], (BATCH, SEQ, D_MODEL), f32),
        'norm_mix_pre': g(ks[1], (DEPTH, D_MODEL)),
        'norm_mix_post': g(ks[2], (DEPTH, D_MODEL)),
        'norm_ffn_pre': g(ks[3], (DEPTH, D_MODEL)),
        'norm_ffn_post': g(ks[4], (DEPTH, D_MODEL)),
        'ffn_wg': w(ks[5], (DEPTH, D_MODEL, D_FF), D_MODEL),
        'ffn_wu': w(ks[6], (DEPTH, D_MODEL, D_FF), D_MODEL),
        'ffn_wd': w(ks[7], (DEPTH, D_FF, D_MODEL), D_FF),
        'a_wqkv': w(ks[8], (N_A, D_MODEL, A_QKV), D_MODEL),
        'a_wo': w(ks[9], (N_A, A_HEADS * A_HEAD_DIM, D_MODEL), A_HEADS * A_HEAD_DIM),
        'b_wqkv': w(ks[10], (N_B, D_MODEL, B_QKV), D_MODEL),
        'b_qnorm': g(ks[11], (N_B, B_HEAD_DIM)),
        'b_knorm': g(ks[12], (N_B, B_HEAD_DIM)),
        'b_wo': w(ks[13], (N_B, B_HEADS * B_HEAD_DIM, D_MODEL), B_HEADS * B_HEAD_DIM),
        'c_win': w(ks[14], (N_C, D_MODEL, C_IN), D_MODEL),
        'c_qnorm': g(ks[15], (N_C, C_Q_LORA)),
        'c_kvnorm': g(ks[16], (N_C, C_KV_LORA)),
        'c_wuq': w(ks[17], (N_C, C_Q_LORA, C_HEADS * (C_NOPE + C_ROPE)), C_Q_LORA),
        'c_wukv': w(ks[18], (N_C, C_KV_LORA, C_HEADS * (C_NOPE + C_V)), C_KV_LORA),
        'c_wo': w(ks[19], (N_C, C_HEADS * C_V, D_MODEL), C_HEADS * C_V),
    }


def reference(x, norm_mix_pre, norm_mix_post, norm_ffn_pre, norm_ffn_post, ffn_wg, ffn_wu, ffn_wd,
              a_wqkv, a_wo, b_wqkv, b_qnorm, b_knorm, b_wo,
              c_win, c_qnorm, c_kvnorm, c_wuq, c_wukv, c_wo):
    h = x
    for i in range(DEPTH):
        kind = i % N_MIXERS
        j = i // N_MIXERS
        y = rms_norm(h, norm_mix_pre[i])
        if kind == 0:
            y = mixer_a(y, a_wqkv[j], a_wo[j])
        elif kind == 1:
            y = mixer_b(y, b_wqkv[j], b_qnorm[j], b_knorm[j], b_wo[j])
        else:
            y = mixer_c(y, c_win[j], c_qnorm[j], c_kvnorm[j], c_wuq[j], c_wukv[j], c_wo[j])
        h = h + rms_norm(y, norm_mix_post[i])
        y = swiglu(rms_norm(h, norm_ffn_pre[i]), ffn_wg[i], ffn_wu[i], ffn_wd[i])
        h = h + rms_norm(y, norm_ffn_post[i])
    return h
```

```python
import functools

import numpy as np
import jax
import jax.numpy as jnp
from jax import lax
from jax.experimental import pallas as pl
from jax.experimental.pallas import tpu as pltpu

F32 = jnp.float32
BF16 = jnp.bfloat16

D_MODEL = 1024
BATCH = 2
SEQ = 8192
DEPTH = 4
N_MIXERS = 3
GRID_W = 64
D_FF = 2816
NORM_EPS = 1e-6
NEG_INF = -1e30

A_GROUPS = ((128, 1), (512, 4), (2048, 16))
A_NG = 3
A_HEAD_DIM = 64
A_HEADS = 16
A_QKV = 3 * A_NG * A_HEADS * A_HEAD_DIM
A_RADIUS = 64

B_HEAD_DIM = 128
B_HEADS = 8
B_KV_HEADS = 2
B_ROPE_THETA = 10000.0
B_QKV = (B_HEADS + 2 * B_KV_HEADS) * B_HEAD_DIM

C_HEADS = 16
C_Q_LORA = 384
C_KV_LORA = 256
C_NOPE = 64
C_ROPE = 32
C_V = 64
C_ROPE_THETA = 10000.0
C_IN_PAD = 768

LANES = 128
VMEM_LIMIT = 56 * 1024 * 1024

TM_PROJ = 1024
TN_PROJ = 1024
TM_ROW = 512
TF_FFN = 1408
TQ = 512
TK = 512
A_QB = 256
A_W = A_QB + 2 * A_RADIUS


def _cparams(sem):
    return pltpu.CompilerParams(dimension_semantics=sem, vmem_limit_bytes=VMEM_LIMIT)


def _rms(x, gain):
    ms = jnp.mean(x * x, axis=-1, keepdims=True)
    return (x * lax.rsqrt(ms + NORM_EPS)) * gain


def _norm_matmul_kernel(x_ref, g_ref, w_ref, o_ref, xn_ref):
    @pl.when(pl.program_id(1) == 0)
    def _():
        xn_ref[...] = _rms(x_ref[...], g_ref[...]).astype(BF16)

    o_ref[...] = jnp.dot(xn_ref[...], w_ref[...], preferred_element_type=F32)


def norm_matmul(h, gain, w):
    t, d = h.shape
    n = w.shape[1]
    return pl.pallas_call(
        _norm_matmul_kernel,
        grid=(t // TM_PROJ, n // TN_PROJ),
        in_specs=[
            pl.BlockSpec((TM_PROJ, d), lambda i, j: (i, 0)),
            pl.BlockSpec((1, d), lambda i, j: (0, 0)),
            pl.BlockSpec((d, TN_PROJ), lambda i, j: (0, j)),
        ],
        out_specs=pl.BlockSpec((TM_PROJ, TN_PROJ), lambda i, j: (i, j)),
        out_shape=jax.ShapeDtypeStruct((t, n), F32),
        scratch_shapes=[pltpu.VMEM((TM_PROJ, d), BF16)],
        compiler_params=_cparams(("parallel", "arbitrary")),
        name="a_qkv_proj",
    )(h, gain, w)


def _a_attn_kernel(slopes_ref, q_ref, k_ref, v_ref, o_ref, m_ref, l_ref, acc_ref):
    hp = pl.program_id(1)
    g = pl.program_id(2)
    lane = lax.broadcasted_iota(jnp.int32, (A_QB, LANES), 1)
    lo = lane < A_HEAD_DIM
    rel = (lax.broadcasted_iota(jnp.int32, (A_QB, A_W), 1)
           - lax.broadcasted_iota(jnp.int32, (A_QB, A_W), 0))

    def group_body(gi, d):
        cls_len = SEQ // d
        nblk = cls_len // A_QB

        def body(it, carry):
            r = it // nblk
            i0 = (it % nblk) * A_QB
            kstart = jnp.clip(i0 - A_RADIUS, 0, cls_len - A_W)
            off = kstart - i0
            if d == 1:
                q_rows = pl.ds(pl.multiple_of(i0, A_QB), A_QB)
                k_rows = pl.ds(pl.multiple_of(kstart, A_RADIUS), A_W)
            else:
                q_rows = pl.ds(r + i0 * d, A_QB, stride=d)
                k_rows = pl.ds(r + kstart * d, A_W, stride=d)
            q = (q_ref[q_rows, :] * (A_HEAD_DIM ** -0.5)).astype(BF16)
            k = k_ref[k_rows, :].astype(BF16)
            v = v_ref[k_rows, :].astype(BF16)
            dist = jnp.abs(rel + off)
            valid = dist <= A_RADIUS
            distf = dist.astype(F32) * float(d)
            ms, ls, pvs = [], [], []
            for h2 in range(2):
                slope = slopes_ref[gi * A_HEADS + hp * 2 + h2]
                qm = jnp.where(lo if h2 == 0 else jnp.logical_not(lo), q, jnp.zeros_like(q))
                s = lax.dot_general(qm, k, (((1,), (1,)), ((), ())), preferred_element_type=F32)
                s = jnp.where(valid, s - slope * distf, NEG_INF)
                m = jnp.max(s, axis=1, keepdims=True)
                p = jnp.exp(s - m)
                ls.append(jnp.sum(p, axis=1, keepdims=True))
                ms.append(m)
                pvs.append(jnp.dot(p.astype(BF16), v, preferred_element_type=F32))
            m_blk = jnp.where(lo, ms[0], ms[1])
            l_blk = jnp.where(lo, ls[0], ls[1])
            pv_blk = jnp.where(lo, pvs[0], pvs[1])
            if gi == 0:
                m_ref[q_rows, :] = m_blk
                l_ref[q_rows, :] = l_blk
                acc_ref[q_rows, :] = pv_blk
            else:
                m_old = m_ref[q_rows, :]
                m_new = jnp.maximum(m_old, m_blk)
                a_old = jnp.exp(m_old - m_new)
                a_blk = jnp.exp(m_blk - m_new)
                l_new = a_old * l_ref[q_rows, :] + a_blk * l_blk
                acc_new = a_old * acc_ref[q_rows, :] + a_blk * pv_blk
                if gi == A_NG - 1:
                    acc_ref[q_rows, :] = acc_new / l_new
                else:
                    m_ref[q_rows, :] = m_new
                    l_ref[q_rows, :] = l_new
                    acc_ref[q_rows, :] = acc_new
            return carry

        lax.fori_loop(0, d * nblk, body, 0)
        if gi == A_NG - 1:
            o_ref[...] = acc_ref[...].astype(o_ref.dtype)

    for gi, (_, d) in enumerate(A_GROUPS):
        pl.when(g == gi)(functools.partial(group_body, gi, d))


def a_attention(qkv, slopes):
    npairs = A_HEADS // 2

    def col(which):
        return lambda b, hp, g: (b, 0, (which * A_NG + g) * npairs + hp)

    return pl.pallas_call(
        _a_attn_kernel,
        grid=(BATCH, npairs, A_NG),
        in_specs=[
            pl.BlockSpec(memory_space=pltpu.SMEM),
            pl.BlockSpec((None, SEQ, LANES), col(0)),
            pl.BlockSpec((None, SEQ, LANES), col(1)),
            pl.BlockSpec((None, SEQ, LANES), col(2)),
        ],
        out_specs=pl.BlockSpec((None, SEQ, LANES), lambda b, hp, g: (b, 0, hp)),
        out_shape=jax.ShapeDtypeStruct((BATCH, SEQ, D_MODEL), BF16),
        scratch_shapes=[pltpu.VMEM((SEQ, LANES), F32)] * 3,
        compiler_params=_cparams(("parallel", "parallel", "arbitrary")),
        name="a_attention",
    )(slopes, qkv, qkv, qkv)


def _flash_kernel(q_ref, k_ref, v_ref, o_ref, m_ref, l_ref, acc_ref, *, nh, shared_k, pair_v):
    m_ref[...] = jnp.full(m_ref.shape, -jnp.inf, F32)
    l_ref[...] = jnp.zeros(l_ref.shape, F32)
    acc_ref[...] = jnp.zeros(acc_ref.shape, F32)

    def body(j, carry):
        rows = pl.ds(pl.multiple_of(j * TK, TK), TK)
        v = v_ref[rows, :]
        for h in range(nh):
            cols = slice(h * LANES, (h + 1) * LANES)
            k = k_ref[rows, :] if shared_k else k_ref[rows, cols]
            s = lax.dot_general(q_ref[:, cols], k, (((1,), (1,)), ((), ())),
                                preferred_element_type=F32)
            m_prev = m_ref[h]
            m_new = jnp.maximum(m_prev, jnp.max(s, axis=1, keepdims=True))
            alpha = jnp.exp(m_prev - m_new)
            p = jnp.exp(s - m_new[:, :1])
            l_ref[h] = alpha * l_ref[h] + jnp.sum(p, axis=1, keepdims=True)
            acc_ref[h] = alpha * acc_ref[h] + jnp.dot(p.astype(BF16), v, preferred_element_type=F32)
            m_ref[h] = m_new
        return carry

    lax.fori_loop(0, SEQ // TK, body, 0)

    if pair_v:
        lane = lax.broadcasted_iota(jnp.int32, (TQ, LANES), 1)
        o = jnp.where(lane < C_V, acc_ref[0] / l_ref[0], acc_ref[1] / l_ref[1])
        o_ref[...] = o.astype(o_ref.dtype)
    else:
        for h in range(nh):
            o_ref[:, h * LANES:(h + 1) * LANES] = (acc_ref[h] / l_ref[h]).astype(o_ref.dtype)


def _flash_call(kernel, n_outer, nh, in_specs, out_spec, args, name):
    return pl.pallas_call(
        kernel,
        grid=(BATCH, n_outer, SEQ // TQ),
        in_specs=in_specs,
        out_specs=out_spec,
        out_shape=jax.ShapeDtypeStruct((BATCH, SEQ, D_MODEL), BF16),
        scratch_shapes=[pltpu.VMEM((nh, TQ, LANES), F32)] * 3,
        compiler_params=_cparams(("parallel", "parallel", "arbitrary")),
        name=name,
    )(*args)


def b_attention(qkv):
    grp = B_HEADS // B_KV_HEADS
    kernel = functools.partial(_flash_kernel, nh=grp, shared_k=True, pair_v=False)
    in_specs = [
        pl.BlockSpec((None, TQ, grp * LANES), lambda b, kv, i: (b, i, kv)),
        pl.BlockSpec((None, SEQ, LANES), lambda b, kv, i: (b, 0, B_HEADS + kv)),
        pl.BlockSpec((None, SEQ, LANES), lambda b, kv, i: (b, 0, B_HEADS + B_KV_HEADS + kv)),
    ]
    out_spec = pl.BlockSpec((None, TQ, grp * LANES), lambda b, kv, i: (b, i, kv))
    return _flash_call(kernel, B_KV_HEADS, grp, in_specs, out_spec, (qkv, qkv, qkv), "b_attention")


def c_attention(q, k, v):
    kernel = functools.partial(_flash_kernel, nh=2, shared_k=False, pair_v=True)
    in_specs = [
        pl.BlockSpec((None, TQ, 2 * LANES), lambda b, p, i: (b, i, p)),
        pl.BlockSpec((None, SEQ, 2 * LANES), lambda b, p, i: (b, 0, p)),
        pl.BlockSpec((None, SEQ, LANES), lambda b, p, i: (b, 0, p)),
    ]
    out_spec = pl.BlockSpec((None, TQ, LANES), lambda b, p, i: (b, i, p))
    return _flash_call(kernel, C_HEADS // 2, 2, in_specs, out_spec, (q, k, v), "c_attention")


def _rope_lanes(x, cos, sin_a, sin_b, shift):
    return (x * cos + pltpu.roll(x, LANES - shift, 1) * sin_a + pltpu.roll(x, shift, 1) * sin_b)


def _b_prep_kernel(h_ref, g_ref, w_ref, qn_ref, kn_ref, cos_ref, sa_ref, sb_ref, o_ref):
    xn = _rms(h_ref[...], g_ref[...]).astype(BF16)
    qkv = jnp.dot(xn, w_ref[...], preferred_element_type=F32)
    cos, sa, sb = cos_ref[...], sa_ref[...], sb_ref[...]
    for hd in range(B_HEADS + 2 * B_KV_HEADS):
        cols = slice(hd * LANES, (hd + 1) * LANES)
        x = qkv[:, cols]
        if hd < B_HEADS + B_KV_HEADS:
            x = _rms(x, qn_ref[...] if hd < B_HEADS else kn_ref[...])
            x = _rope_lanes(x, cos, sa, sb, B_HEAD_DIM // 4)
            if hd < B_HEADS:
                x = x * (B_HEAD_DIM ** -0.5)
        o_ref[:, cols] = x.astype(BF16)


def b_prep(h, gain, w, qnorm, knorm, tables):
    t, d = h.shape
    nrow = SEQ // TM_ROW
    row = lambda i: (i, 0)
    fixed = lambda i: (0, 0)
    tab = lambda i: (i % nrow, 0)
    return pl.pallas_call(
        _b_prep_kernel,
        grid=(t // TM_ROW,),
        in_specs=[
            pl.BlockSpec((TM_ROW, d), row),
            pl.BlockSpec((1, d), fixed),
            pl.BlockSpec((d, B_QKV), fixed),
            pl.BlockSpec((1, B_HEAD_DIM), fixed),
            pl.BlockSpec((1, B_HEAD_DIM), fixed),
            pl.BlockSpec((TM_ROW, LANES), tab),
            pl.BlockSpec((TM_ROW, LANES), tab),
            pl.BlockSpec((TM_ROW, LANES), tab),
        ],
        out_specs=pl.BlockSpec((TM_ROW, B_QKV), row),
        out_shape=jax.ShapeDtypeStruct((t, B_QKV), BF16),
        compiler_params=_cparams(("parallel",)),
        name="b_prep",
    )(h, gain, w, qnorm, knorm, *tables)


def _c_prep_kernel(h_ref, g_ref, win_ref, qn_ref, kvn_ref, wuq_ref, wuk_ref, wuv_ref,
                   cos_ref, sa_ref, sb_ref, q_ref, k_ref, v_ref):
    xn = _rms(h_ref[...], g_ref[...]).astype(BF16)
    c = jnp.dot(xn, win_ref[...], preferred_element_type=F32)
    cq = _rms(c[:, :C_Q_LORA], qn_ref[...]).astype(BF16)
    ckv = _rms(c[:, C_Q_LORA:C_Q_LORA + C_KV_LORA], kvn_ref[...]).astype(BF16)
    cos, sa, sb = cos_ref[...], sa_ref[...], sb_ref[...]
    kr = _rope_lanes(c[:, C_Q_LORA + C_KV_LORA:], cos, sa, sb, C_ROPE // 2)
    q = jnp.dot(cq, wuq_ref[...], preferred_element_type=F32)
    kn = jnp.dot(ckv, wuk_ref[...], preferred_element_type=F32)
    v_ref[...] = jnp.dot(ckv, wuv_ref[...], preferred_element_type=F32).astype(BF16)
    scale = (C_NOPE + C_ROPE) ** -0.5
    for hd in range(C_HEADS):
        cols = slice(hd * LANES, (hd + 1) * LANES)
        q_ref[:, cols] = (_rope_lanes(q[:, cols], cos, sa, sb, C_ROPE // 2) * scale).astype(BF16)
        k_ref[:, cols] = (kn[:, cols] + kr).astype(BF16)


def c_prep(h, gain, win, qnorm, kvnorm, wuq, wuk, wuv, tables):
    t, d = h.shape
    nrow = SEQ // TM_ROW
    row = lambda i: (i, 0)
    fixed = lambda i: (0, 0)
    tab = lambda i: (i % nrow, 0)
    hq = C_HEADS * LANES
    return pl.pallas_call(
        _c_prep_kernel,
        grid=(t // TM_ROW,),
        in_specs=[
            pl.BlockSpec((TM_ROW, d), row),
            pl.BlockSpec((1, d), fixed),
            pl.BlockSpec((d, C_IN_PAD), fixed),
            pl.BlockSpec((1, C_Q_LORA), fixed),
            pl.BlockSpec((1, C_KV_LORA), fixed),
            pl.BlockSpec((C_Q_LORA, hq), fixed),
            pl.BlockSpec((C_KV_LORA, hq), fixed),
            pl.BlockSpec((C_KV_LORA, C_HEADS * C_V), fixed),
            pl.BlockSpec((TM_ROW, LANES), tab),
            pl.BlockSpec((TM_ROW, LANES), tab),
            pl.BlockSpec((TM_ROW, LANES), tab),
        ],
        out_specs=[
            pl.BlockSpec((TM_ROW, hq), row),
            pl.BlockSpec((TM_ROW, hq), row),
            pl.BlockSpec((TM_ROW, C_HEADS * C_V), row),
        ],
        out_shape=[
            jax.ShapeDtypeStruct((t, hq), BF16),
            jax.ShapeDtypeStruct((t, hq), BF16),
            jax.ShapeDtypeStruct((t, C_HEADS * C_V), BF16),
        ],
        compiler_params=_cparams(("parallel",)),
        name="c_prep",
    )(h, gain, win, qnorm, kvnorm, wuq, wuk, wuv, *tables)


def _out_proj_kernel(y_ref, w_ref, g_ref, h_ref, o_ref):
    z = jnp.dot(y_ref[...], w_ref[...], preferred_element_type=F32)
    o_ref[...] = h_ref[...] + _rms(z, g_ref[...])


def out_proj(y, w, gain, h):
    t, d = h.shape
    row = lambda i: (i, 0)
    fixed = lambda i: (0, 0)
    return pl.pallas_call(
        _out_proj_kernel,
        grid=(t // TM_ROW,),
        in_specs=[
            pl.BlockSpec((TM_ROW, d), row),
            pl.BlockSpec((d, d), fixed),
            pl.BlockSpec((1, d), fixed),
            pl.BlockSpec((TM_ROW, d), row),
        ],
        out_specs=pl.BlockSpec((TM_ROW, d), row),
        out_shape=jax.ShapeDtypeStruct((t, d), F32),
        compiler_params=_cparams(("parallel",)),
        name="out_proj",
    )(y, w, gain, h)


def _ffn_kernel(h_ref, gpre_ref, wg_ref, wu_ref, wd_ref, gpost_ref, o_ref, xn_ref, acc_ref):
    f = pl.program_id(1)

    @pl.when(f == 0)
    def _():
        xn_ref[...] = _rms(h_ref[...], gpre_ref[...]).astype(BF16)

    xn = xn_ref[...]
    gate = jnp.dot(xn, wg_ref[...], preferred_element_type=F32)
    up = jnp.dot(xn, wu_ref[...], preferred_element_type=F32)
    act = (gate * jax.nn.sigmoid(gate)) * up
    part = jnp.dot(act.astype(BF16), wd_ref[...], preferred_element_type=F32)

    @pl.when(f == 0)
    def _():
        acc_ref[...] = part

    @pl.when(f > 0)
    def _():
        acc_ref[...] += part

    @pl.when(f == pl.num_programs(1) - 1)
    def _():
        o_ref[...] = h_ref[...] + _rms(acc_ref[...], gpost_ref[...])


def ffn(h, gpre, wg, wu, wd, gpost):
    t, d = h.shape
    row = lambda i, f: (i, 0)
    fixed = lambda i, f: (0, 0)
    return pl.pallas_call(
        _ffn_kernel,
        grid=(t // TM_ROW, D_FF // TF_FFN),
        in_specs=[
            pl.BlockSpec((TM_ROW, d), row),
            pl.BlockSpec((1, d), fixed),
            pl.BlockSpec((d, TF_FFN), lambda i, f: (0, f)),
            pl.BlockSpec((d, TF_FFN), lambda i, f: (0, f)),
            pl.BlockSpec((TF_FFN, d), lambda i, f: (f, 0)),
            pl.BlockSpec((1, d), fixed),
        ],
        out_specs=pl.BlockSpec((TM_ROW, d), row),
        out_shape=jax.ShapeDtypeStruct((t, d), F32),
        scratch_shapes=[pltpu.VMEM((TM_ROW, d), BF16), pltpu.VMEM((TM_ROW, d), F32)],
        compiler_params=_cparams(("parallel", "arbitrary")),
        name="ffn",
    )(h, gpre, wg, wu, wd, gpost)


def _rope_tables(pos_by_lane, freq_idx, dim, theta, active, first_half):
    freqs = jnp.power(jnp.float32(theta), -freq_idx.astype(F32) * 2.0 / dim)
    ang = pos_by_lane * freqs[None, :]
    cos = jnp.where(active[None, :], jnp.cos(ang), 1.0)
    sin = jnp.sin(ang)
    sin_a = jnp.where((active & first_half)[None, :], -sin, 0.0)
    sin_b = jnp.where((active & ~first_half)[None, :], sin, 0.0)
    return cos.astype(F32), sin_a.astype(F32), sin_b.astype(F32)


def _b_tables():
    lane = np.arange(LANES)
    half = B_HEAD_DIM // 2
    quarter = half // 2
    t = jnp.arange(SEQ)
    row = (t // GRID_W).astype(F32)
    col = (t % GRID_W).astype(F32)
    pos = jnp.where(jnp.asarray(lane < half)[None, :], row[:, None], col[:, None])
    return _rope_tables(pos, jnp.asarray(lane % quarter), half, B_ROPE_THETA,
                        jnp.asarray(np.ones(LANES, bool)), jnp.asarray((lane % half) < quarter))


def _c_tables():
    lane = np.arange(LANES)
    half = C_ROPE // 2
    active = (lane >= C_NOPE) & (lane < C_NOPE + C_ROPE)
    pos = jnp.broadcast_to(jnp.arange(SEQ, dtype=F32)[:, None], (SEQ, LANES))
    return _rope_tables(pos, jnp.asarray((lane - C_NOPE) % half), C_ROPE, C_ROPE_THETA,
                        jnp.asarray(active), jnp.asarray((lane - C_NOPE) < half))


def _alibi_slopes():
    n = A_NG * A_HEADS
    return jnp.asarray(2.0 ** (-8.0 * np.arange(1, n + 1) / n), dtype=F32)


def _mixer_a(h, gain, wqkv, wo, gpost):
    qkv = norm_matmul(h, gain, wqkv.astype(BF16))
    y = a_attention(qkv.reshape(BATCH, SEQ, A_QKV), _alibi_slopes())
    return out_proj(y.reshape(BATCH * SEQ, D_MODEL), wo.astype(BF16), gpost, h)


def _mixer_b(h, gain, wqkv, qnorm, knorm, wo, gpost):
    qkv = b_prep(h, gain, wqkv.astype(BF16), qnorm[None, :], knorm[None, :], _b_tables())
    y = b_attention(qkv.reshape(BATCH, SEQ, B_QKV))
    return out_proj(y.reshape(BATCH * SEQ, D_MODEL), wo.astype(BF16), gpost, h)


def _mixer_c(h, gain, win, qnorm, kvnorm, wuq, wukv, wo, gpost):
    win_p = jnp.zeros((D_MODEL, C_IN_PAD), F32)
    win_p = win_p.at[:, :C_Q_LORA + C_KV_LORA].set(win[:, :C_Q_LORA + C_KV_LORA])
    kr0 = C_Q_LORA + C_KV_LORA + C_NOPE
    win_p = win_p.at[:, kr0:kr0 + C_ROPE].set(win[:, C_Q_LORA + C_KV_LORA:])
    wuq_p = jnp.pad(wuq.reshape(C_Q_LORA, C_HEADS, C_NOPE + C_ROPE),
                    ((0, 0), (0, 0), (0, LANES - C_NOPE - C_ROPE))).reshape(C_Q_LORA, C_HEADS * LANES)
    wukv3 = wukv.reshape(C_KV_LORA, C_HEADS, C_NOPE + C_V)
    wuk_p = jnp.pad(wukv3[:, :, :C_NOPE],
                    ((0, 0), (0, 0), (0, LANES - C_NOPE))).reshape(C_KV_LORA, C_HEADS * LANES)
    wuv = wukv3[:, :, C_NOPE:].reshape(C_KV_LORA, C_HEADS * C_V)
    q, k, v = c_prep(h, gain, win_p.astype(BF16), qnorm[None, :], kvnorm[None, :],
                     wuq_p.astype(BF16), wuk_p.astype(BF16), wuv.astype(BF16), _c_tables())
    shp = (BATCH, SEQ, -1)
    y = c_attention(q.reshape(shp), k.reshape(shp), v.reshape(shp))
    return out_proj(y.reshape(BATCH * SEQ, D_MODEL), wo.astype(BF16), gpost, h)


def kernel(x, norm_mix_pre, norm_mix_post, norm_ffn_pre, norm_ffn_post, ffn_wg, ffn_wu, ffn_wd,
           a_wqkv, a_wo, b_wqkv, b_qnorm, b_knorm, b_wo,
           c_win, c_qnorm, c_kvnorm, c_wuq, c_wukv, c_wo):
    h = x.reshape(BATCH * SEQ, D_MODEL)
    for i in range(DEPTH):
        kind = i % N_MIXERS
        j = i // N_MIXERS
        gpre = norm_mix_pre[i][None, :]
        gpost = norm_mix_post[i][None, :]
        if kind == 0:
            h = _mixer_a(h, gpre, a_wqkv[j], a_wo[j], gpost)
        elif kind == 1:
            h = _mixer_b(h, gpre, b_wqkv[j], b_qnorm[j], b_knorm[j], b_wo[j], gpost)
        else:
            h = _mixer_c(h, gpre, c_win[j], c_qnorm[j], c_kvnorm[j], c_wuq[j], c_wukv[j], c_wo[j], gpost)
        h = ffn(h, norm_ffn_pre[i][None, :], ffn_wg[i].astype(BF16), ffn_wu[i].astype(BF16),
                ffn_wd[i].astype(BF16), norm_ffn_post[i][None, :])
    return h.reshape(BATCH, SEQ, D_MODEL)
```

```python
import functools

import numpy as np
import jax
import jax.numpy as jnp
from jax import lax
from jax.experimental import pallas as pl
from jax.experimental.pallas import tpu as pltpu

F32 = jnp.float32
BF16 = jnp.bfloat16

D_MODEL = 1024
BATCH = 2
SEQ = 8192
DEPTH = 4
N_MIXERS = 3
GRID_W = 64
D_FF = 2816
NORM_EPS = 1e-6
NEG_INF = -1e30

A_GROUPS = ((128, 1), (512, 4), (2048, 16))
A_NG = 3
A_HEAD_DIM = 64
A_HEADS = 16
A_QKV = 3 * A_NG * A_HEADS * A_HEAD_DIM
A_RADIUS = 64

B_HEAD_DIM = 128
B_HEADS = 8
B_KV_HEADS = 2
B_ROPE_THETA = 10000.0
B_QKV = (B_HEADS + 2 * B_KV_HEADS) * B_HEAD_DIM

C_HEADS = 16
C_Q_LORA = 384
C_KV_LORA = 256
C_NOPE = 64
C_ROPE = 32
C_V = 64
C_ROPE_THETA = 10000.0
C_IN_PAD = 768

LANES = 128
ONES_ROWS = 16
LOG2E = 1.4426950408889634
VMEM_LIMIT = 56 * 1024 * 1024

TM_PROJ = 1024
TN_PROJ = 1024
TM_ROW = 512
TF_FFN = 1408
TQ = 512
TK = 512
TQS = 512
TKS = 512
A_QB = 256
A_W = A_QB + 2 * A_RADIUS


def _cparams(sem):
    return pltpu.CompilerParams(dimension_semantics=sem, vmem_limit_bytes=VMEM_LIMIT)


def _rms(x, gain):
    ms = jnp.mean(x * x, axis=-1, keepdims=True)
    return (x * lax.rsqrt(ms + NORM_EPS)) * gain


def _norm_matmul_kernel(x_ref, g_ref, w_ref, o_ref, xn_ref):
    @pl.when(pl.program_id(1) == 0)
    def _():
        xn_ref[...] = _rms(x_ref[...], g_ref[...]).astype(BF16)

    o_ref[...] = jnp.dot(xn_ref[...], w_ref[...], preferred_element_type=F32)


def norm_matmul(h, gain, w):
    t, d = h.shape
    n = w.shape[1]
    return pl.pallas_call(
        _norm_matmul_kernel,
        grid=(t // TM_PROJ, n // TN_PROJ),
        in_specs=[
            pl.BlockSpec((TM_PROJ, d), lambda i, j: (i, 0)),
            pl.BlockSpec((1, d), lambda i, j: (0, 0)),
            pl.BlockSpec((d, TN_PROJ), lambda i, j: (0, j)),
        ],
        out_specs=pl.BlockSpec((TM_PROJ, TN_PROJ), lambda i, j: (i, j)),
        out_shape=jax.ShapeDtypeStruct((t, n), F32),
        scratch_shapes=[pltpu.VMEM((TM_PROJ, d), BF16)],
        compiler_params=_cparams(("parallel", "arbitrary")),
        name="a_qkv_proj",
    )(h, gain, w)


def _a_attn_kernel(slopes_ref, q_ref, k_ref, v_ref, o_ref, m_ref, l_ref, acc_ref):
    hp = pl.program_id(1)
    g = pl.program_id(2)
    lane = lax.broadcasted_iota(jnp.int32, (A_QB, LANES), 1)
    lo = lane < A_HEAD_DIM
    rel = (lax.broadcasted_iota(jnp.int32, (A_QB, A_W), 1)
           - lax.broadcasted_iota(jnp.int32, (A_QB, A_W), 0))

    def group_body(gi, d):
        cls_len = SEQ // d
        nblk = cls_len // A_QB

        def body(it, carry):
            r = it // nblk
            i0 = (it % nblk) * A_QB
            kstart = jnp.clip(i0 - A_RADIUS, 0, cls_len - A_W)
            off = kstart - i0
            if d == 1:
                q_rows = pl.ds(pl.multiple_of(i0, A_QB), A_QB)
                k_rows = pl.ds(pl.multiple_of(kstart, A_RADIUS), A_W)
            else:
                q_rows = pl.ds(r + i0 * d, A_QB, stride=d)
                k_rows = pl.ds(r + kstart * d, A_W, stride=d)
            q = (q_ref[q_rows, :] * (A_HEAD_DIM ** -0.5)).astype(BF16)
            k = k_ref[k_rows, :].astype(BF16)
            v = v_ref[k_rows, :].astype(BF16)
            dist = jnp.abs(rel + off)
            valid = dist <= A_RADIUS
            distf = dist.astype(F32) * float(d)
            ms, ls, pvs = [], [], []
            for h2 in range(2):
                slope = slopes_ref[gi * A_HEADS + hp * 2 + h2]
                qm = jnp.where(lo if h2 == 0 else jnp.logical_not(lo), q, jnp.zeros_like(q))
                s = lax.dot_general(qm, k, (((1,), (1,)), ((), ())), preferred_element_type=F32)
                s = jnp.where(valid, s - slope * distf, NEG_INF)
                m = jnp.max(s, axis=1, keepdims=True)
                p = jnp.exp(s - m)
                ls.append(jnp.sum(p, axis=1, keepdims=True))
                ms.append(m)
                pvs.append(jnp.dot(p.astype(BF16), v, preferred_element_type=F32))
            m_blk = jnp.where(lo, ms[0], ms[1])
            l_blk = jnp.where(lo, ls[0], ls[1])
            pv_blk = jnp.where(lo, pvs[0], pvs[1])
            if gi == 0:
                m_ref[q_rows, :] = m_blk
                l_ref[q_rows, :] = l_blk
                acc_ref[q_rows, :] = pv_blk
            else:
                m_old = m_ref[q_rows, :]
                m_new = jnp.maximum(m_old, m_blk)
                a_old = jnp.exp(m_old - m_new)
                a_blk = jnp.exp(m_blk - m_new)
                l_new = a_old * l_ref[q_rows, :] + a_blk * l_blk
                acc_new = a_old * acc_ref[q_rows, :] + a_blk * pv_blk
                if gi == A_NG - 1:
                    acc_ref[q_rows, :] = acc_new / l_new
                else:
                    m_ref[q_rows, :] = m_new
                    l_ref[q_rows, :] = l_new
                    acc_ref[q_rows, :] = acc_new
            return carry

        lax.fori_loop(0, d * nblk, body, 0)
        if gi == A_NG - 1:
            o_ref[...] = acc_ref[...].astype(o_ref.dtype)

    for gi, (_, d) in enumerate(A_GROUPS):
        pl.when(g == gi)(functools.partial(group_body, gi, d))


def a_attention(qkv, slopes):
    npairs = A_HEADS // 2

    def col(which):
        return lambda b, hp, g: (b, 0, (which * A_NG + g) * npairs + hp)

    return pl.pallas_call(
        _a_attn_kernel,
        grid=(BATCH, npairs, A_NG),
        in_specs=[
            pl.BlockSpec(memory_space=pltpu.SMEM),
            pl.BlockSpec((None, SEQ, LANES), col(0)),
            pl.BlockSpec((None, SEQ, LANES), col(1)),
            pl.BlockSpec((None, SEQ, LANES), col(2)),
        ],
        out_specs=pl.BlockSpec((None, SEQ, LANES), lambda b, hp, g: (b, 0, hp)),
        out_shape=jax.ShapeDtypeStruct((BATCH, SEQ, D_MODEL), BF16),
        scratch_shapes=[pltpu.VMEM((SEQ, LANES), F32)] * 3,
        compiler_params=_cparams(("parallel", "parallel", "arbitrary")),
        name="a_attention",
    )(slopes, qkv, qkv, qkv)


def _flash_kernel(qt_ref, k_ref, vt_ref, o_ref, m_ref, acc_ref, s_ref, mc_ref, *, nh, dv, shared_kv):
    m_ref[...] = jnp.full(m_ref.shape, -jnp.inf, F32)
    acc_ref[...] = jnp.zeros(acc_ref.shape, F32)
    ones = jnp.ones((ONES_ROWS, TK), BF16)
    nchunk = SEQ // TK

    def scores(chunk, slot):
        rows = pl.ds(pl.multiple_of(chunk * TK, TK), TK)
        for h in range(nh):
            k = k_ref[rows, :] if shared_kv else k_ref[rows, h * LANES:(h + 1) * LANES]
            st = jnp.dot(k, qt_ref[h * LANES:(h + 1) * LANES, :], preferred_element_type=F32)
            s_ref[slot, h] = st
            mc_ref[slot, h] = jnp.max(st, axis=0, keepdims=True)

    def update(chunk, slot):
        rows = pl.ds(pl.multiple_of(chunk * TK, TK), TK)
        for h in range(nh):
            vt = vt_ref[:, rows] if shared_kv else vt_ref[h * dv:(h + 1) * dv, rows]
            m_prev = m_ref[h]
            m_new = jnp.maximum(m_prev, mc_ref[slot, h])
            alpha = jnp.exp2(m_prev - m_new)
            pt = jnp.exp2(s_ref[slot, h] - m_new).astype(BF16)
            pv = jnp.dot(jnp.concatenate([vt, ones], axis=0), pt, preferred_element_type=F32)
            acc_ref[h] = alpha * acc_ref[h] + pv
            m_ref[h] = m_new

    scores(0, 0)

    def body(jj, carry):
        j = 2 * jj
        scores(j + 1, 1)
        update(j, 0)
        scores(jnp.minimum(j + 2, nchunk - 1), 0)
        update(j + 1, 1)
        return carry

    lax.fori_loop(0, nchunk // 2, body, 0)

    outs = []
    for h in range(nh):
        acc = acc_ref[h]
        outs.append(acc[:dv, :] / acc[dv:dv + 1, :])
    ot = outs[0] if nh == 1 else jnp.concatenate(outs, axis=0)
    o_ref[...] = ot.T.astype(o_ref.dtype)


def _flash_call(kernel, grid, nh, dv, in_specs, out_spec, args, name):
    return pl.pallas_call(
        kernel,
        grid=grid,
        in_specs=in_specs,
        out_specs=out_spec,
        out_shape=jax.ShapeDtypeStruct((BATCH, SEQ, D_MODEL), BF16),
        scratch_shapes=[
            pltpu.VMEM((nh, 1, TQ), F32),
            pltpu.VMEM((nh, dv + ONES_ROWS, TQ), F32),
            pltpu.VMEM((2, nh, TK, TQ), F32),
            pltpu.VMEM((2, nh, 1, TQ), F32),
        ],
        compiler_params=_cparams(("parallel",) * (len(grid) - 1) + ("arbitrary",)),
        name=name,
    )(*args)


def b_attention(qt, k, vt):
    grp = B_HEADS // B_KV_HEADS
    nh = 2
    kernel = functools.partial(_flash_kernel, nh=nh, dv=B_HEAD_DIM, shared_kv=True)
    in_specs = [
        pl.BlockSpec((None, nh * LANES, TQ), lambda b, kv, g, i: (b, kv * (grp // nh) + g, i)),
        pl.BlockSpec((None, SEQ, LANES), lambda b, kv, g, i: (b, 0, kv)),
        pl.BlockSpec((None, LANES, SEQ), lambda b, kv, g, i: (b, kv, 0)),
    ]
    out_spec = pl.BlockSpec((None, TQ, nh * LANES), lambda b, kv, g, i: (b, i, kv * (grp // nh) + g))
    grid = (BATCH, B_KV_HEADS, grp // nh, SEQ // TQ)
    return _flash_call(kernel, grid, nh, B_HEAD_DIM, in_specs, out_spec, (qt, k, vt), "b_attention")


def c_attention(qt, k, vt):
    nh = 2
    kernel = functools.partial(_flash_kernel, nh=nh, dv=C_V, shared_kv=False)
    in_specs = [
        pl.BlockSpec((None, nh * LANES, TQ), lambda b, p, i: (b, p, i)),
        pl.BlockSpec((None, SEQ, nh * LANES), lambda b, p, i: (b, 0, p)),
        pl.BlockSpec((None, nh * C_V, SEQ), lambda b, p, i: (b, p, 0)),
    ]
    out_spec = pl.BlockSpec((None, TQ, nh * C_V), lambda b, p, i: (b, i, p))
    grid = (BATCH, C_HEADS // nh, SEQ // TQ)
    return _flash_call(kernel, grid, nh, C_V, in_specs, out_spec, (qt, k, vt), "c_attention")


def _rope_lanes(x, cos, sin_a, sin_b, shift):
    return (x * cos + pltpu.roll(x, LANES - shift, 1) * sin_a + pltpu.roll(x, shift, 1) * sin_b)


def _b_prep_kernel(h_ref, g_ref, w_ref, qn_ref, kn_ref, cos_ref, sa_ref, sb_ref,
                   qt_ref, k_ref, vt_ref):
    xn = _rms(h_ref[...], g_ref[...]).astype(BF16)
    qkv = jnp.dot(xn, w_ref[...], preferred_element_type=F32)
    cos, sa, sb = cos_ref[...], sa_ref[...], sb_ref[...]
    for hd in range(B_HEADS + 2 * B_KV_HEADS):
        x = qkv[:, hd * LANES:(hd + 1) * LANES]
        if hd < B_HEADS + B_KV_HEADS:
            x = _rms(x, qn_ref[...] if hd < B_HEADS else kn_ref[...])
            x = _rope_lanes(x, cos, sa, sb, B_HEAD_DIM // 4)
        if hd < B_HEADS:
            x = x * (B_HEAD_DIM ** -0.5 * LOG2E)
            qt_ref[hd * LANES:(hd + 1) * LANES, :] = x.T.astype(BF16)
        elif hd < B_HEADS + B_KV_HEADS:
            j = hd - B_HEADS
            k_ref[:, j * LANES:(j + 1) * LANES] = x.astype(BF16)
        else:
            j = hd - B_HEADS - B_KV_HEADS
            vt_ref[j * LANES:(j + 1) * LANES, :] = x.T.astype(BF16)


def b_prep(h, gain, w, qnorm, knorm, tables):
    t, d = h.shape
    nrow = SEQ // TM_ROW
    row = lambda i: (i, 0)
    fixed = lambda i: (0, 0)
    tab = lambda i: (i % nrow, 0)
    colblk = lambda i: (i // nrow, 0, i % nrow)
    nq = B_HEADS * B_HEAD_DIM
    nkv = B_KV_HEADS * B_HEAD_DIM
    return pl.pallas_call(
        _b_prep_kernel,
        grid=(t // TM_ROW,),
        in_specs=[
            pl.BlockSpec((TM_ROW, d), row),
            pl.BlockSpec((1, d), fixed),
            pl.BlockSpec((d, B_QKV), fixed),
            pl.BlockSpec((1, B_HEAD_DIM), fixed),
            pl.BlockSpec((1, B_HEAD_DIM), fixed),
            pl.BlockSpec((TM_ROW, LANES), tab),
            pl.BlockSpec((TM_ROW, LANES), tab),
            pl.BlockSpec((TM_ROW, LANES), tab),
        ],
        out_specs=[
            pl.BlockSpec((None, nq, TM_ROW), colblk),
            pl.BlockSpec((TM_ROW, nkv), row),
            pl.BlockSpec((None, nkv, TM_ROW), colblk),
        ],
        out_shape=[
            jax.ShapeDtypeStruct((BATCH, nq, SEQ), BF16),
            jax.ShapeDtypeStruct((t, nkv), BF16),
            jax.ShapeDtypeStruct((BATCH, nkv, SEQ), BF16),
        ],
        compiler_params=_cparams(("parallel",)),
        name="b_prep",
    )(h, gain, w, qnorm, knorm, *tables)


def _c_prep_kernel(h_ref, g_ref, win_ref, qn_ref, kvn_ref, wuq_ref, wuk_ref, wuv_ref,
                   cos_ref, sa_ref, sb_ref, q_ref, k_ref, v_ref):
    xn = _rms(h_ref[...], g_ref[...]).astype(BF16)
    c = jnp.dot(xn, win_ref[...], preferred_element_type=F32)
    cq = _rms(c[:, :C_Q_LORA], qn_ref[...]).astype(BF16)
    ckv = _rms(c[:, C_Q_LORA:C_Q_LORA + C_KV_LORA], kvn_ref[...]).astype(BF16)
    cos, sa, sb = cos_ref[...], sa_ref[...], sb_ref[...]
    kr = _rope_lanes(c[:, C_Q_LORA + C_KV_LORA:], cos, sa, sb, C_ROPE // 2)
    q = jnp.dot(cq, wuq_ref[...], preferred_element_type=F32)
    kn = jnp.dot(ckv, wuk_ref[...], preferred_element_type=F32)
    v = jnp.dot(ckv, wuv_ref[...], preferred_element_type=F32)
    scale = (C_NOPE + C_ROPE) ** -0.5 * LOG2E
    for hd in range(C_HEADS):
        cols = slice(hd * LANES, (hd + 1) * LANES)
        qh = _rope_lanes(q[:, cols], cos, sa, sb, C_ROPE // 2) * scale
        q_ref[cols, :] = qh.T.astype(BF16)
        k_ref[:, cols] = (kn[:, cols] + kr).astype(BF16)
    for blk in range(C_HEADS * C_V // LANES):
        cols = slice(blk * LANES, (blk + 1) * LANES)
        v_ref[cols, :] = v[:, cols].T.astype(BF16)


def c_prep(h, gain, win, qnorm, kvnorm, wuq, wuk, wuv, tables):
    t, d = h.shape
    nrow = SEQ // TM_ROW
    row = lambda i: (i, 0)
    fixed = lambda i: (0, 0)
    tab = lambda i: (i % nrow, 0)
    colblk = lambda i: (i // nrow, 0, i % nrow)
    hq = C_HEADS * LANES
    return pl.pallas_call(
        _c_prep_kernel,
        grid=(t // TM_ROW,),
        in_specs=[
            pl.BlockSpec((TM_ROW, d), row),
            pl.BlockSpec((1, d), fixed),
            pl.BlockSpec((d, C_IN_PAD), fixed),
            pl.BlockSpec((1, C_Q_LORA), fixed),
            pl.BlockSpec((1, C_KV_LORA), fixed),
            pl.BlockSpec((C_Q_LORA, hq), fixed),
            pl.BlockSpec((C_KV_LORA, hq), fixed),
            pl.BlockSpec((C_KV_LORA, C_HEADS * C_V), fixed),
            pl.BlockSpec((TM_ROW, LANES), tab),
            pl.BlockSpec((TM_ROW, LANES), tab),
            pl.BlockSpec((TM_ROW, LANES), tab),
        ],
        out_specs=[
            pl.BlockSpec((None, hq, TM_ROW), colblk),
            pl.BlockSpec((TM_ROW, hq), row),
            pl.BlockSpec((None, C_HEADS * C_V, TM_ROW), colblk),
        ],
        out_shape=[
            jax.ShapeDtypeStruct((BATCH, hq, SEQ), BF16),
            jax.ShapeDtypeStruct((t, hq), BF16),
            jax.ShapeDtypeStruct((BATCH, C_HEADS * C_V, SEQ), BF16),
        ],
        compiler_params=_cparams(("parallel",)),
        name="c_prep",
    )(h, gain, win, qnorm, kvnorm, wuq, wuk, wuv, *tables)


def _out_proj_kernel(y_ref, w_ref, g_ref, h_ref, o_ref):
    z = jnp.dot(y_ref[...], w_ref[...], preferred_element_type=F32)
    o_ref[...] = h_ref[...] + _rms(z, g_ref[...])


def out_proj(y, w, gain, h):
    t, d = h.shape
    row = lambda i: (i, 0)
    fixed = lambda i: (0, 0)
    return pl.pallas_call(
        _out_proj_kernel,
        grid=(t // TM_ROW,),
        in_specs=[
            pl.BlockSpec((TM_ROW, d), row),
            pl.BlockSpec((d, d), fixed),
            pl.BlockSpec((1, d), fixed),
            pl.BlockSpec((TM_ROW, d), row),
        ],
        out_specs=pl.BlockSpec((TM_ROW, d), row),
        out_shape=jax.ShapeDtypeStruct((t, d), F32),
        compiler_params=_cparams(("parallel",)),
        name="out_proj",
    )(y, w, gain, h)


def _ffn_kernel(h_ref, gpre_ref, wg_ref, wu_ref, wd_ref, gpost_ref, o_ref, xn_ref, acc_ref):
    f = pl.program_id(1)

    @pl.when(f == 0)
    def _():
        xn_ref[...] = _rms(h_ref[...], gpre_ref[...]).astype(BF16)

    xn = xn_ref[...]
    gate = jnp.dot(xn, wg_ref[...], preferred_element_type=F32)
    up = jnp.dot(xn, wu_ref[...], preferred_element_type=F32)
    act = (gate * jax.nn.sigmoid(gate)) * up
    part = jnp.dot(act.astype(BF16), wd_ref[...], preferred_element_type=F32)

    @pl.when(f == 0)
    def _():
        acc_ref[...] = part

    @pl.when(f > 0)
    def _():
        acc_ref[...] += part

    @pl.when(f == pl.num_programs(1) - 1)
    def _():
        o_ref[...] = h_ref[...] + _rms(acc_ref[...], gpost_ref[...])


def ffn(h, gpre, wg, wu, wd, gpost):
    t, d = h.shape
    row = lambda i, f: (i, 0)
    fixed = lambda i, f: (0, 0)
    return pl.pallas_call(
        _ffn_kernel,
        grid=(t // TM_ROW, D_FF // TF_FFN),
        in_specs=[
            pl.BlockSpec((TM_ROW, d), row),
            pl.BlockSpec((1, d), fixed),
            pl.BlockSpec((d, TF_FFN), lambda i, f: (0, f)),
            pl.BlockSpec((d, TF_FFN), lambda i, f: (0, f)),
            pl.BlockSpec((TF_FFN, d), lambda i, f: (f, 0)),
            pl.BlockSpec((1, d), fixed),
        ],
        out_specs=pl.BlockSpec((TM_ROW, d), row),
        out_shape=jax.ShapeDtypeStruct((t, d), F32),
        scratch_shapes=[pltpu.VMEM((TM_ROW, d), BF16), pltpu.VMEM((TM_ROW, d), F32)],
        compiler_params=_cparams(("parallel", "arbitrary")),
        name="ffn",
    )(h, gpre, wg, wu, wd, gpost)


def _rope_tables(pos_by_lane, freq_idx, dim, theta, active, first_half):
    freqs = jnp.power(jnp.float32(theta), -freq_idx.astype(F32) * 2.0 / dim)
    ang = pos_by_lane * freqs[None, :]
    cos = jnp.where(active[None, :], jnp.cos(ang), 1.0)
    sin = jnp.sin(ang)
    sin_a = jnp.where((active & first_half)[None, :], -sin, 0.0)
    sin_b = jnp.where((active & ~first_half)[None, :], sin, 0.0)
    return cos.astype(F32), sin_a.astype(F32), sin_b.astype(F32)


def _b_tables():
    lane = np.arange(LANES)
    half = B_HEAD_DIM // 2
    quarter = half // 2
    t = jnp.arange(SEQ)
    row = (t // GRID_W).astype(F32)
    col = (t % GRID_W).astype(F32)
    pos = jnp.where(jnp.asarray(lane < half)[None, :], row[:, None], col[:, None])
    return _rope_tables(pos, jnp.asarray(lane % quarter), half, B_ROPE_THETA,
                        jnp.asarray(np.ones(LANES, bool)), jnp.asarray((lane % half) < quarter))


def _c_tables():
    lane = np.arange(LANES)
    half = C_ROPE // 2
    active = (lane >= C_NOPE) & (lane < C_NOPE + C_ROPE)
    pos = jnp.broadcast_to(jnp.arange(SEQ, dtype=F32)[:, None], (SEQ, LANES))
    return _rope_tables(pos, jnp.asarray((lane - C_NOPE) % half), C_ROPE, C_ROPE_THETA,
                        jnp.asarray(active), jnp.asarray((lane - C_NOPE) < half))


def _alibi_slopes():
    n = A_NG * A_HEADS
    return jnp.asarray(2.0 ** (-8.0 * np.arange(1, n + 1) / n), dtype=F32)


def _mixer_a(h, gain, wqkv, wo, gpost):
    qkv = norm_matmul(h, gain, wqkv.astype(BF16))
    y = a_attention(qkv.reshape(BATCH, SEQ, A_QKV), _alibi_slopes())
    return out_proj(y.reshape(BATCH * SEQ, D_MODEL), wo.astype(BF16), gpost, h)


def _mixer_b(h, gain, wqkv, qnorm, knorm, wo, gpost):
    qt, k, vt = b_prep(h, gain, wqkv.astype(BF16), qnorm[None, :], knorm[None, :], _b_tables())
    y = b_attention(qt, k.reshape(BATCH, SEQ, -1), vt)
    return out_proj(y.reshape(BATCH * SEQ, D_MODEL), wo.astype(BF16), gpost, h)


def _mixer_c(h, gain, win, qnorm, kvnorm, wuq, wukv, wo, gpost):
    win_p = jnp.zeros((D_MODEL, C_IN_PAD), F32)
    win_p = win_p.at[:, :C_Q_LORA + C_KV_LORA].set(win[:, :C_Q_LORA + C_KV_LORA])
    kr0 = C_Q_LORA + C_KV_LORA + C_NOPE
    win_p = win_p.at[:, kr0:kr0 + C_ROPE].set(win[:, C_Q_LORA + C_KV_LORA:])
    wuq_p = jnp.pad(wuq.reshape(C_Q_LORA, C_HEADS, C_NOPE + C_ROPE),
                    ((0, 0), (0, 0), (0, LANES - C_NOPE - C_ROPE))).reshape(C_Q_LORA, C_HEADS * LANES)
    wukv3 = wukv.reshape(C_KV_LORA, C_HEADS, C_NOPE + C_V)
    wuk_p = jnp.pad(wukv3[:, :, :C_NOPE],
                    ((0, 0), (0, 0), (0, LANES - C_NOPE))).reshape(C_KV_LORA, C_HEADS * LANES)
    wuv = wukv3[:, :, C_NOPE:].reshape(C_KV_LORA, C_HEADS * C_V)
    qt, k, vt = c_prep(h, gain, win_p.astype(BF16), qnorm[None, :], kvnorm[None, :],
                       wuq_p.astype(BF16), wuk_p.astype(BF16), wuv.astype(BF16), _c_tables())
    y = c_attention(qt, k.reshape(BATCH, SEQ, -1), vt)
    return out_proj(y.reshape(BATCH * SEQ, D_MODEL), wo.astype(BF16), gpost, h)


def kernel(x, norm_mix_pre, norm_mix_post, norm_ffn_pre, norm_ffn_post, ffn_wg, ffn_wu, ffn_wd,
           a_wqkv, a_wo, b_wqkv, b_qnorm, b_knorm, b_wo,
           c_win, c_qnorm, c_kvnorm, c_wuq, c_wukv, c_wo):
    h = x.reshape(BATCH * SEQ, D_MODEL)
    for i in range(DEPTH):
        kind = i % N_MIXERS
        j = i // N_MIXERS
        gpre = norm_mix_pre[i][None, :]
        gpost = norm_mix_post[i][None, :]
        if kind == 0:
            h = _mixer_a(h, gpre, a_wqkv[j], a_wo[j], gpost)
        elif kind == 1:
            h = _mixer_b(h, gpre, b_wqkv[j], b_qnorm[j], b_knorm[j], b_wo[j], gpost)
        else:
            h = _mixer_c(h, gpre, c_win[j], c_qnorm[j], c_kvnorm[j], c_wuq[j], c_wukv[j], c_wo[j], gpost)
        h = ffn(h, norm_ffn_pre[i][None, :], ffn_wg[i].astype(BF16), ffn_wu[i].astype(BF16),
                ffn_wd[i].astype(BF16), norm_ffn_post[i][None, :])
    return h.reshape(BATCH, SEQ, D_MODEL)
```

```python
import functools

import numpy as np
import jax
import jax.numpy as jnp
from jax import lax
from jax.experimental import pallas as pl
from jax.experimental.pallas import tpu as pltpu

F32 = jnp.float32
BF16 = jnp.bfloat16

D_MODEL = 1024
BATCH = 2
SEQ = 8192
DEPTH = 4
N_MIXERS = 3
GRID_W = 64
D_FF = 2816
NORM_EPS = 1e-6
NEG_INF = -1e30

A_GROUPS = ((128, 1), (512, 4), (2048, 16))
A_NG = 3
A_HEAD_DIM = 64
A_HEADS = 16
A_QKV = 3 * A_NG * A_HEADS * A_HEAD_DIM
A_RADIUS = 64

B_HEAD_DIM = 128
B_HEADS = 8
B_KV_HEADS = 2
B_ROPE_THETA = 10000.0
B_QKV = (B_HEADS + 2 * B_KV_HEADS) * B_HEAD_DIM

C_HEADS = 16
C_Q_LORA = 384
C_KV_LORA = 256
C_NOPE = 64
C_ROPE = 32
C_V = 64
C_ROPE_THETA = 10000.0
C_IN_PAD = 768

LANES = 128
ONES_ROWS = 16
LOG2E = 1.4426950408889634
VMEM_LIMIT = 56 * 1024 * 1024

TM_PROJ = 1024
TN_PROJ = 1024
TM_ROW = 512
TF_FFN = 1408
TQ = 512
TK = 512
FLASH_UNROLL = 4
A_QB = 256
A_W = A_QB + 2 * A_RADIUS
A_UNROLL = 4


def _cparams(sem):
    return pltpu.CompilerParams(dimension_semantics=sem, vmem_limit_bytes=VMEM_LIMIT)


def _rms(x, gain):
    ms = jnp.mean(x * x, axis=-1, keepdims=True)
    return (x * lax.rsqrt(ms + NORM_EPS)) * gain


def _norm_matmul_kernel(x_ref, g_ref, w_ref, o_ref, xn_ref):
    @pl.when(pl.program_id(1) == 0)
    def _():
        xn_ref[...] = _rms(x_ref[...], g_ref[...]).astype(BF16)

    o_ref[...] = jnp.dot(xn_ref[...], w_ref[...], preferred_element_type=F32)


def norm_matmul(h, gain, w):
    t, d = h.shape
    n = w.shape[1]
    return pl.pallas_call(
        _norm_matmul_kernel,
        grid=(t // TM_PROJ, n // TN_PROJ),
        in_specs=[
            pl.BlockSpec((TM_PROJ, d), lambda i, j: (i, 0)),
            pl.BlockSpec((1, d), lambda i, j: (0, 0)),
            pl.BlockSpec((d, TN_PROJ), lambda i, j: (0, j)),
        ],
        out_specs=pl.BlockSpec((TM_PROJ, TN_PROJ), lambda i, j: (i, j)),
        out_shape=jax.ShapeDtypeStruct((t, n), F32),
        scratch_shapes=[pltpu.VMEM((TM_PROJ, d), BF16)],
        compiler_params=_cparams(("parallel", "arbitrary")),
        name="a_qkv_proj",
    )(h, gain, w)


def _a_attn_kernel(slopes_ref, q_ref, k_ref, v_ref, o_ref, m_ref, l_ref, acc_ref,
                   s_ref, mf_ref, bias_ref):
    hp = pl.program_id(1)
    g = pl.program_id(2)
    lane = lax.broadcasted_iota(jnp.int32, (A_QB, LANES), 1)
    lo = lane < A_HEAD_DIM
    head_lanes = (lo, jnp.logical_not(lo))
    lo_w = lax.broadcasted_iota(jnp.int32, (A_W, LANES), 1) < A_HEAD_DIM
    head_lanes_w = (lo_w, jnp.logical_not(lo_w))
    rel = (lax.broadcasted_iota(jnp.int32, (A_QB, A_W), 1)
           - lax.broadcasted_iota(jnp.int32, (A_QB, A_W), 0))

    def group_body(gi, d):
        cls_len = SEQ // d
        nblk = cls_len // A_QB
        nit = d * nblk

        for oi in range(3):
            dist = jnp.abs(rel - oi * A_RADIUS)
            for h2 in range(2):
                slope = slopes_ref[gi * A_HEADS + hp * 2 + h2] * (d * LOG2E)
                bias_ref[oi, h2] = jnp.where(dist <= A_RADIUS, -slope * dist.astype(F32), NEG_INF)

        def geometry(it):
            r = it // nblk
            i0 = (it % nblk) * A_QB
            kstart = jnp.clip(i0 - A_RADIUS, 0, cls_len - A_W)
            oi = (i0 - kstart) // A_RADIUS
            if d == 1:
                q_rows = pl.ds(pl.multiple_of(i0, A_QB), A_QB)
                k_rows = pl.ds(pl.multiple_of(kstart, A_RADIUS), A_W)
            else:
                q_rows = pl.ds(r + i0 * d, A_QB, stride=d)
                k_rows = pl.ds(r + kstart * d, A_W, stride=d)
            return q_rows, k_rows, oi

        def load_qk(it):
            q_rows, k_rows, oi = geometry(it)
            q = (q_ref[q_rows, :] * (A_HEAD_DIM ** -0.5 * LOG2E)).astype(BF16)
            return q, k_ref[k_rows, :].astype(BF16), oi

        def scores(qk, slot, h2):
            q, k, oi = qk
            qm = jnp.where(head_lanes[h2], q, jnp.zeros_like(q))
            s = lax.dot_general(qm, k, (((1,), (1,)), ((), ())), preferred_element_type=F32)
            s = s + bias_ref[oi, h2]
            s_ref[slot, h2] = s
            mf_ref[slot, h2] = jnp.broadcast_to(jnp.max(s, axis=1, keepdims=True), (A_QB, LANES))

        def weighted_values(v, slot, h2):
            va = jnp.where(head_lanes_w[h2], v, 1.0).astype(BF16)
            mfull = mf_ref[slot, h2]
            p = jnp.exp2(s_ref[slot, h2] - jnp.concatenate([mfull] * (A_W // LANES), axis=1))
            return jnp.dot(p.astype(BF16), va, preferred_element_type=F32)

        def merge(q_rows, slot, outs):
            m_blk = jnp.where(lo, mf_ref[slot, 0], mf_ref[slot, 1])
            pv_blk = jnp.where(lo, outs[0], outs[1])
            l_blk = pltpu.roll(jnp.where(lo, outs[1], outs[0]), A_HEAD_DIM, 1)
            if gi == 0:
                m_ref[q_rows, :] = m_blk
                l_ref[q_rows, :] = l_blk
                acc_ref[q_rows, :] = pv_blk
            else:
                m_old = m_ref[q_rows, :]
                m_new = jnp.maximum(m_old, m_blk)
                a_old = jnp.exp2(m_old - m_new)
                a_blk = jnp.exp2(m_blk - m_new)
                l_new = a_old * l_ref[q_rows, :] + a_blk * l_blk
                acc_new = a_old * acc_ref[q_rows, :] + a_blk * pv_blk
                if gi == A_NG - 1:
                    acc_ref[q_rows, :] = acc_new / l_new
                else:
                    m_ref[q_rows, :] = m_new
                    l_ref[q_rows, :] = l_new
                    acc_ref[q_rows, :] = acc_new

        qk0 = load_qk(0)
        for h2 in range(2):
            scores(qk0, 0, h2)

        def body(jj, carry):
            it0 = A_UNROLL * jj
            for u in range(A_UNROLL):
                nxt = it0 + u + 1
                if u == A_UNROLL - 1:
                    nxt = jnp.minimum(nxt, nit - 1)
                qk = load_qk(nxt)
                q_rows, k_rows, _ = geometry(it0 + u)
                v = v_ref[k_rows, :]
                outs = []
                for h2 in range(2):
                    scores(qk, (u + 1) % 2, h2)
                    outs.append(weighted_values(v, u % 2, h2))
                merge(q_rows, u % 2, outs)
            return carry

        lax.fori_loop(0, nit // A_UNROLL, body, 0)
        if gi == A_NG - 1:
            o_ref[...] = acc_ref[...].astype(o_ref.dtype)

    for gi, (_, d) in enumerate(A_GROUPS):
        pl.when(g == gi)(functools.partial(group_body, gi, d))


def a_attention(qkv, slopes):
    npairs = A_HEADS // 2

    def col(which):
        return lambda b, hp, g: (b, 0, (which * A_NG + g) * npairs + hp)

    return pl.pallas_call(
        _a_attn_kernel,
        grid=(BATCH, npairs, A_NG),
        in_specs=[
            pl.BlockSpec(memory_space=pltpu.SMEM),
            pl.BlockSpec((None, SEQ, LANES), col(0)),
            pl.BlockSpec((None, SEQ, LANES), col(1)),
            pl.BlockSpec((None, SEQ, LANES), col(2)),
        ],
        out_specs=pl.BlockSpec((None, SEQ, LANES), lambda b, hp, g: (b, 0, hp)),
        out_shape=jax.ShapeDtypeStruct((BATCH, SEQ, D_MODEL), BF16),
        scratch_shapes=[
            pltpu.VMEM((SEQ, LANES), F32),
            pltpu.VMEM((SEQ, LANES), F32),
            pltpu.VMEM((SEQ, LANES), F32),
            pltpu.VMEM((2, 2, A_QB, A_W), F32),
            pltpu.VMEM((2, 2, A_QB, LANES), F32),
            pltpu.VMEM((3, 2, A_QB, A_W), F32),
        ],
        compiler_params=_cparams(("parallel", "parallel", "arbitrary")),
        name="a_attention",
    )(slopes, qkv, qkv, qkv)


def _flash_kernel(qt_ref, k_ref, vt_ref, o_ref, m_ref, acc_ref, s_ref, mc_ref, *, nh, dv, shared_kv):
    m_ref[...] = jnp.full(m_ref.shape, -jnp.inf, F32)
    acc_ref[...] = jnp.zeros(acc_ref.shape, F32)
    ones = jnp.ones((ONES_ROWS, TK), BF16)
    nchunk = SEQ // TK

    def scores(chunk, slot, h):
        rows = pl.ds(pl.multiple_of(chunk * TK, TK), TK)
        k = k_ref[rows, :] if shared_kv else k_ref[rows, h * LANES:(h + 1) * LANES]
        st = jnp.dot(k, qt_ref[h * LANES:(h + 1) * LANES, :], preferred_element_type=F32)
        s_ref[slot, h] = st
        mc_ref[slot, h] = jnp.max(st, axis=0, keepdims=True)

    def update(chunk, slot, h):
        rows = pl.ds(pl.multiple_of(chunk * TK, TK), TK)
        vt = vt_ref[:, rows] if shared_kv else vt_ref[h * dv:(h + 1) * dv, rows]
        m_prev = m_ref[h]
        m_new = jnp.maximum(m_prev, mc_ref[slot, h])
        alpha = jnp.exp2(m_prev - m_new)
        pt = jnp.exp2(s_ref[slot, h] - m_new).astype(BF16)
        pv = jnp.dot(jnp.concatenate([vt, ones], axis=0), pt, preferred_element_type=F32)
        acc_ref[h] = alpha * acc_ref[h] + pv
        m_ref[h] = m_new

    for h in range(nh):
        scores(0, 0, h)

    def body(jj, carry):
        j = FLASH_UNROLL * jj
        for u in range(FLASH_UNROLL):
            nxt = j + u + 1
            if u == FLASH_UNROLL - 1:
                nxt = jnp.minimum(nxt, nchunk - 1)
            for h in range(nh):
                scores(nxt, (u + 1) % 2, h)
                update(j + u, u % 2, h)
        return carry

    lax.fori_loop(0, nchunk // FLASH_UNROLL, body, 0)

    outs = []
    for h in range(nh):
        acc = acc_ref[h]
        outs.append(acc[:dv, :] / acc[dv:dv + 1, :])
    ot = outs[0] if nh == 1 else jnp.concatenate(outs, axis=0)
    o_ref[...] = ot.T.astype(o_ref.dtype)


def _flash_call(kernel, grid, nh, dv, in_specs, out_spec, args, name):
    return pl.pallas_call(
        kernel,
        grid=grid,
        in_specs=in_specs,
        out_specs=out_spec,
        out_shape=jax.ShapeDtypeStruct((BATCH, SEQ, D_MODEL), BF16),
        scratch_shapes=[
            pltpu.VMEM((nh, 1, TQ), F32),
            pltpu.VMEM((nh, dv + ONES_ROWS, TQ), F32),
            pltpu.VMEM((2, nh, TK, TQ), F32),
            pltpu.VMEM((2, nh, 1, TQ), F32),
        ],
        compiler_params=_cparams(("parallel",) * (len(grid) - 1) + ("arbitrary",)),
        name=name,
    )(*args)


def b_attention(qt, k, vt):
    grp = B_HEADS // B_KV_HEADS
    nh = 2
    kernel = functools.partial(_flash_kernel, nh=nh, dv=B_HEAD_DIM, shared_kv=True)
    in_specs = [
        pl.BlockSpec((None, nh * LANES, TQ), lambda b, kv, g, i: (b, kv * (grp // nh) + g, i)),
        pl.BlockSpec((None, SEQ, LANES), lambda b, kv, g, i: (b, 0, kv)),
        pl.BlockSpec((None, LANES, SEQ), lambda b, kv, g, i: (b, kv, 0)),
    ]
    out_spec = pl.BlockSpec((None, TQ, nh * LANES), lambda b, kv, g, i: (b, i, kv * (grp // nh) + g))
    grid = (BATCH, B_KV_HEADS, grp // nh, SEQ // TQ)
    return _flash_call(kernel, grid, nh, B_HEAD_DIM, in_specs, out_spec, (qt, k, vt), "b_attention")


def c_attention(qt, k, vt):
    nh = 2
    kernel = functools.partial(_flash_kernel, nh=nh, dv=C_V, shared_kv=False)
    in_specs = [
        pl.BlockSpec((None, nh * LANES, TQ), lambda b, p, i: (b, p, i)),
        pl.BlockSpec((None, SEQ, nh * LANES), lambda b, p, i: (b, 0, p)),
        pl.BlockSpec((None, nh * C_V, SEQ), lambda b, p, i: (b, p, 0)),
    ]
    out_spec = pl.BlockSpec((None, TQ, nh * C_V), lambda b, p, i: (b, i, p))
    grid = (BATCH, C_HEADS // nh, SEQ // TQ)
    return _flash_call(kernel, grid, nh, C_V, in_specs, out_spec, (qt, k, vt), "c_attention")


def _rope_lanes(x, cos, sin_a, sin_b, shift):
    return (x * cos + pltpu.roll(x, LANES - shift, 1) * sin_a + pltpu.roll(x, shift, 1) * sin_b)


def _b_prep_kernel(h_ref, g_ref, w_ref, qn_ref, kn_ref, cos_ref, sa_ref, sb_ref,
                   qt_ref, k_ref, vt_ref):
    xn = _rms(h_ref[...], g_ref[...]).astype(BF16)
    qkv = jnp.dot(xn, w_ref[...], preferred_element_type=F32)
    cos, sa, sb = cos_ref[...], sa_ref[...], sb_ref[...]
    for hd in range(B_HEADS + 2 * B_KV_HEADS):
        x = qkv[:, hd * LANES:(hd + 1) * LANES]
        if hd < B_HEADS + B_KV_HEADS:
            x = _rms(x, qn_ref[...] if hd < B_HEADS else kn_ref[...])
            x = _rope_lanes(x, cos, sa, sb, B_HEAD_DIM // 4)
        if hd < B_HEADS:
            x = x * (B_HEAD_DIM ** -0.5 * LOG2E)
            qt_ref[hd * LANES:(hd + 1) * LANES, :] = x.T.astype(BF16)
        elif hd < B_HEADS + B_KV_HEADS:
            j = hd - B_HEADS
            k_ref[:, j * LANES:(j + 1) * LANES] = x.astype(BF16)
        else:
            j = hd - B_HEADS - B_KV_HEADS
            vt_ref[j * LANES:(j + 1) * LANES, :] = x.T.astype(BF16)


def b_prep(h, gain, w, qnorm, knorm, tables):
    t, d = h.shape
    nrow = SEQ // TM_ROW
    row = lambda i: (i, 0)
    fixed = lambda i: (0, 0)
    tab = lambda i: (i % nrow, 0)
    colblk = lambda i: (i // nrow, 0, i % nrow)
    nq = B_HEADS * B_HEAD_DIM
    nkv = B_KV_HEADS * B_HEAD_DIM
    return pl.pallas_call(
        _b_prep_kernel,
        grid=(t // TM_ROW,),
        in_specs=[
            pl.BlockSpec((TM_ROW, d), row),
            pl.BlockSpec((1, d), fixed),
            pl.BlockSpec((d, B_QKV), fixed),
            pl.BlockSpec((1, B_HEAD_DIM), fixed),
            pl.BlockSpec((1, B_HEAD_DIM), fixed),
            pl.BlockSpec((TM_ROW, LANES), tab),
            pl.BlockSpec((TM_ROW, LANES), tab),
            pl.BlockSpec((TM_ROW, LANES), tab),
        ],
        out_specs=[
            pl.BlockSpec((None, nq, TM_ROW), colblk),
            pl.BlockSpec((TM_ROW, nkv), row),
            pl.BlockSpec((None, nkv, TM_ROW), colblk),
        ],
        out_shape=[
            jax.ShapeDtypeStruct((BATCH, nq, SEQ), BF16),
            jax.ShapeDtypeStruct((t, nkv), BF16),
            jax.ShapeDtypeStruct((BATCH, nkv, SEQ), BF16),
        ],
        compiler_params=_cparams(("parallel",)),
        name="b_prep",
    )(h, gain, w, qnorm, knorm, *tables)


def _c_prep_kernel(h_ref, g_ref, win_ref, qn_ref, kvn_ref, wuq_ref, wuk_ref, wuv_ref,
                   cos_ref, sa_ref, sb_ref, q_ref, k_ref, v_ref):
    xn = _rms(h_ref[...], g_ref[...]).astype(BF16)
    c = jnp.dot(xn, win_ref[...], preferred_element_type=F32)
    cq = _rms(c[:, :C_Q_LORA], qn_ref[...]).astype(BF16)
    ckv = _rms(c[:, C_Q_LORA:C_Q_LORA + C_KV_LORA], kvn_ref[...]).astype(BF16)
    cos, sa, sb = cos_ref[...], sa_ref[...], sb_ref[...]
    kr = _rope_lanes(c[:, C_Q_LORA + C_KV_LORA:], cos, sa, sb, C_ROPE // 2)
    q = jnp.dot(cq, wuq_ref[...], preferred_element_type=F32)
    kn = jnp.dot(ckv, wuk_ref[...], preferred_element_type=F32)
    v = jnp.dot(ckv, wuv_ref[...], preferred_element_type=F32)
    scale = (C_NOPE + C_ROPE) ** -0.5 * LOG2E
    for hd in range(C_HEADS):
        cols = slice(hd * LANES, (hd + 1) * LANES)
        qh = _rope_lanes(q[:, cols], cos, sa, sb, C_ROPE // 2) * scale
        q_ref[cols, :] = qh.T.astype(BF16)
        k_ref[:, cols] = (kn[:, cols] + kr).astype(BF16)
    for blk in range(C_HEADS * C_V // LANES):
        cols = slice(blk * LANES, (blk + 1) * LANES)
        v_ref[cols, :] = v[:, cols].T.astype(BF16)


def c_prep(h, gain, win, qnorm, kvnorm, wuq, wuk, wuv, tables):
    t, d = h.shape
    nrow = SEQ // TM_ROW
    row = lambda i: (i, 0)
    fixed = lambda i: (0, 0)
    tab = lambda i: (i % nrow, 0)
    colblk = lambda i: (i // nrow, 0, i % nrow)
    hq = C_HEADS * LANES
    return pl.pallas_call(
        _c_prep_kernel,
        grid=(t // TM_ROW,),
        in_specs=[
            pl.BlockSpec((TM_ROW, d), row),
            pl.BlockSpec((1, d), fixed),
            pl.BlockSpec((d, C_IN_PAD), fixed),
            pl.BlockSpec((1, C_Q_LORA), fixed),
            pl.BlockSpec((1, C_KV_LORA), fixed),
            pl.BlockSpec((C_Q_LORA, hq), fixed),
            pl.BlockSpec((C_KV_LORA, hq), fixed),
            pl.BlockSpec((C_KV_LORA, C_HEADS * C_V), fixed),
            pl.BlockSpec((TM_ROW, LANES), tab),
            pl.BlockSpec((TM_ROW, LANES), tab),
            pl.BlockSpec((TM_ROW, LANES), tab),
        ],
        out_specs=[
            pl.BlockSpec((None, hq, TM_ROW), colblk),
            pl.BlockSpec((TM_ROW, hq), row),
            pl.BlockSpec((None, C_HEADS * C_V, TM_ROW), colblk),
        ],
        out_shape=[
            jax.ShapeDtypeStruct((BATCH, hq, SEQ), BF16),
            jax.ShapeDtypeStruct((t, hq), BF16),
            jax.ShapeDtypeStruct((BATCH, C_HEADS * C_V, SEQ), BF16),
        ],
        compiler_params=_cparams(("parallel",)),
        name="c_prep",
    )(h, gain, win, qnorm, kvnorm, wuq, wuk, wuv, *tables)


def _out_proj_kernel(y_ref, w_ref, g_ref, h_ref, o_ref):
    z = jnp.dot(y_ref[...], w_ref[...], preferred_element_type=F32)
    o_ref[...] = h_ref[...] + _rms(z, g_ref[...])


def out_proj(y, w, gain, h):
    t, d = h.shape
    row = lambda i: (i, 0)
    fixed = lambda i: (0, 0)
    return pl.pallas_call(
        _out_proj_kernel,
        grid=(t // TM_ROW,),
        in_specs=[
            pl.BlockSpec((TM_ROW, d), row),
            pl.BlockSpec((d, d), fixed),
            pl.BlockSpec((1, d), fixed),
            pl.BlockSpec((TM_ROW, d), row),
        ],
        out_specs=pl.BlockSpec((TM_ROW, d), row),
        out_shape=jax.ShapeDtypeStruct((t, d), F32),
        compiler_params=_cparams(("parallel",)),
        name="out_proj",
    )(y, w, gain, h)


def _ffn_kernel(h_ref, gpre_ref, wg_ref, wu_ref, wd_ref, gpost_ref, o_ref, xn_ref, acc_ref):
    f = pl.program_id(1)

    @pl.when(f == 0)
    def _():
        xn_ref[...] = _rms(h_ref[...], gpre_ref[...]).astype(BF16)

    xn = xn_ref[...]
    gate = jnp.dot(xn, wg_ref[...], preferred_element_type=F32)
    up = jnp.dot(xn, wu_ref[...], preferred_element_type=F32)
    act = (gate * jax.nn.sigmoid(gate)) * up
    part = jnp.dot(act.astype(BF16), wd_ref[...], preferred_element_type=F32)

    @pl.when(f == 0)
    def _():
        acc_ref[...] = part

    @pl.when(f > 0)
    def _():
        acc_ref[...] += part

    @pl.when(f == pl.num_programs(1) - 1)
    def _():
        o_ref[...] = h_ref[...] + _rms(acc_ref[...], gpost_ref[...])


def ffn(h, gpre, wg, wu, wd, gpost):
    t, d = h.shape
    row = lambda i, f: (i, 0)
    fixed = lambda i, f: (0, 0)
    return pl.pallas_call(
        _ffn_kernel,
        grid=(t // TM_ROW, D_FF // TF_FFN),
        in_specs=[
            pl.BlockSpec((TM_ROW, d), row),
            pl.BlockSpec((1, d), fixed),
            pl.BlockSpec((d, TF_FFN), lambda i, f: (0, f)),
            pl.BlockSpec((d, TF_FFN), lambda i, f: (0, f)),
            pl.BlockSpec((TF_FFN, d), lambda i, f: (f, 0)),
            pl.BlockSpec((1, d), fixed),
        ],
        out_specs=pl.BlockSpec((TM_ROW, d), row),
        out_shape=jax.ShapeDtypeStruct((t, d), F32),
        scratch_shapes=[pltpu.VMEM((TM_ROW, d), BF16), pltpu.VMEM((TM_ROW, d), F32)],
        compiler_params=_cparams(("parallel", "arbitrary")),
        name="ffn",
    )(h, gpre, wg, wu, wd, gpost)


def _rope_tables(pos_by_lane, freq_idx, dim, theta, active, first_half):
    freqs = jnp.power(jnp.float32(theta), -freq_idx.astype(F32) * 2.0 / dim)
    ang = pos_by_lane * freqs[None, :]
    cos = jnp.where(active[None, :], jnp.cos(ang), 1.0)
    sin = jnp.sin(ang)
    sin_a = jnp.where((active & first_half)[None, :], -sin, 0.0)
    sin_b = jnp.where((active & ~first_half)[None, :], sin, 0.0)
    return cos.astype(F32), sin_a.astype(F32), sin_b.astype(F32)


def _b_tables():
    lane = np.arange(LANES)
    half = B_HEAD_DIM // 2
    quarter = half // 2
    t = jnp.arange(SEQ)
    row = (t // GRID_W).astype(F32)
    col = (t % GRID_W).astype(F32)
    pos = jnp.where(jnp.asarray(lane < half)[None, :], row[:, None], col[:, None])
    return _rope_tables(pos, jnp.asarray(lane % quarter), half, B_ROPE_THETA,
                        jnp.asarray(np.ones(LANES, bool)), jnp.asarray((lane % half) < quarter))


def _c_tables():
    lane = np.arange(LANES)
    half = C_ROPE // 2
    active = (lane >= C_NOPE) & (lane < C_NOPE + C_ROPE)
    pos = jnp.broadcast_to(jnp.arange(SEQ, dtype=F32)[:, None], (SEQ, LANES))
    return _rope_tables(pos, jnp.asarray((lane - C_NOPE) % half), C_ROPE, C_ROPE_THETA,
                        jnp.asarray(active), jnp.asarray((lane - C_NOPE) < half))


def _alibi_slopes():
    n = A_NG * A_HEADS
    return jnp.asarray(2.0 ** (-8.0 * np.arange(1, n + 1) / n), dtype=F32)


def _mixer_a(h, gain, wqkv, wo, gpost):
    qkv = norm_matmul(h, gain, wqkv.astype(BF16))
    y = a_attention(qkv.reshape(BATCH, SEQ, A_QKV), _alibi_slopes())
    return out_proj(y.reshape(BATCH * SEQ, D_MODEL), wo.astype(BF16), gpost, h)


def _mixer_b(h, gain, wqkv, qnorm, knorm, wo, gpost):
    qt, k, vt = b_prep(h, gain, wqkv.astype(BF16), qnorm[None, :], knorm[None, :], _b_tables())
    y = b_attention(qt, k.reshape(BATCH, SEQ, -1), vt)
    return out_proj(y.reshape(BATCH * SEQ, D_MODEL), wo.astype(BF16), gpost, h)


def _mixer_c(h, gain, win, qnorm, kvnorm, wuq, wukv, wo, gpost):
    win_p = jnp.zeros((D_MODEL, C_IN_PAD), F32)
    win_p = win_p.at[:, :C_Q_LORA + C_KV_LORA].set(win[:, :C_Q_LORA + C_KV_LORA])
    kr0 = C_Q_LORA + C_KV_LORA + C_NOPE
    win_p = win_p.at[:, kr0:kr0 + C_ROPE].set(win[:, C_Q_LORA + C_KV_LORA:])
    wuq_p = jnp.pad(wuq.reshape(C_Q_LORA, C_HEADS, C_NOPE + C_ROPE),
                    ((0, 0), (0, 0), (0, LANES - C_NOPE - C_ROPE))).reshape(C_Q_LORA, C_HEADS * LANES)
    wukv3 = wukv.reshape(C_KV_LORA, C_HEADS, C_NOPE + C_V)
    wuk_p = jnp.pad(wukv3[:, :, :C_NOPE],
                    ((0, 0), (0, 0), (0, LANES - C_NOPE))).reshape(C_KV_LORA, C_HEADS * LANES)
    wuv = wukv3[:, :, C_NOPE:].reshape(C_KV_LORA, C_HEADS * C_V)
    qt, k, vt = c_prep(h, gain, win_p.astype(BF16), qnorm[None, :], kvnorm[None, :],
                       wuq_p.astype(BF16), wuk_p.astype(BF16), wuv.astype(BF16), _c_tables())
    y = c_attention(qt, k.reshape(BATCH, SEQ, -1), vt)
    return out_proj(y.reshape(BATCH * SEQ, D_MODEL), wo.astype(BF16), gpost, h)


def kernel(x, norm_mix_pre, norm_mix_post, norm_ffn_pre, norm_ffn_post, ffn_wg, ffn_wu, ffn_wd,
           a_wqkv, a_wo, b_wqkv, b_qnorm, b_knorm, b_wo,
           c_win, c_qnorm, c_kvnorm, c_wuq, c_wukv, c_wo):
    h = x.reshape(BATCH * SEQ, D_MODEL)
    for i in range(DEPTH):
        kind = i % N_MIXERS
        j = i // N_MIXERS
        gpre = norm_mix_pre[i][None, :]
        gpost = norm_mix_post[i][None, :]
        if kind == 0:
            h = _mixer_a(h, gpre, a_wqkv[j], a_wo[j], gpost)
        elif kind == 1:
            h = _mixer_b(h, gpre, b_wqkv[j], b_qnorm[j], b_knorm[j], b_wo[j], gpost)
        else:
            h = _mixer_c(h, gpre, c_win[j], c_qnorm[j], c_kvnorm[j], c_wuq[j], c_wukv[j], c_wo[j], gpost)
        h = ffn(h, norm_ffn_pre[i][None, :], ffn_wg[i].astype(BF16), ffn_wu[i].astype(BF16),
                ffn_wd[i].astype(BF16), norm_ffn_post[i][None, :])
    return h.reshape(BATCH, SEQ, D_MODEL)
```

```python
import functools

import numpy as np
import jax
import jax.numpy as jnp
from jax import lax
from jax.experimental import pallas as pl
from jax.experimental.pallas import tpu as pltpu

F32 = jnp.float32
BF16 = jnp.bfloat16

D_MODEL = 1024
BATCH = 2
SEQ = 8192
DEPTH = 4
N_MIXERS = 3
GRID_W = 64
D_FF = 2816
NORM_EPS = 1e-6
NEG_INF = -1e30

A_GROUPS = ((128, 1), (512, 4), (2048, 16))
A_NG = 3
A_HEAD_DIM = 64
A_HEADS = 16
A_QKV = 3 * A_NG * A_HEADS * A_HEAD_DIM
A_RADIUS = 64

B_HEAD_DIM = 128
B_HEADS = 8
B_KV_HEADS = 2
B_ROPE_THETA = 10000.0
B_QKV = (B_HEADS + 2 * B_KV_HEADS) * B_HEAD_DIM

C_HEADS = 16
C_Q_LORA = 384
C_KV_LORA = 256
C_NOPE = 64
C_ROPE = 32
C_V = 64
C_ROPE_THETA = 10000.0
C_IN_PAD = 768

LANES = 128
ONES_ROWS = 16
LOG2E = 1.4426950408889634
VMEM_LIMIT = 56 * 1024 * 1024

TM_PROJ = 1024
TN_PROJ = 1024
TM_ROW = 512
TF_FFN = 1408
TQ = 512
TK = 512
FLASH_UNROLL = 4
A_QB = 256
A_W = A_QB + 2 * A_RADIUS
A_UNROLL = 4


def _cparams(sem):
    return pltpu.CompilerParams(dimension_semantics=sem, vmem_limit_bytes=VMEM_LIMIT)


def _rms(x, gain):
    ms = jnp.mean(x * x, axis=-1, keepdims=True)
    return (x * lax.rsqrt(ms + NORM_EPS)) * gain


def _norm_matmul_kernel(x_ref, g_ref, w_ref, o_ref, xn_ref):
    @pl.when(pl.program_id(1) == 0)
    def _():
        xn_ref[...] = _rms(x_ref[...], g_ref[...]).astype(BF16)

    o_ref[...] = jnp.dot(xn_ref[...], w_ref[...], preferred_element_type=F32)


def norm_matmul(h, gain, w):
    t, d = h.shape
    n = w.shape[1]
    return pl.pallas_call(
        _norm_matmul_kernel,
        grid=(t // TM_PROJ, n // TN_PROJ),
        in_specs=[
            pl.BlockSpec((TM_PROJ, d), lambda i, j: (i, 0)),
            pl.BlockSpec((1, d), lambda i, j: (0, 0)),
            pl.BlockSpec((d, TN_PROJ), lambda i, j: (0, j)),
        ],
        out_specs=pl.BlockSpec((TM_PROJ, TN_PROJ), lambda i, j: (i, j)),
        out_shape=jax.ShapeDtypeStruct((t, n), F32),
        scratch_shapes=[pltpu.VMEM((TM_PROJ, d), BF16)],
        compiler_params=_cparams(("parallel", "arbitrary")),
        name="a_qkv_proj",
    )(h, gain, w)


def _a_attn_kernel(slopes_ref, q_ref, k_ref, v_ref, o_ref, m_ref, l_ref, acc_ref,
                   s_ref, mf_ref, bias_ref):
    hp = pl.program_id(1)
    g = pl.program_id(2)
    lane = lax.broadcasted_iota(jnp.int32, (A_QB, LANES), 1)
    lo = lane < A_HEAD_DIM
    head_lanes = (lo, jnp.logical_not(lo))
    lo_w = lax.broadcasted_iota(jnp.int32, (A_W, LANES), 1) < A_HEAD_DIM
    head_lanes_w = (lo_w, jnp.logical_not(lo_w))
    rel = (lax.broadcasted_iota(jnp.int32, (A_QB, A_W), 1)
           - lax.broadcasted_iota(jnp.int32, (A_QB, A_W), 0))

    def group_body(gi, d):
        cls_len = SEQ // d
        nblk = cls_len // A_QB
        nit = d * nblk

        for oi in range(3):
            dist = jnp.abs(rel - oi * A_RADIUS)
            for h2 in range(2):
                slope = slopes_ref[gi * A_HEADS + hp * 2 + h2] * (d * LOG2E)
                bias_ref[oi, h2] = jnp.where(dist <= A_RADIUS, -slope * dist.astype(F32), NEG_INF)

        def geometry(it):
            r = it // nblk
            i0 = (it % nblk) * A_QB
            if isinstance(it, int):
                kstart = min(max(i0 - A_RADIUS, 0), cls_len - A_W)
            else:
                kstart = jnp.clip(i0 - A_RADIUS, 0, cls_len - A_W)
            oi = (i0 - kstart) // A_RADIUS
            if d == 1 and isinstance(it, int):
                q_rows = pl.ds(i0, A_QB)
                k_rows = pl.ds(kstart, A_W)
            elif d == 1:
                q_rows = pl.ds(pl.multiple_of(i0, A_QB), A_QB)
                k_rows = pl.ds(pl.multiple_of(kstart, A_RADIUS), A_W)
            else:
                q_rows = pl.ds(r + i0 * d, A_QB, stride=d)
                k_rows = pl.ds(r + kstart * d, A_W, stride=d)
            return q_rows, k_rows, oi

        def load_qk(it):
            q_rows, k_rows, oi = geometry(it)
            q = (q_ref[q_rows, :] * (A_HEAD_DIM ** -0.5 * LOG2E)).astype(BF16)
            return q, k_ref[k_rows, :].astype(BF16), oi

        def scores(qk, slot, h2):
            q, k, oi = qk
            qm = jnp.where(head_lanes[h2], q, jnp.zeros_like(q))
            s = lax.dot_general(qm, k, (((1,), (1,)), ((), ())), preferred_element_type=F32)
            s = s + bias_ref[oi, h2]
            s_ref[slot, h2] = s
            mf_ref[slot, h2] = jnp.broadcast_to(jnp.max(s, axis=1, keepdims=True), (A_QB, LANES))

        def weighted_values(v, slot, h2):
            va = jnp.where(head_lanes_w[h2], v, 1.0).astype(BF16)
            mfull = mf_ref[slot, h2]
            p = jnp.exp2(s_ref[slot, h2] - jnp.concatenate([mfull] * (A_W // LANES), axis=1))
            return jnp.dot(p.astype(BF16), va, preferred_element_type=F32)

        def merge(q_rows, slot, outs):
            m_blk = jnp.where(lo, mf_ref[slot, 0], mf_ref[slot, 1])
            pv_blk = jnp.where(lo, outs[0], outs[1])
            l_blk = pltpu.roll(jnp.where(lo, outs[1], outs[0]), A_HEAD_DIM, 1)
            if gi == 0:
                m_ref[q_rows, :] = m_blk
                l_ref[q_rows, :] = l_blk
                acc_ref[q_rows, :] = pv_blk
            else:
                m_old = m_ref[q_rows, :]
                m_new = jnp.maximum(m_old, m_blk)
                a_old = jnp.exp2(m_old - m_new)
                a_blk = jnp.exp2(m_blk - m_new)
                l_new = a_old * l_ref[q_rows, :] + a_blk * l_blk
                acc_new = a_old * acc_ref[q_rows, :] + a_blk * pv_blk
                if gi == A_NG - 1:
                    acc_ref[q_rows, :] = acc_new / l_new
                else:
                    m_ref[q_rows, :] = m_new
                    l_ref[q_rows, :] = l_new
                    acc_ref[q_rows, :] = acc_new

        qk0 = load_qk(0)
        for h2 in range(2):
            scores(qk0, 0, h2)

        def blocks(it0, last):
            for u in range(A_UNROLL):
                has_next = not (last and u == A_UNROLL - 1)
                qk = load_qk(it0 + u + 1) if has_next else None
                q_rows, k_rows, _ = geometry(it0 + u)
                v = v_ref[k_rows, :]
                outs = []
                for h2 in range(2):
                    if has_next:
                        scores(qk, (u + 1) % 2, h2)
                    outs.append(weighted_values(v, u % 2, h2))
                merge(q_rows, u % 2, outs)

        def body(jj, carry):
            blocks(A_UNROLL * jj, False)
            return carry

        lax.fori_loop(0, nit // A_UNROLL - 1, body, 0)
        blocks(nit - A_UNROLL, True)
        if gi == A_NG - 1:
            o_ref[...] = acc_ref[...].astype(o_ref.dtype)

    for gi, (_, d) in enumerate(A_GROUPS):
        pl.when(g == gi)(functools.partial(group_body, gi, d))


def a_attention(qkv, slopes):
    npairs = A_HEADS // 2

    def col(which):
        return lambda b, hp, g: (b, 0, (which * A_NG + g) * npairs + hp)

    return pl.pallas_call(
        _a_attn_kernel,
        grid=(BATCH, npairs, A_NG),
        in_specs=[
            pl.BlockSpec(memory_space=pltpu.SMEM),
            pl.BlockSpec((None, SEQ, LANES), col(0)),
            pl.BlockSpec((None, SEQ, LANES), col(1)),
            pl.BlockSpec((None, SEQ, LANES), col(2)),
        ],
        out_specs=pl.BlockSpec((None, SEQ, LANES), lambda b, hp, g: (b, 0, hp)),
        out_shape=jax.ShapeDtypeStruct((BATCH, SEQ, D_MODEL), BF16),
        scratch_shapes=[
            pltpu.VMEM((SEQ, LANES), F32),
            pltpu.VMEM((SEQ, LANES), F32),
            pltpu.VMEM((SEQ, LANES), F32),
            pltpu.VMEM((2, 2, A_QB, A_W), F32),
            pltpu.VMEM((2, 2, A_QB, LANES), F32),
            pltpu.VMEM((3, 2, A_QB, A_W), F32),
        ],
        compiler_params=_cparams(("parallel", "parallel", "arbitrary")),
        name="a_attention",
    )(slopes, qkv, qkv, qkv)


def _flash_kernel(qt_ref, k_ref, vt_ref, o_ref, m_ref, acc_ref, s_ref, mc_ref, *, nh, dv, shared_kv):
    m_ref[...] = jnp.full(m_ref.shape, -jnp.inf, F32)
    acc_ref[...] = jnp.zeros(acc_ref.shape, F32)
    ones = jnp.ones((ONES_ROWS, TK), BF16)
    nchunk = SEQ // TK

    def chunk_rows(chunk):
        start = chunk * TK
        return pl.ds(start if isinstance(start, int) else pl.multiple_of(start, TK), TK)

    def scores(chunk, slot, h):
        rows = chunk_rows(chunk)
        k = k_ref[rows, :] if shared_kv else k_ref[rows, h * LANES:(h + 1) * LANES]
        st = jnp.dot(k, qt_ref[h * LANES:(h + 1) * LANES, :], preferred_element_type=F32)
        s_ref[slot, h] = st
        mc_ref[slot, h] = jnp.max(st, axis=0, keepdims=True)

    def update(chunk, slot, h):
        rows = chunk_rows(chunk)
        vt = vt_ref[:, rows] if shared_kv else vt_ref[h * dv:(h + 1) * dv, rows]
        m_prev = m_ref[h]
        m_new = jnp.maximum(m_prev, mc_ref[slot, h])
        alpha = jnp.exp2(m_prev - m_new)
        pt = jnp.exp2(s_ref[slot, h] - m_new).astype(BF16)
        pv = jnp.dot(jnp.concatenate([vt, ones], axis=0), pt, preferred_element_type=F32)
        acc_ref[h] = alpha * acc_ref[h] + pv
        m_ref[h] = m_new

    for h in range(nh):
        scores(0, 0, h)

    def chunks(j, last):
        for u in range(FLASH_UNROLL):
            for h in range(nh):
                if not (last and u == FLASH_UNROLL - 1):
                    scores(j + u + 1, (u + 1) % 2, h)
                update(j + u, u % 2, h)

    def body(jj, carry):
        chunks(FLASH_UNROLL * jj, False)
        return carry

    lax.fori_loop(0, nchunk // FLASH_UNROLL - 1, body, 0)
    chunks(nchunk - FLASH_UNROLL, True)

    outs = []
    for h in range(nh):
        acc = acc_ref[h]
        outs.append(acc[:dv, :] / acc[dv:dv + 1, :])
    ot = outs[0] if nh == 1 else jnp.concatenate(outs, axis=0)
    o_ref[...] = ot.T.astype(o_ref.dtype)


def _flash_call(kernel, grid, nh, dv, in_specs, out_spec, args, name):
    return pl.pallas_call(
        kernel,
        grid=grid,
        in_specs=in_specs,
        out_specs=out_spec,
        out_shape=jax.ShapeDtypeStruct((BATCH, SEQ, D_MODEL), BF16),
        scratch_shapes=[
            pltpu.VMEM((nh, 1, TQ), F32),
            pltpu.VMEM((nh, dv + ONES_ROWS, TQ), F32),
            pltpu.VMEM((2, nh, TK, TQ), F32),
            pltpu.VMEM((2, nh, 1, TQ), F32),
        ],
        compiler_params=_cparams(("parallel",) * (len(grid) - 1) + ("arbitrary",)),
        name=name,
    )(*args)


def b_attention(qt, k, vt):
    grp = B_HEADS // B_KV_HEADS
    nh = 2
    kernel = functools.partial(_flash_kernel, nh=nh, dv=B_HEAD_DIM, shared_kv=True)
    in_specs = [
        pl.BlockSpec((None, nh * LANES, TQ), lambda b, kv, g, i: (b, kv * (grp // nh) + g, i)),
        pl.BlockSpec((None, SEQ, LANES), lambda b, kv, g, i: (b, 0, kv)),
        pl.BlockSpec((None, LANES, SEQ), lambda b, kv, g, i: (b, kv, 0)),
    ]
    out_spec = pl.BlockSpec((None, TQ, nh * LANES), lambda b, kv, g, i: (b, i, kv * (grp // nh) + g))
    grid = (BATCH, B_KV_HEADS, grp // nh, SEQ // TQ)
    return _flash_call(kernel, grid, nh, B_HEAD_DIM, in_specs, out_spec, (qt, k, vt), "b_attention")


def c_attention(qt, k, vt):
    nh = 2
    kernel = functools.partial(_flash_kernel, nh=nh, dv=C_V, shared_kv=False)
    in_specs = [
        pl.BlockSpec((None, nh * LANES, TQ), lambda b, p, i: (b, p, i)),
        pl.BlockSpec((None, SEQ, nh * LANES), lambda b, p, i: (b, 0, p)),
        pl.BlockSpec((None, nh * C_V, SEQ), lambda b, p, i: (b, p, 0)),
    ]
    out_spec = pl.BlockSpec((None, TQ, nh * C_V), lambda b, p, i: (b, i, p))
    grid = (BATCH, C_HEADS // nh, SEQ // TQ)
    return _flash_call(kernel, grid, nh, C_V, in_specs, out_spec, (qt, k, vt), "c_attention")


def _rope_lanes(x, cos, sin_a, sin_b, shift):
    return (x * cos + pltpu.roll(x, LANES - shift, 1) * sin_a + pltpu.roll(x, shift, 1) * sin_b)


def _b_prep_kernel(h_ref, g_ref, w_ref, qn_ref, kn_ref, cos_ref, sa_ref, sb_ref,
                   qt_ref, k_ref, vt_ref):
    xn = _rms(h_ref[...], g_ref[...]).astype(BF16)
    qkv = jnp.dot(xn, w_ref[...], preferred_element_type=F32)
    cos, sa, sb = cos_ref[...], sa_ref[...], sb_ref[...]
    for hd in range(B_HEADS + 2 * B_KV_HEADS):
        x = qkv[:, hd * LANES:(hd + 1) * LANES]
        if hd < B_HEADS + B_KV_HEADS:
            x = _rms(x, qn_ref[...] if hd < B_HEADS else kn_ref[...])
            x = _rope_lanes(x, cos, sa, sb, B_HEAD_DIM // 4)
        if hd < B_HEADS:
            x = x * (B_HEAD_DIM ** -0.5 * LOG2E)
            qt_ref[hd * LANES:(hd + 1) * LANES, :] = x.T.astype(BF16)
        elif hd < B_HEADS + B_KV_HEADS:
            j = hd - B_HEADS
            k_ref[:, j * LANES:(j + 1) * LANES] = x.astype(BF16)
        else:
            j = hd - B_HEADS - B_KV_HEADS
            vt_ref[j * LANES:(j + 1) * LANES, :] = x.T.astype(BF16)


def b_prep(h, gain, w, qnorm, knorm, tables):
    t, d = h.shape
    nrow = SEQ // TM_ROW
    row = lambda i: (i, 0)
    fixed = lambda i: (0, 0)
    tab = lambda i: (i % nrow, 0)
    colblk = lambda i: (i // nrow, 0, i % nrow)
    nq = B_HEADS * B_HEAD_DIM
    nkv = B_KV_HEADS * B_HEAD_DIM
    return pl.pallas_call(
        _b_prep_kernel,
        grid=(t // TM_ROW,),
        in_specs=[
            pl.BlockSpec((TM_ROW, d), row),
            pl.BlockSpec((1, d), fixed),
            pl.BlockSpec((d, B_QKV), fixed),
            pl.BlockSpec((1, B_HEAD_DIM), fixed),
            pl.BlockSpec((1, B_HEAD_DIM), fixed),
            pl.BlockSpec((TM_ROW, LANES), tab),
            pl.BlockSpec((TM_ROW, LANES), tab),
            pl.BlockSpec((TM_ROW, LANES), tab),
        ],
        out_specs=[
            pl.BlockSpec((None, nq, TM_ROW), colblk),
            pl.BlockSpec((TM_ROW, nkv), row),
            pl.BlockSpec((None, nkv, TM_ROW), colblk),
        ],
        out_shape=[
            jax.ShapeDtypeStruct((BATCH, nq, SEQ), BF16),
            jax.ShapeDtypeStruct((t, nkv), BF16),
            jax.ShapeDtypeStruct((BATCH, nkv, SEQ), BF16),
        ],
        compiler_params=_cparams(("parallel",)),
        name="b_prep",
    )(h, gain, w, qnorm, knorm, *tables)


def _c_prep_kernel(h_ref, g_ref, win_ref, qn_ref, kvn_ref, wuq_ref, wuk_ref, wuv_ref,
                   cos_ref, sa_ref, sb_ref, q_ref, k_ref, v_ref):
    xn = _rms(h_ref[...], g_ref[...]).astype(BF16)
    c = jnp.dot(xn, win_ref[...], preferred_element_type=F32)
    cq = _rms(c[:, :C_Q_LORA], qn_ref[...]).astype(BF16)
    ckv = _rms(c[:, C_Q_LORA:C_Q_LORA + C_KV_LORA], kvn_ref[...]).astype(BF16)
    cos, sa, sb = cos_ref[...], sa_ref[...], sb_ref[...]
    kr = _rope_lanes(c[:, C_Q_LORA + C_KV_LORA:], cos, sa, sb, C_ROPE // 2)
    q = jnp.dot(cq, wuq_ref[...], preferred_element_type=F32)
    kn = jnp.dot(ckv, wuk_ref[...], preferred_element_type=F32)
    v = jnp.dot(ckv, wuv_ref[...], preferred_element_type=F32)
    scale = (C_NOPE + C_ROPE) ** -0.5 * LOG2E
    for hd in range(C_HEADS):
        cols = slice(hd * LANES, (hd + 1) * LANES)
        qh = _rope_lanes(q[:, cols], cos, sa, sb, C_ROPE // 2) * scale
        q_ref[cols, :] = qh.T.astype(BF16)
        k_ref[:, cols] = (kn[:, cols] + kr).astype(BF16)
    for blk in range(C_HEADS * C_V // LANES):
        cols = slice(blk * LANES, (blk + 1) * LANES)
        v_ref[cols, :] = v[:, cols].T.astype(BF16)


def c_prep(h, gain, win, qnorm, kvnorm, wuq, wuk, wuv, tables):
    t, d = h.shape
    nrow = SEQ // TM_ROW
    row = lambda i: (i, 0)
    fixed = lambda i: (0, 0)
    tab = lambda i: (i % nrow, 0)
    colblk = lambda i: (i // nrow, 0, i % nrow)
    hq = C_HEADS * LANES
    return pl.pallas_call(
        _c_prep_kernel,
        grid=(t // TM_ROW,),
        in_specs=[
            pl.BlockSpec((TM_ROW, d), row),
            pl.BlockSpec((1, d), fixed),
            pl.BlockSpec((d, C_IN_PAD), fixed),
            pl.BlockSpec((1, C_Q_LORA), fixed),
            pl.BlockSpec((1, C_KV_LORA), fixed),
            pl.BlockSpec((C_Q_LORA, hq), fixed),
            pl.BlockSpec((C_KV_LORA, hq), fixed),
            pl.BlockSpec((C_KV_LORA, C_HEADS * C_V), fixed),
            pl.BlockSpec((TM_ROW, LANES), tab),
            pl.BlockSpec((TM_ROW, LANES), tab),
            pl.BlockSpec((TM_ROW, LANES), tab),
        ],
        out_specs=[
            pl.BlockSpec((None, hq, TM_ROW), colblk),
            pl.BlockSpec((TM_ROW, hq), row),
            pl.BlockSpec((None, C_HEADS * C_V, TM_ROW), colblk),
        ],
        out_shape=[
            jax.ShapeDtypeStruct((BATCH, hq, SEQ), BF16),
            jax.ShapeDtypeStruct((t, hq), BF16),
            jax.ShapeDtypeStruct((BATCH, C_HEADS * C_V, SEQ), BF16),
        ],
        compiler_params=_cparams(("parallel",)),
        name="c_prep",
    )(h, gain, win, qnorm, kvnorm, wuq, wuk, wuv, *tables)


def _mix_ffn_kernel(y_ref, wo_ref, gmix_ref, h_ref, gpre_ref, wg_ref, wu_ref, wd_ref, gpost_ref,
                    o_ref, h1_ref, xn_ref, acc_ref):
    f = pl.program_id(1)

    @pl.when(f == 0)
    def _():
        z = jnp.dot(y_ref[...], wo_ref[...], preferred_element_type=F32)
        h1 = h_ref[...] + _rms(z, gmix_ref[...])
        h1_ref[...] = h1
        xn_ref[...] = _rms(h1, gpre_ref[...]).astype(BF16)

    xn = xn_ref[...]
    gate = jnp.dot(xn, wg_ref[...], preferred_element_type=F32)
    up = jnp.dot(xn, wu_ref[...], preferred_element_type=F32)
    act = (gate * jax.nn.sigmoid(gate)) * up
    part = jnp.dot(act.astype(BF16), wd_ref[...], preferred_element_type=F32)

    @pl.when(f == 0)
    def _():
        acc_ref[...] = part

    @pl.when(f > 0)
    def _():
        acc_ref[...] += part

    @pl.when(f == pl.num_programs(1) - 1)
    def _():
        o_ref[...] = h1_ref[...] + _rms(acc_ref[...], gpost_ref[...])


def mix_ffn(y, wo, gmix, h, gpre, wg, wu, wd, gpost):
    t, d = h.shape
    row = lambda i, f: (i, 0)
    fixed = lambda i, f: (0, 0)
    return pl.pallas_call(
        _mix_ffn_kernel,
        grid=(t // TM_ROW, D_FF // TF_FFN),
        in_specs=[
            pl.BlockSpec((TM_ROW, d), row),
            pl.BlockSpec((d, d), fixed),
            pl.BlockSpec((1, d), fixed),
            pl.BlockSpec((TM_ROW, d), row),
            pl.BlockSpec((1, d), fixed),
            pl.BlockSpec((d, TF_FFN), lambda i, f: (0, f)),
            pl.BlockSpec((d, TF_FFN), lambda i, f: (0, f)),
            pl.BlockSpec((TF_FFN, d), lambda i, f: (f, 0)),
            pl.BlockSpec((1, d), fixed),
        ],
        out_specs=pl.BlockSpec((TM_ROW, d), row),
        out_shape=jax.ShapeDtypeStruct((t, d), F32),
        scratch_shapes=[
            pltpu.VMEM((TM_ROW, d), F32),
            pltpu.VMEM((TM_ROW, d), BF16),
            pltpu.VMEM((TM_ROW, d), F32),
        ],
        compiler_params=_cparams(("parallel", "arbitrary")),
        name="mix_ffn",
    )(y, wo, gmix, h, gpre, wg, wu, wd, gpost)


def _rope_tables(pos_by_lane, freq_idx, dim, theta, active, first_half):
    freqs = jnp.power(jnp.float32(theta), -freq_idx.astype(F32) * 2.0 / dim)
    ang = pos_by_lane * freqs[None, :]
    cos = jnp.where(active[None, :], jnp.cos(ang), 1.0)
    sin = jnp.sin(ang)
    sin_a = jnp.where((active & first_half)[None, :], -sin, 0.0)
    sin_b = jnp.where((active & ~first_half)[None, :], sin, 0.0)
    return cos.astype(F32), sin_a.astype(F32), sin_b.astype(F32)


def _b_tables():
    lane = np.arange(LANES)
    half = B_HEAD_DIM // 2
    quarter = half // 2
    t = jnp.arange(SEQ)
    row = (t // GRID_W).astype(F32)
    col = (t % GRID_W).astype(F32)
    pos = jnp.where(jnp.asarray(lane < half)[None, :], row[:, None], col[:, None])
    return _rope_tables(pos, jnp.asarray(lane % quarter), half, B_ROPE_THETA,
                        jnp.asarray(np.ones(LANES, bool)), jnp.asarray((lane % half) < quarter))


def _c_tables():
    lane = np.arange(LANES)
    half = C_ROPE // 2
    active = (lane >= C_NOPE) & (lane < C_NOPE + C_ROPE)
    pos = jnp.broadcast_to(jnp.arange(SEQ, dtype=F32)[:, None], (SEQ, LANES))
    return _rope_tables(pos, jnp.asarray((lane - C_NOPE) % half), C_ROPE, C_ROPE_THETA,
                        jnp.asarray(active), jnp.asarray((lane - C_NOPE) < half))


def _alibi_slopes():
    n = A_NG * A_HEADS
    return jnp.asarray(2.0 ** (-8.0 * np.arange(1, n + 1) / n), dtype=F32)


def _mixer_a(h, gain, wqkv):
    qkv = norm_matmul(h, gain, wqkv.astype(BF16))
    return a_attention(qkv.reshape(BATCH, SEQ, A_QKV), _alibi_slopes())


def _mixer_b(h, gain, wqkv, qnorm, knorm):
    qt, k, vt = b_prep(h, gain, wqkv.astype(BF16), qnorm[None, :], knorm[None, :], _b_tables())
    return b_attention(qt, k.reshape(BATCH, SEQ, -1), vt)


def _mixer_c(h, gain, win, qnorm, kvnorm, wuq, wukv):
    win_p = jnp.zeros((D_MODEL, C_IN_PAD), F32)
    win_p = win_p.at[:, :C_Q_LORA + C_KV_LORA].set(win[:, :C_Q_LORA + C_KV_LORA])
    kr0 = C_Q_LORA + C_KV_LORA + C_NOPE
    win_p = win_p.at[:, kr0:kr0 + C_ROPE].set(win[:, C_Q_LORA + C_KV_LORA:])
    wuq_p = jnp.pad(wuq.reshape(C_Q_LORA, C_HEADS, C_NOPE + C_ROPE),
                    ((0, 0), (0, 0), (0, LANES - C_NOPE - C_ROPE))).reshape(C_Q_LORA, C_HEADS * LANES)
    wukv3 = wukv.reshape(C_KV_LORA, C_HEADS, C_NOPE + C_V)
    wuk_p = jnp.pad(wukv3[:, :, :C_NOPE],
                    ((0, 0), (0, 0), (0, LANES - C_NOPE))).reshape(C_KV_LORA, C_HEADS * LANES)
    wuv = wukv3[:, :, C_NOPE:].reshape(C_KV_LORA, C_HEADS * C_V)
    qt, k, vt = c_prep(h, gain, win_p.astype(BF16), qnorm[None, :], kvnorm[None, :],
                       wuq_p.astype(BF16), wuk_p.astype(BF16), wuv.astype(BF16), _c_tables())
    return c_attention(qt, k.reshape(BATCH, SEQ, -1), vt)


def kernel(x, norm_mix_pre, norm_mix_post, norm_ffn_pre, norm_ffn_post, ffn_wg, ffn_wu, ffn_wd,
           a_wqkv, a_wo, b_wqkv, b_qnorm, b_knorm, b_wo,
           c_win, c_qnorm, c_kvnorm, c_wuq, c_wukv, c_wo):
    h = x.reshape(BATCH * SEQ, D_MODEL)
    for i in range(DEPTH):
        kind = i % N_MIXERS
        j = i // N_MIXERS
        gpre = norm_mix_pre[i][None, :]
        if kind == 0:
            y, wo = _mixer_a(h, gpre, a_wqkv[j]), a_wo[j]
        elif kind == 1:
            y, wo = _mixer_b(h, gpre, b_wqkv[j], b_qnorm[j], b_knorm[j]), b_wo[j]
        else:
            y, wo = _mixer_c(h, gpre, c_win[j], c_qnorm[j], c_kvnorm[j], c_wuq[j], c_wukv[j]), c_wo[j]
        h = mix_ffn(y.reshape(BATCH * SEQ, D_MODEL), wo.astype(BF16), norm_mix_post[i][None, :], h,
                    norm_ffn_pre[i][None, :], ffn_wg[i].astype(BF16), ffn_wu[i].astype(BF16),
                    ffn_wd[i].astype(BF16), norm_ffn_post[i][None, :])
    return h.reshape(BATCH, SEQ, D_MODEL)
```

```python
import functools

import numpy as np
import jax
import jax.numpy as jnp
from jax import lax
from jax.experimental import pallas as pl
from jax.experimental.pallas import tpu as pltpu

F32 = jnp.float32
BF16 = jnp.bfloat16

D_MODEL = 1024
BATCH = 2
SEQ = 8192
DEPTH = 4
N_MIXERS = 3
GRID_W = 64
D_FF = 2816
NORM_EPS = 1e-6
NEG_INF = -1e30

A_GROUPS = ((128, 1), (512, 4), (2048, 16))
A_NG = 3
A_HEAD_DIM = 64
A_HEADS = 16
A_QKV = 3 * A_NG * A_HEADS * A_HEAD_DIM
A_RADIUS = 64

B_HEAD_DIM = 128
B_HEADS = 8
B_KV_HEADS = 2
B_ROPE_THETA = 10000.0
B_QKV = (B_HEADS + 2 * B_KV_HEADS) * B_HEAD_DIM

C_HEADS = 16
C_Q_LORA = 384
C_KV_LORA = 256
C_NOPE = 64
C_ROPE = 32
C_V = 64
C_ROPE_THETA = 10000.0
C_IN_PAD = 768

LANES = 128
ONES_ROWS = 16
LOG2E = 1.4426950408889634
VMEM_LIMIT = 56 * 1024 * 1024

TM_PROJ = 256
TN_PROJ = 1024
TM_ROW = 512
TF_FFN = 2816
TQ = 512
TK = 512
FLASH_UNROLL = 4
A_QB = 256
A_W = A_QB + 2 * A_RADIUS
A_UNROLL = 4


def _cparams(sem):
    return pltpu.CompilerParams(dimension_semantics=sem, vmem_limit_bytes=VMEM_LIMIT)


def _rms(x, gain):
    ms = jnp.mean(x * x, axis=-1, keepdims=True)
    return (x * lax.rsqrt(ms + NORM_EPS)) * gain


def _norm_matmul_kernel(x_ref, g_ref, w_ref, o_ref):
    xn = _rms(x_ref[...], g_ref[...]).astype(BF16)
    for c in range(o_ref.shape[1] // TN_PROJ):
        cols = slice(c * TN_PROJ, (c + 1) * TN_PROJ)
        o_ref[:, cols] = jnp.dot(xn, w_ref[:, cols], preferred_element_type=F32)


def norm_matmul(h, gain, w):
    t, d = h.shape
    n = w.shape[1]
    return pl.pallas_call(
        _norm_matmul_kernel,
        grid=(t // TM_PROJ,),
        in_specs=[
            pl.BlockSpec((TM_PROJ, d), lambda i: (i, 0)),
            pl.BlockSpec((1, d), lambda i: (0, 0)),
            pl.BlockSpec((d, n), lambda i: (0, 0), pipeline_mode=pl.Buffered(1)),
        ],
        out_specs=pl.BlockSpec((TM_PROJ, n), lambda i: (i, 0)),
        out_shape=jax.ShapeDtypeStruct((t, n), F32),
        compiler_params=_cparams(("parallel",)),
        name="a_qkv_proj",
    )(h, gain, w)


def _a_attn_kernel(slopes_ref, q_ref, k_ref, v_ref, o_ref, m_ref, l_ref, acc_ref,
                   s_ref, mf_ref, bias_ref):
    hp = pl.program_id(1)
    g = pl.program_id(2)
    lane = lax.broadcasted_iota(jnp.int32, (A_QB, LANES), 1)
    lo = lane < A_HEAD_DIM
    head_lanes = (lo, jnp.logical_not(lo))
    lo_w = lax.broadcasted_iota(jnp.int32, (A_W, LANES), 1) < A_HEAD_DIM
    head_lanes_w = (lo_w, jnp.logical_not(lo_w))
    rel = (lax.broadcasted_iota(jnp.int32, (A_QB, A_W), 1)
           - lax.broadcasted_iota(jnp.int32, (A_QB, A_W), 0))

    def group_body(step, gi, d):
        first = step == 0
        final = step == A_NG - 1
        cls_len = SEQ // d
        nblk = cls_len // A_QB
        nit = d * nblk

        for oi in range(3):
            dist = jnp.abs(rel - oi * A_RADIUS)
            for h2 in range(2):
                slope = slopes_ref[gi * A_HEADS + hp * 2 + h2] * (d * LOG2E)
                bias_ref[oi, h2] = jnp.where(dist <= A_RADIUS, -slope * dist.astype(F32), NEG_INF)

        def geometry(it):
            r = it // nblk
            i0 = (it % nblk) * A_QB
            if isinstance(it, int):
                kstart = min(max(i0 - A_RADIUS, 0), cls_len - A_W)
            else:
                kstart = jnp.clip(i0 - A_RADIUS, 0, cls_len - A_W)
            oi = (i0 - kstart) // A_RADIUS
            if d == 1 and isinstance(it, int):
                q_rows = pl.ds(i0, A_QB)
                k_rows = pl.ds(kstart, A_W)
            elif d == 1:
                q_rows = pl.ds(pl.multiple_of(i0, A_QB), A_QB)
                k_rows = pl.ds(pl.multiple_of(kstart, A_RADIUS), A_W)
            else:
                q_rows = pl.ds(r + i0 * d, A_QB, stride=d)
                k_rows = pl.ds(r + kstart * d, A_W, stride=d)
            return q_rows, k_rows, oi

        def load_qk(it):
            q_rows, k_rows, oi = geometry(it)
            q = (q_ref[q_rows, :] * (A_HEAD_DIM ** -0.5 * LOG2E)).astype(BF16)
            return q, k_ref[k_rows, :].astype(BF16), oi

        def scores(qk, slot, h2):
            q, k, oi = qk
            qm = jnp.where(head_lanes[h2], q, jnp.zeros_like(q))
            s = lax.dot_general(qm, k, (((1,), (1,)), ((), ())), preferred_element_type=F32)
            s = s + bias_ref[oi, h2]
            s_ref[slot, h2] = s
            mf_ref[slot, h2] = jnp.broadcast_to(jnp.max(s, axis=1, keepdims=True), (A_QB, LANES))

        def weighted_values(v, slot, h2):
            va = jnp.where(head_lanes_w[h2], v, 1.0).astype(BF16)
            mfull = mf_ref[slot, h2]
            p = jnp.exp2(s_ref[slot, h2] - jnp.concatenate([mfull] * (A_W // LANES), axis=1))
            return jnp.dot(p.astype(BF16), va, preferred_element_type=F32)

        def merge(q_rows, slot, outs):
            m_blk = jnp.where(lo, mf_ref[slot, 0], mf_ref[slot, 1])
            pv_blk = jnp.where(lo, outs[0], outs[1])
            l_blk = pltpu.roll(jnp.where(lo, outs[1], outs[0]), A_HEAD_DIM, 1)
            if first:
                m_ref[q_rows, :] = m_blk
                l_ref[q_rows, :] = l_blk
                acc_ref[q_rows, :] = pv_blk
            else:
                m_old = m_ref[q_rows, :]
                m_new = jnp.maximum(m_old, m_blk)
                a_old = jnp.exp2(m_old - m_new)
                a_blk = jnp.exp2(m_blk - m_new)
                l_new = a_old * l_ref[q_rows, :] + a_blk * l_blk
                acc_new = a_old * acc_ref[q_rows, :] + a_blk * pv_blk
                if final:
                    acc_ref[q_rows, :] = acc_new / l_new
                else:
                    m_ref[q_rows, :] = m_new
                    l_ref[q_rows, :] = l_new
                    acc_ref[q_rows, :] = acc_new

        qk0 = load_qk(0)
        for h2 in range(2):
            scores(qk0, 0, h2)

        def blocks(it0, last):
            for u in range(A_UNROLL):
                has_next = not (last and u == A_UNROLL - 1)
                qk = load_qk(it0 + u + 1) if has_next else None
                q_rows, k_rows, _ = geometry(it0 + u)
                v = v_ref[k_rows, :]
                outs = []
                for h2 in range(2):
                    if has_next:
                        scores(qk, (u + 1) % 2, h2)
                    outs.append(weighted_values(v, u % 2, h2))
                merge(q_rows, u % 2, outs)

        def body(jj, carry):
            blocks(A_UNROLL * jj, False)
            return carry

        lax.fori_loop(0, nit // A_UNROLL - 1, body, 0)
        blocks(nit - A_UNROLL, True)
        if final:
            o_ref[...] = acc_ref[...].astype(o_ref.dtype)

    for step in range(A_NG):
        gi = A_NG - 1 - step
        pl.when(g == step)(functools.partial(group_body, step, gi, A_GROUPS[gi][1]))


def a_attention(qkv, slopes):
    npairs = A_HEADS // 2

    def col(which):
        return lambda b, hp, g: (b, 0, (which * A_NG + (A_NG - 1 - g)) * npairs + hp)

    return pl.pallas_call(
        _a_attn_kernel,
        grid=(BATCH, npairs, A_NG),
        in_specs=[
            pl.BlockSpec(memory_space=pltpu.SMEM),
            pl.BlockSpec((None, SEQ, LANES), col(0)),
            pl.BlockSpec((None, SEQ, LANES), col(1)),
            pl.BlockSpec((None, SEQ, LANES), col(2)),
        ],
        out_specs=pl.BlockSpec((None, SEQ, LANES), lambda b, hp, g: (b, 0, hp)),
        out_shape=jax.ShapeDtypeStruct((BATCH, SEQ, D_MODEL), BF16),
        scratch_shapes=[
            pltpu.VMEM((SEQ, LANES), F32),
            pltpu.VMEM((SEQ, LANES), F32),
            pltpu.VMEM((SEQ, LANES), F32),
            pltpu.VMEM((2, 2, A_QB, A_W), F32),
            pltpu.VMEM((2, 2, A_QB, LANES), F32),
            pltpu.VMEM((3, 2, A_QB, A_W), F32),
        ],
        compiler_params=_cparams(("parallel", "parallel", "arbitrary")),
        name="a_attention",
    )(slopes, qkv, qkv, qkv)


def _flash_kernel(qt_ref, k_ref, vt_ref, o_ref, m_ref, acc_ref, s_ref, mc_ref, *, nh, dv, shared_kv):
    m_ref[...] = jnp.full(m_ref.shape, -jnp.inf, F32)
    acc_ref[...] = jnp.zeros(acc_ref.shape, F32)
    ones = jnp.ones((ONES_ROWS, TK), BF16)
    nchunk = SEQ // TK

    def chunk_rows(chunk):
        start = chunk * TK
        return pl.ds(start if isinstance(start, int) else pl.multiple_of(start, TK), TK)

    def scores(chunk, slot, h):
        rows = chunk_rows(chunk)
        k = k_ref[rows, :] if shared_kv else k_ref[rows, h * LANES:(h + 1) * LANES]
        st = jnp.dot(k, qt_ref[h * LANES:(h + 1) * LANES, :], preferred_element_type=F32)
        s_ref[slot, h] = st
        mc_ref[slot, h] = jnp.max(st, axis=0, keepdims=True)

    def update(chunk, slot, h):
        rows = chunk_rows(chunk)
        vt = vt_ref[:, rows] if shared_kv else vt_ref[h * dv:(h + 1) * dv, rows]
        m_prev = m_ref[h]
        m_new = jnp.maximum(m_prev, mc_ref[slot, h])
        alpha = jnp.exp2(m_prev - m_new)
        pt = jnp.exp2(s_ref[slot, h] - m_new).astype(BF16)
        pv = jnp.dot(jnp.concatenate([vt, ones], axis=0), pt, preferred_element_type=F32)
        acc_ref[h] = alpha * acc_ref[h] + pv
        m_ref[h] = m_new

    for h in range(nh):
        scores(0, 0, h)

    def chunks(j, last):
        for u in range(FLASH_UNROLL):
            for h in range(nh):
                if not (last and u == FLASH_UNROLL - 1):
                    scores(j + u + 1, (u + 1) % 2, h)
                update(j + u, u % 2, h)

    def body(jj, carry):
        chunks(FLASH_UNROLL * jj, False)
        return carry

    lax.fori_loop(0, nchunk // FLASH_UNROLL - 1, body, 0)
    chunks(nchunk - FLASH_UNROLL, True)

    outs = []
    for h in range(nh):
        acc = acc_ref[h]
        outs.append(acc[:dv, :] / acc[dv:dv + 1, :])
    ot = outs[0] if nh == 1 else jnp.concatenate(outs, axis=0)
    o_ref[...] = ot.T.astype(o_ref.dtype)


def _flash_call(kernel, grid, nh, dv, in_specs, out_spec, args, name):
    return pl.pallas_call(
        kernel,
        grid=grid,
        in_specs=in_specs,
        out_specs=out_spec,
        out_shape=jax.ShapeDtypeStruct((BATCH, SEQ, D_MODEL), BF16),
        scratch_shapes=[
            pltpu.VMEM((nh, 1, TQ), F32),
            pltpu.VMEM((nh, dv + ONES_ROWS, TQ), F32),
            pltpu.VMEM((2, nh, TK, TQ), F32),
            pltpu.VMEM((2, nh, 1, TQ), F32),
        ],
        compiler_params=_cparams(("parallel",) * (len(grid) - 1) + ("arbitrary",)),
        name=name,
    )(*args)


def b_attention(qt, k, vt):
    grp = B_HEADS // B_KV_HEADS
    nh = 2
    kernel = functools.partial(_flash_kernel, nh=nh, dv=B_HEAD_DIM, shared_kv=True)
    in_specs = [
        pl.BlockSpec((None, nh * LANES, TQ), lambda b, kv, g, i: (b, kv * (grp // nh) + g, i)),
        pl.BlockSpec((None, SEQ, LANES), lambda b, kv, g, i: (b, 0, kv)),
        pl.BlockSpec((None, LANES, SEQ), lambda b, kv, g, i: (b, kv, 0)),
    ]
    out_spec = pl.BlockSpec((None, TQ, nh * LANES), lambda b, kv, g, i: (b, i, kv * (grp // nh) + g))
    grid = (BATCH, B_KV_HEADS, grp // nh, SEQ // TQ)
    return _flash_call(kernel, grid, nh, B_HEAD_DIM, in_specs, out_spec, (qt, k, vt), "b_attention")


def c_attention(qt, k, vt):
    nh = 2
    kernel = functools.partial(_flash_kernel, nh=nh, dv=C_V, shared_kv=False)
    in_specs = [
        pl.BlockSpec((None, nh * LANES, TQ), lambda b, p, i: (b, p, i)),
        pl.BlockSpec((None, SEQ, nh * LANES), lambda b, p, i: (b, 0, p)),
        pl.BlockSpec((None, nh * C_V, SEQ), lambda b, p, i: (b, p, 0)),
    ]
    out_spec = pl.BlockSpec((None, TQ, nh * C_V), lambda b, p, i: (b, i, p))
    grid = (BATCH, C_HEADS // nh, SEQ // TQ)
    return _flash_call(kernel, grid, nh, C_V, in_specs, out_spec, (qt, k, vt), "c_attention")


def _rope_lanes(x, cos, sin_a, sin_b, shift):
    return (x * cos + pltpu.roll(x, LANES - shift, 1) * sin_a + pltpu.roll(x, shift, 1) * sin_b)


def _b_prep_kernel(h_ref, g_ref, w_ref, qn_ref, kn_ref, cos_ref, sa_ref, sb_ref,
                   qt_ref, k_ref, vt_ref):
    xn = _rms(h_ref[...], g_ref[...]).astype(BF16)
    qkv = jnp.dot(xn, w_ref[...], preferred_element_type=F32)
    cos, sa, sb = cos_ref[...], sa_ref[...], sb_ref[...]
    for hd in range(B_HEADS + 2 * B_KV_HEADS):
        x = qkv[:, hd * LANES:(hd + 1) * LANES]
        if hd < B_HEADS + B_KV_HEADS:
            x = _rms(x, qn_ref[...] if hd < B_HEADS else kn_ref[...])
            x = _rope_lanes(x, cos, sa, sb, B_HEAD_DIM // 4)
        if hd < B_HEADS:
            x = x * (B_HEAD_DIM ** -0.5 * LOG2E)
            qt_ref[hd * LANES:(hd + 1) * LANES, :] = x.T.astype(BF16)
        elif hd < B_HEADS + B_KV_HEADS:
            j = hd - B_HEADS
            k_ref[:, j * LANES:(j + 1) * LANES] = x.astype(BF16)
        else:
            j = hd - B_HEADS - B_KV_HEADS
            vt_ref[j * LANES:(j + 1) * LANES, :] = x.T.astype(BF16)


def b_prep(h, gain, w, qnorm, knorm, tables):
    t, d = h.shape
    nrow = SEQ // TM_ROW
    row = lambda i: (i, 0)
    fixed = lambda i: (0, 0)
    tab = lambda i: (i % nrow, 0)
    colblk = lambda i: (i // nrow, 0, i % nrow)
    nq = B_HEADS * B_HEAD_DIM
    nkv = B_KV_HEADS * B_HEAD_DIM
    return pl.pallas_call(
        _b_prep_kernel,
        grid=(t // TM_ROW,),
        in_specs=[
            pl.BlockSpec((TM_ROW, d), row),
            pl.BlockSpec((1, d), fixed),
            pl.BlockSpec((d, B_QKV), fixed),
            pl.BlockSpec((1, B_HEAD_DIM), fixed),
            pl.BlockSpec((1, B_HEAD_DIM), fixed),
            pl.BlockSpec((TM_ROW, LANES), tab),
            pl.BlockSpec((TM_ROW, LANES), tab),
            pl.BlockSpec((TM_ROW, LANES), tab),
        ],
        out_specs=[
            pl.BlockSpec((None, nq, TM_ROW), colblk),
            pl.BlockSpec((TM_ROW, nkv), row),
            pl.BlockSpec((None, nkv, TM_ROW), colblk),
        ],
        out_shape=[
            jax.ShapeDtypeStruct((BATCH, nq, SEQ), BF16),
            jax.ShapeDtypeStruct((t, nkv), BF16),
            jax.ShapeDtypeStruct((BATCH, nkv, SEQ), BF16),
        ],
        compiler_params=_cparams(("parallel",)),
        name="b_prep",
    )(h, gain, w, qnorm, knorm, *tables)


def _c_prep_kernel(h_ref, g_ref, win_ref, qn_ref, kvn_ref, wuq_ref, wuk_ref, wuv_ref,
                   cos_ref, sa_ref, sb_ref, q_ref, k_ref, v_ref):
    xn = _rms(h_ref[...], g_ref[...]).astype(BF16)
    c = jnp.dot(xn, win_ref[...], preferred_element_type=F32)
    cq = _rms(c[:, :C_Q_LORA], qn_ref[...]).astype(BF16)
    ckv = _rms(c[:, C_Q_LORA:C_Q_LORA + C_KV_LORA], kvn_ref[...]).astype(BF16)
    cos, sa, sb = cos_ref[...], sa_ref[...], sb_ref[...]
    kr = _rope_lanes(c[:, C_Q_LORA + C_KV_LORA:], cos, sa, sb, C_ROPE // 2)
    q = jnp.dot(cq, wuq_ref[...], preferred_element_type=F32)
    kn = jnp.dot(ckv, wuk_ref[...], preferred_element_type=F32)
    v = jnp.dot(ckv, wuv_ref[...], preferred_element_type=F32)
    scale = (C_NOPE + C_ROPE) ** -0.5 * LOG2E
    for hd in range(C_HEADS):
        cols = slice(hd * LANES, (hd + 1) * LANES)
        qh = _rope_lanes(q[:, cols], cos, sa, sb, C_ROPE // 2) * scale
        q_ref[cols, :] = qh.T.astype(BF16)
        k_ref[:, cols] = (kn[:, cols] + kr).astype(BF16)
    for blk in range(C_HEADS * C_V // LANES):
        cols = slice(blk * LANES, (blk + 1) * LANES)
        v_ref[cols, :] = v[:, cols].T.astype(BF16)


def c_prep(h, gain, win, qnorm, kvnorm, wuq, wuk, wuv, tables):
    t, d = h.shape
    nrow = SEQ // TM_ROW
    row = lambda i: (i, 0)
    fixed = lambda i: (0, 0)
    tab = lambda i: (i % nrow, 0)
    colblk = lambda i: (i // nrow, 0, i % nrow)
    hq = C_HEADS * LANES
    return pl.pallas_call(
        _c_prep_kernel,
        grid=(t // TM_ROW,),
        in_specs=[
            pl.BlockSpec((TM_ROW, d), row),
            pl.BlockSpec((1, d), fixed),
            pl.BlockSpec((d, C_IN_PAD), fixed),
            pl.BlockSpec((1, C_Q_LORA), fixed),
            pl.BlockSpec((1, C_KV_LORA), fixed),
            pl.BlockSpec((C_Q_LORA, hq), fixed),
            pl.BlockSpec((C_KV_LORA, hq), fixed),
            pl.BlockSpec((C_KV_LORA, C_HEADS * C_V), fixed),
            pl.BlockSpec((TM_ROW, LANES), tab),
            pl.BlockSpec((TM_ROW, LANES), tab),
            pl.BlockSpec((TM_ROW, LANES), tab),
        ],
        out_specs=[
            pl.BlockSpec((None, hq, TM_ROW), colblk),
            pl.BlockSpec((TM_ROW, hq), row),
            pl.BlockSpec((None, C_HEADS * C_V, TM_ROW), colblk),
        ],
        out_shape=[
            jax.ShapeDtypeStruct((BATCH, hq, SEQ), BF16),
            jax.ShapeDtypeStruct((t, hq), BF16),
            jax.ShapeDtypeStruct((BATCH, C_HEADS * C_V, SEQ), BF16),
        ],
        compiler_params=_cparams(("parallel",)),
        name="c_prep",
    )(h, gain, win, qnorm, kvnorm, wuq, wuk, wuv, *tables)


def _mix_ffn_kernel(y_ref, wo_ref, gmix_ref, h_ref, gpre_ref, wg_ref, wu_ref, wd_ref, gpost_ref,
                    o_ref, h1_ref, xn_ref, acc_ref):
    f = pl.program_id(1)

    @pl.when(f == 0)
    def _():
        z = jnp.dot(y_ref[...], wo_ref[...], preferred_element_type=F32)
        h1 = h_ref[...] + _rms(z, gmix_ref[...])
        h1_ref[...] = h1
        xn_ref[...] = _rms(h1, gpre_ref[...]).astype(BF16)

    xn = xn_ref[...]
    gate = jnp.dot(xn, wg_ref[...], preferred_element_type=F32)
    up = jnp.dot(xn, wu_ref[...], preferred_element_type=F32)
    act = (gate * jax.nn.sigmoid(gate)) * up
    part = jnp.dot(act.astype(BF16), wd_ref[...], preferred_element_type=F32)

    @pl.when(f == 0)
    def _():
        acc_ref[...] = part

    @pl.when(f > 0)
    def _():
        acc_ref[...] += part

    @pl.when(f == pl.num_programs(1) - 1)
    def _():
        o_ref[...] = h1_ref[...] + _rms(acc_ref[...], gpost_ref[...])


def mix_ffn(y, wo, gmix, h, gpre, wg, wu, wd, gpost):
    t, d = h.shape
    row = lambda i, f: (i, 0)
    fixed = lambda i, f: (0, 0)
    wmode = pl.Buffered(1) if TF_FFN == D_FF else None
    return pl.pallas_call(
        _mix_ffn_kernel,
        grid=(t // TM_ROW, D_FF // TF_FFN),
        in_specs=[
            pl.BlockSpec((TM_ROW, d), row),
            pl.BlockSpec((d, d), fixed),
            pl.BlockSpec((1, d), fixed),
            pl.BlockSpec((TM_ROW, d), row),
            pl.BlockSpec((1, d), fixed),
            pl.BlockSpec((d, TF_FFN), lambda i, f: (0, f), pipeline_mode=wmode),
            pl.BlockSpec((d, TF_FFN), lambda i, f: (0, f), pipeline_mode=wmode),
            pl.BlockSpec((TF_FFN, d), lambda i, f: (f, 0), pipeline_mode=wmode),
            pl.BlockSpec((1, d), fixed),
        ],
        out_specs=pl.BlockSpec((TM_ROW, d), row),
        out_shape=jax.ShapeDtypeStruct((t, d), F32),
        scratch_shapes=[
            pltpu.VMEM((TM_ROW, d), F32),
            pltpu.VMEM((TM_ROW, d), BF16),
            pltpu.VMEM((TM_ROW, d), F32),
        ],
        compiler_params=_cparams(("parallel", "arbitrary")),
        name="mix_ffn",
    )(y, wo, gmix, h, gpre, wg, wu, wd, gpost)


def _rope_tables(pos_by_lane, freq_idx, dim, theta, active, first_half):
    freqs = jnp.power(jnp.float32(theta), -freq_idx.astype(F32) * 2.0 / dim)
    ang = pos_by_lane * freqs[None, :]
    cos = jnp.where(active[None, :], jnp.cos(ang), 1.0)
    sin = jnp.sin(ang)
    sin_a = jnp.where((active & first_half)[None, :], -sin, 0.0)
    sin_b = jnp.where((active & ~first_half)[None, :], sin, 0.0)
    return cos.astype(F32), sin_a.astype(F32), sin_b.astype(F32)


def _b_tables():
    lane = np.arange(LANES)
    half = B_HEAD_DIM // 2
    quarter = half // 2
    t = jnp.arange(SEQ)
    row = (t // GRID_W).astype(F32)
    col = (t % GRID_W).astype(F32)
    pos = jnp.where(jnp.asarray(lane < half)[None, :], row[:, None], col[:, None])
    return _rope_tables(pos, jnp.asarray(lane % quarter), half, B_ROPE_THETA,
                        jnp.asarray(np.ones(LANES, bool)), jnp.asarray((lane % half) < quarter))


def _c_tables():
    lane = np.arange(LANES)
    half = C_ROPE // 2
    active = (lane >= C_NOPE) & (lane < C_NOPE + C_ROPE)
    pos = jnp.broadcast_to(jnp.arange(SEQ, dtype=F32)[:, None], (SEQ, LANES))
    return _rope_tables(pos, jnp.asarray((lane - C_NOPE) % half), C_ROPE, C_ROPE_THETA,
                        jnp.asarray(active), jnp.asarray((lane - C_NOPE) < half))


def _alibi_slopes():
    n = A_NG * A_HEADS
    return jnp.asarray(2.0 ** (-8.0 * np.arange(1, n + 1) / n), dtype=F32)


def _mixer_a(h, gain, wqkv):
    qkv = norm_matmul(h, gain, wqkv.astype(BF16))
    return a_attention(qkv.reshape(BATCH, SEQ, A_QKV), _alibi_slopes())


def _mixer_b(h, gain, wqkv, qnorm, knorm):
    qt, k, vt = b_prep(h, gain, wqkv.astype(BF16), qnorm[None, :], knorm[None, :], _b_tables())
    return b_attention(qt, k.reshape(BATCH, SEQ, -1), vt)


def _mixer_c(h, gain, win, qnorm, kvnorm, wuq, wukv):
    win_p = jnp.zeros((D_MODEL, C_IN_PAD), F32)
    win_p = win_p.at[:, :C_Q_LORA + C_KV_LORA].set(win[:, :C_Q_LORA + C_KV_LORA])
    kr0 = C_Q_LORA + C_KV_LORA + C_NOPE
    win_p = win_p.at[:, kr0:kr0 + C_ROPE].set(win[:, C_Q_LORA + C_KV_LORA:])
    wuq_p = jnp.pad(wuq.reshape(C_Q_LORA, C_HEADS, C_NOPE + C_ROPE),
                    ((0, 0), (0, 0), (0, LANES - C_NOPE - C_ROPE))).reshape(C_Q_LORA, C_HEADS * LANES)
    wukv3 = wukv.reshape(C_KV_LORA, C_HEADS, C_NOPE + C_V)
    wuk_p = jnp.pad(wukv3[:, :, :C_NOPE],
                    ((0, 0), (0, 0), (0, LANES - C_NOPE))).reshape(C_KV_LORA, C_HEADS * LANES)
    wuv = wukv3[:, :, C_NOPE:].reshape(C_KV_LORA, C_HEADS * C_V)
    qt, k, vt = c_prep(h, gain, win_p.astype(BF16), qnorm[None, :], kvnorm[None, :],
                       wuq_p.astype(BF16), wuk_p.astype(BF16), wuv.astype(BF16), _c_tables())
    return c_attention(qt, k.reshape(BATCH, SEQ, -1), vt)


def kernel(x, norm_mix_pre, norm_mix_post, norm_ffn_pre, norm_ffn_post, ffn_wg, ffn_wu, ffn_wd,
           a_wqkv, a_wo, b_wqkv, b_qnorm, b_knorm, b_wo,
           c_win, c_qnorm, c_kvnorm, c_wuq, c_wukv, c_wo):
    h = x.reshape(BATCH * SEQ, D_MODEL)
    for i in range(DEPTH):
        kind = i % N_MIXERS
        j = i // N_MIXERS
        gpre = norm_mix_pre[i][None, :]
        if kind == 0:
            y, wo = _mixer_a(h, gpre, a_wqkv[j]), a_wo[j]
        elif kind == 1:
            y, wo = _mixer_b(h, gpre, b_wqkv[j], b_qnorm[j], b_knorm[j]), b_wo[j]
        else:
            y, wo = _mixer_c(h, gpre, c_win[j], c_qnorm[j], c_kvnorm[j], c_wuq[j], c_wukv[j]), c_wo[j]
        h = mix_ffn(y.reshape(BATCH * SEQ, D_MODEL), wo.astype(BF16), norm_mix_post[i][None, :], h,
                    norm_ffn_pre[i][None, :], ffn_wg[i].astype(BF16), ffn_wu[i].astype(BF16),
                    ffn_wd[i].astype(BF16), norm_ffn_post[i][None, :])
    return h.reshape(BATCH, SEQ, D_MODEL)
```

```python
import functools

import numpy as np
import jax
import jax.numpy as jnp
from jax import lax
from jax.experimental import pallas as pl
from jax.experimental.pallas import tpu as pltpu

F32 = jnp.float32
BF16 = jnp.bfloat16

D_MODEL = 1024
BATCH = 2
SEQ = 8192
DEPTH = 4
N_MIXERS = 3
GRID_W = 64
D_FF = 2816
NORM_EPS = 1e-6
NEG_INF = -1e30

A_GROUPS = ((128, 1), (512, 4), (2048, 16))
A_NG = 3
A_HEAD_DIM = 64
A_HEADS = 16
A_QKV = 3 * A_NG * A_HEADS * A_HEAD_DIM
A_RADIUS = 64

B_HEAD_DIM = 128
B_HEADS = 8
B_KV_HEADS = 2
B_ROPE_THETA = 10000.0
B_QKV = (B_HEADS + 2 * B_KV_HEADS) * B_HEAD_DIM

C_HEADS = 16
C_Q_LORA = 384
C_KV_LORA = 256
C_NOPE = 64
C_ROPE = 32
C_V = 64
C_ROPE_THETA = 10000.0
C_IN_PAD = 768

LANES = 128
ONES_ROWS = 16
LOG2E = 1.4426950408889634
VMEM_LIMIT = 56 * 1024 * 1024

TM_PROJ = 256
TN_PROJ = 1024
TM_ROW = 512
TF_FFN = 2816
TQ = 512
TK = 512
FLASH_UNROLL = 4
A_QB = 256
A_W = A_QB + 2 * A_RADIUS
A_UNROLL = 4


def _cparams(sem):
    return pltpu.CompilerParams(dimension_semantics=sem, vmem_limit_bytes=VMEM_LIMIT)


def _rms(x, gain):
    ms = jnp.mean(x * x, axis=-1, keepdims=True)
    return (x * lax.rsqrt(ms + NORM_EPS)) * gain


def _norm_matmul_kernel(x_ref, g_ref, w_ref, o_ref):
    xn = _rms(x_ref[...], g_ref[...]).astype(BF16)
    for c in range(o_ref.shape[1] // TN_PROJ):
        cols = slice(c * TN_PROJ, (c + 1) * TN_PROJ)
        o_ref[:, cols] = jnp.dot(xn, w_ref[:, cols], preferred_element_type=F32)


def norm_matmul(h, gain, w_stack, layer):
    t, d = h.shape
    n = w_stack.shape[2]
    return pl.pallas_call(
        _norm_matmul_kernel,
        grid=(t // TM_PROJ,),
        in_specs=[
            pl.BlockSpec((TM_PROJ, d), lambda i: (i, 0)),
            pl.BlockSpec((1, d), lambda i: (0, 0)),
            pl.BlockSpec((None, d, n), lambda i: (layer, 0, 0), pipeline_mode=pl.Buffered(1)),
        ],
        out_specs=pl.BlockSpec((TM_PROJ, n), lambda i: (i, 0)),
        out_shape=jax.ShapeDtypeStruct((t, n), F32),
        compiler_params=_cparams(("parallel",)),
        name="a_qkv_proj",
    )(h, gain, w_stack)


def _a_attn_kernel(slopes_ref, q_ref, k_ref, v_ref, o_ref, m_ref, l_ref, acc_ref,
                   s_ref, mf_ref, bias_ref):
    hp = pl.program_id(1)
    g = pl.program_id(2)
    lane = lax.broadcasted_iota(jnp.int32, (A_QB, LANES), 1)
    lo = lane < A_HEAD_DIM
    head_lanes = (lo, jnp.logical_not(lo))
    lo_w = lax.broadcasted_iota(jnp.int32, (A_W, LANES), 1) < A_HEAD_DIM
    head_lanes_w = (lo_w, jnp.logical_not(lo_w))
    rel = (lax.broadcasted_iota(jnp.int32, (A_QB, A_W), 1)
           - lax.broadcasted_iota(jnp.int32, (A_QB, A_W), 0))

    def group_body(step, gi, d):
        first = step == 0
        final = step == A_NG - 1
        cls_len = SEQ // d
        nblk = cls_len // A_QB
        nit = d * nblk

        for oi in range(3):
            dist = jnp.abs(rel - oi * A_RADIUS)
            for h2 in range(2):
                slope = slopes_ref[gi * A_HEADS + hp * 2 + h2] * (d * LOG2E)
                bias_ref[oi, h2] = jnp.where(dist <= A_RADIUS, -slope * dist.astype(F32), NEG_INF)

        def geometry(it):
            r = it // nblk
            i0 = (it % nblk) * A_QB
            if isinstance(it, int):
                kstart = min(max(i0 - A_RADIUS, 0), cls_len - A_W)
            else:
                kstart = jnp.clip(i0 - A_RADIUS, 0, cls_len - A_W)
            oi = (i0 - kstart) // A_RADIUS
            if d == 1 and isinstance(it, int):
                q_rows = pl.ds(i0, A_QB)
                k_rows = pl.ds(kstart, A_W)
            elif d == 1:
                q_rows = pl.ds(pl.multiple_of(i0, A_QB), A_QB)
                k_rows = pl.ds(pl.multiple_of(kstart, A_RADIUS), A_W)
            else:
                q_rows = pl.ds(r + i0 * d, A_QB, stride=d)
                k_rows = pl.ds(r + kstart * d, A_W, stride=d)
            return q_rows, k_rows, oi

        def load_qk(it):
            q_rows, k_rows, oi = geometry(it)
            q = (q_ref[q_rows, :] * (A_HEAD_DIM ** -0.5 * LOG2E)).astype(BF16)
            return q, k_ref[k_rows, :].astype(BF16), oi

        def scores(qk, slot, h2):
            q, k, oi = qk
            qm = jnp.where(head_lanes[h2], q, jnp.zeros_like(q))
            s = lax.dot_general(qm, k, (((1,), (1,)), ((), ())), preferred_element_type=F32)
            s = s + bias_ref[oi, h2]
            s_ref[slot, h2] = s
            mf_ref[slot, h2] = jnp.broadcast_to(jnp.max(s, axis=1, keepdims=True), (A_QB, LANES))

        def weighted_values(v, slot, h2):
            va = jnp.where(head_lanes_w[h2], v, 1.0).astype(BF16)
            mfull = mf_ref[slot, h2]
            p = jnp.exp2(s_ref[slot, h2] - jnp.concatenate([mfull] * (A_W // LANES), axis=1))
            return jnp.dot(p.astype(BF16), va, preferred_element_type=F32)

        def merge(q_rows, slot, outs):
            m_blk = jnp.where(lo, mf_ref[slot, 0], mf_ref[slot, 1])
            pv_blk = jnp.where(lo, outs[0], outs[1])
            l_blk = pltpu.roll(jnp.where(lo, outs[1], outs[0]), A_HEAD_DIM, 1)
            if first:
                m_ref[q_rows, :] = m_blk
                l_ref[q_rows, :] = l_blk
                acc_ref[q_rows, :] = pv_blk
            else:
                m_old = m_ref[q_rows, :]
                m_new = jnp.maximum(m_old, m_blk)
                a_old = jnp.exp2(m_old - m_new)
                a_blk = jnp.exp2(m_blk - m_new)
                l_new = a_old * l_ref[q_rows, :] + a_blk * l_blk
                acc_new = a_old * acc_ref[q_rows, :] + a_blk * pv_blk
                if final:
                    acc_ref[q_rows, :] = acc_new / l_new
                else:
                    m_ref[q_rows, :] = m_new
                    l_ref[q_rows, :] = l_new
                    acc_ref[q_rows, :] = acc_new

        qk0 = load_qk(0)
        for h2 in range(2):
            scores(qk0, 0, h2)

        def blocks(it0, last):
            for u in range(A_UNROLL):
                has_next = not (last and u == A_UNROLL - 1)
                qk = load_qk(it0 + u + 1) if has_next else None
                q_rows, k_rows, _ = geometry(it0 + u)
                v = v_ref[k_rows, :]
                outs = []
                for h2 in range(2):
                    if has_next:
                        scores(qk, (u + 1) % 2, h2)
                    outs.append(weighted_values(v, u % 2, h2))
                merge(q_rows, u % 2, outs)

        def body(jj, carry):
            blocks(A_UNROLL * jj, False)
            return carry

        lax.fori_loop(0, nit // A_UNROLL - 1, body, 0)
        blocks(nit - A_UNROLL, True)
        if final:
            o_ref[...] = acc_ref[...].astype(o_ref.dtype)

    for step in range(A_NG):
        gi = A_NG - 1 - step
        pl.when(g == step)(functools.partial(group_body, step, gi, A_GROUPS[gi][1]))


def a_attention(qkv, slopes):
    npairs = A_HEADS // 2

    def col(which):
        return lambda b, hp, g: (b, 0, (which * A_NG + (A_NG - 1 - g)) * npairs + hp)

    return pl.pallas_call(
        _a_attn_kernel,
        grid=(BATCH, npairs, A_NG),
        in_specs=[
            pl.BlockSpec(memory_space=pltpu.SMEM),
            pl.BlockSpec((None, SEQ, LANES), col(0)),
            pl.BlockSpec((None, SEQ, LANES), col(1)),
            pl.BlockSpec((None, SEQ, LANES), col(2)),
        ],
        out_specs=pl.BlockSpec((None, SEQ, LANES), lambda b, hp, g: (b, 0, hp)),
        out_shape=jax.ShapeDtypeStruct((BATCH, SEQ, D_MODEL), BF16),
        scratch_shapes=[
            pltpu.VMEM((SEQ, LANES), F32),
            pltpu.VMEM((SEQ, LANES), F32),
            pltpu.VMEM((SEQ, LANES), F32),
            pltpu.VMEM((2, 2, A_QB, A_W), F32),
            pltpu.VMEM((2, 2, A_QB, LANES), F32),
            pltpu.VMEM((3, 2, A_QB, A_W), F32),
        ],
        compiler_params=_cparams(("parallel", "parallel", "arbitrary")),
        name="a_attention",
    )(slopes, qkv, qkv, qkv)


def _flash_kernel(qt_ref, k_ref, vt_ref, o_ref, m_ref, acc_ref, s_ref, mc_ref, *, nh, dv, shared_kv):
    m_ref[...] = jnp.full(m_ref.shape, -jnp.inf, F32)
    acc_ref[...] = jnp.zeros(acc_ref.shape, F32)
    ones = jnp.ones((ONES_ROWS, TK), BF16)
    nchunk = SEQ // TK

    def chunk_rows(chunk):
        start = chunk * TK
        return pl.ds(start if isinstance(start, int) else pl.multiple_of(start, TK), TK)

    def scores(chunk, slot, h):
        rows = chunk_rows(chunk)
        k = k_ref[rows, :] if shared_kv else k_ref[rows, h * LANES:(h + 1) * LANES]
        st = jnp.dot(k, qt_ref[h * LANES:(h + 1) * LANES, :], preferred_element_type=F32)
        s_ref[slot, h] = st
        mc_ref[slot, h] = jnp.max(st, axis=0, keepdims=True)

    def update(chunk, slot, h):
        rows = chunk_rows(chunk)
        vt = vt_ref[:, rows] if shared_kv else vt_ref[h * dv:(h + 1) * dv, rows]
        m_prev = m_ref[h]
        m_new = jnp.maximum(m_prev, mc_ref[slot, h])
        alpha = jnp.exp2(m_prev - m_new)
        pt = jnp.exp2(s_ref[slot, h] - m_new).astype(BF16)
        pv = jnp.dot(jnp.concatenate([vt, ones], axis=0), pt, preferred_element_type=F32)
        acc_ref[h] = alpha * acc_ref[h] + pv
        m_ref[h] = m_new

    for h in range(nh):
        scores(0, 0, h)

    def chunks(j, last):
        for u in range(FLASH_UNROLL):
            for h in range(nh):
                if not (last and u == FLASH_UNROLL - 1):
                    scores(j + u + 1, (u + 1) % 2, h)
                update(j + u, u % 2, h)

    def body(jj, carry):
        chunks(FLASH_UNROLL * jj, False)
        return carry

    lax.fori_loop(0, nchunk // FLASH_UNROLL - 1, body, 0)
    chunks(nchunk - FLASH_UNROLL, True)

    outs = []
    for h in range(nh):
        acc = acc_ref[h]
        outs.append(acc[:dv, :] / acc[dv:dv + 1, :])
    ot = outs[0] if nh == 1 else jnp.concatenate(outs, axis=0)
    o_ref[...] = ot.T.astype(o_ref.dtype)


def _flash_call(kernel, grid, nh, dv, in_specs, out_spec, args, name):
    return pl.pallas_call(
        kernel,
        grid=grid,
        in_specs=in_specs,
        out_specs=out_spec,
        out_shape=jax.ShapeDtypeStruct((BATCH, SEQ, D_MODEL), BF16),
        scratch_shapes=[
            pltpu.VMEM((nh, 1, TQ), F32),
            pltpu.VMEM((nh, dv + ONES_ROWS, TQ), F32),
            pltpu.VMEM((2, nh, TK, TQ), F32),
            pltpu.VMEM((2, nh, 1, TQ), F32),
        ],
        compiler_params=_cparams(("parallel",) * (len(grid) - 1) + ("arbitrary",)),
        name=name,
    )(*args)


def b_attention(qt, k, vt):
    grp = B_HEADS // B_KV_HEADS
    nh = 2
    kernel = functools.partial(_flash_kernel, nh=nh, dv=B_HEAD_DIM, shared_kv=True)
    in_specs = [
        pl.BlockSpec((None, nh * LANES, TQ), lambda b, kv, g, i: (b, kv * (grp // nh) + g, i)),
        pl.BlockSpec((None, SEQ, LANES), lambda b, kv, g, i: (b, 0, kv)),
        pl.BlockSpec((None, LANES, SEQ), lambda b, kv, g, i: (b, kv, 0)),
    ]
    out_spec = pl.BlockSpec((None, TQ, nh * LANES), lambda b, kv, g, i: (b, i, kv * (grp // nh) + g))
    grid = (BATCH, B_KV_HEADS, grp // nh, SEQ // TQ)
    return _flash_call(kernel, grid, nh, B_HEAD_DIM, in_specs, out_spec, (qt, k, vt), "b_attention")


def c_attention(qt, k, vt):
    nh = 2
    kernel = functools.partial(_flash_kernel, nh=nh, dv=C_V, shared_kv=False)
    in_specs = [
        pl.BlockSpec((None, nh * LANES, TQ), lambda b, p, i: (b, p, i)),
        pl.BlockSpec((None, SEQ, nh * LANES), lambda b, p, i: (b, 0, p)),
        pl.BlockSpec((None, nh * C_V, SEQ), lambda b, p, i: (b, p, 0)),
    ]
    out_spec = pl.BlockSpec((None, TQ, nh * C_V), lambda b, p, i: (b, i, p))
    grid = (BATCH, C_HEADS // nh, SEQ // TQ)
    return _flash_call(kernel, grid, nh, C_V, in_specs, out_spec, (qt, k, vt), "c_attention")


def _rope_lanes(x, cos, sin_a, sin_b, shift):
    return (x * cos + pltpu.roll(x, LANES - shift, 1) * sin_a + pltpu.roll(x, shift, 1) * sin_b)


def _nt_dot(a, b):
    return lax.dot_general(a, b, (((1,), (1,)), ((), ())), preferred_element_type=F32)


def _swap_rows(x, start, half, groups):
    pieces = [x[:start]] if start else []
    for g0 in range(start, start + 2 * half * groups, 2 * half):
        pieces += [x[g0 + half:g0 + 2 * half], x[g0:g0 + half]]
    if start + 2 * half * groups < x.shape[0]:
        pieces.append(x[start + 2 * half * groups:])
    return jnp.concatenate(pieces, axis=0)


def _b_prep_kernel(h_ref, g_ref, wqt_ref, wk_ref, wvt_ref, qg_ref, kn_ref, cost_ref, sint_ref,
                   cos_ref, sa_ref, sb_ref, qt_ref, k_ref, vt_ref):
    xn = _rms(h_ref[...], g_ref[...]).astype(BF16)
    cos_t, sin_t, qgain = cost_ref[...], sint_ref[...], qg_ref[...]
    cos, sa, sb = cos_ref[...], sa_ref[...], sb_ref[...]
    half_rows = B_HEADS * LANES // 2

    def q_heads(qt_half, first_head):
        for hd in range(B_HEADS // 2):
            x = qt_half[hd * LANES:(hd + 1) * LANES]
            ms = jnp.mean(x * x, axis=0, keepdims=True)
            x = (x * lax.rsqrt(ms + NORM_EPS)) * qgain
            x = x * cos_t + _swap_rows(x, 0, B_HEAD_DIM // 4, 2) * sin_t
            out_rows = slice((first_head + hd) * LANES, (first_head + hd + 1) * LANES)
            qt_ref[out_rows, :] = x.astype(BF16)

    kk = jnp.dot(xn, wk_ref[...], preferred_element_type=F32)
    qt_a = _nt_dot(wqt_ref[:half_rows, :], xn)
    for j in range(B_KV_HEADS):
        cols = slice(j * LANES, (j + 1) * LANES)
        x = _rope_lanes(_rms(kk[:, cols], kn_ref[...]), cos, sa, sb, B_HEAD_DIM // 4)
        k_ref[:, cols] = x.astype(BF16)
    qt_b = _nt_dot(wqt_ref[half_rows:, :], xn)
    q_heads(qt_a, 0)
    vt = _nt_dot(wvt_ref[...], xn)
    q_heads(qt_b, B_HEADS // 2)
    vt_ref[...] = vt.astype(BF16)


def b_prep(h, gain, wq_t, wk, wv_t, qgain_t, knorm, tables_t, tables):
    t, d = h.shape
    nrow = SEQ // TM_ROW
    row = lambda i: (i, 0)
    fixed = lambda i: (0, 0)
    tab = lambda i: (i % nrow, 0)
    tab_t = lambda i: (0, i % nrow)
    colblk = lambda i: (i // nrow, 0, i % nrow)
    nq = B_HEADS * B_HEAD_DIM
    nkv = B_KV_HEADS * B_HEAD_DIM
    return pl.pallas_call(
        _b_prep_kernel,
        grid=(t // TM_ROW,),
        in_specs=[
            pl.BlockSpec((TM_ROW, d), row),
            pl.BlockSpec((1, d), fixed),
            pl.BlockSpec((nq, d), fixed),
            pl.BlockSpec((d, nkv), fixed),
            pl.BlockSpec((nkv, d), fixed),
            pl.BlockSpec((B_HEAD_DIM, TM_ROW), fixed),
            pl.BlockSpec((1, B_HEAD_DIM), fixed),
            pl.BlockSpec((LANES, TM_ROW), tab_t),
            pl.BlockSpec((LANES, TM_ROW), tab_t),
            pl.BlockSpec((TM_ROW, LANES), tab),
            pl.BlockSpec((TM_ROW, LANES), tab),
            pl.BlockSpec((TM_ROW, LANES), tab),
        ],
        out_specs=[
            pl.BlockSpec((None, nq, TM_ROW), colblk),
            pl.BlockSpec((TM_ROW, nkv), row),
            pl.BlockSpec((None, nkv, TM_ROW), colblk),
        ],
        out_shape=[
            jax.ShapeDtypeStruct((BATCH, nq, SEQ), BF16),
            jax.ShapeDtypeStruct((t, nkv), BF16),
            jax.ShapeDtypeStruct((BATCH, nkv, SEQ), BF16),
        ],
        compiler_params=_cparams(("parallel",)),
        name="b_prep",
    )(h, gain, wq_t, wk, wv_t, qgain_t, knorm, *tables_t, *tables)


def _c_prep_kernel(h_ref, g_ref, win_ref, qn_ref, kvn_ref, wuqt_ref, wuk_ref, wuvt_ref,
                   cost_ref, sint_ref, cos_ref, sa_ref, sb_ref, q_ref, k_ref, v_ref):
    xn = _rms(h_ref[...], g_ref[...]).astype(BF16)
    c = jnp.dot(xn, win_ref[...], preferred_element_type=F32)
    cq = _rms(c[:, :C_Q_LORA], qn_ref[...]).astype(BF16)
    ckv = _rms(c[:, C_Q_LORA:C_Q_LORA + C_KV_LORA], kvn_ref[...]).astype(BF16)
    cos, sa, sb = cos_ref[...], sa_ref[...], sb_ref[...]
    cos_t, sin_t = cost_ref[...], sint_ref[...]
    half_rows = C_HEADS * LANES // 2

    def q_heads(qt_half, first_head):
        for hd in range(C_HEADS // 2):
            x = qt_half[hd * LANES:(hd + 1) * LANES]
            x = x * cos_t + _swap_rows(x, C_NOPE, C_ROPE // 2, 1) * sin_t
            out_rows = slice((first_head + hd) * LANES, (first_head + hd + 1) * LANES)
            q_ref[out_rows, :] = x.astype(BF16)

    kn = jnp.dot(ckv, wuk_ref[...], preferred_element_type=F32)
    qt_a = _nt_dot(wuqt_ref[:half_rows, :], cq)
    kr = _rope_lanes(c[:, C_Q_LORA + C_KV_LORA:], cos, sa, sb, C_ROPE // 2)
    for hd in range(C_HEADS):
        cols = slice(hd * LANES, (hd + 1) * LANES)
        k_ref[:, cols] = (kn[:, cols] + kr).astype(BF16)
    qt_b = _nt_dot(wuqt_ref[half_rows:, :], cq)
    q_heads(qt_a, 0)
    vt = _nt_dot(wuvt_ref[...], ckv)
    q_heads(qt_b, C_HEADS // 2)
    v_ref[...] = vt.astype(BF16)


def c_prep(h, gain, win, qnorm, kvnorm, wuq_t, wuk, wuv_t, tables_t, tables):
    t, d = h.shape
    nrow = SEQ // TM_ROW
    row = lambda i: (i, 0)
    fixed = lambda i: (0, 0)
    tab = lambda i: (i % nrow, 0)
    tab_t = lambda i: (0, i % nrow)
    colblk = lambda i: (i // nrow, 0, i % nrow)
    hq = C_HEADS * LANES
    return pl.pallas_call(
        _c_prep_kernel,
        grid=(t // TM_ROW,),
        in_specs=[
            pl.BlockSpec((TM_ROW, d), row),
            pl.BlockSpec((1, d), fixed),
            pl.BlockSpec((d, C_IN_PAD), fixed),
            pl.BlockSpec((1, C_Q_LORA), fixed),
            pl.BlockSpec((1, C_KV_LORA), fixed),
            pl.BlockSpec((hq, C_Q_LORA), fixed),
            pl.BlockSpec((C_KV_LORA, hq), fixed),
            pl.BlockSpec((C_HEADS * C_V, C_KV_LORA), fixed),
            pl.BlockSpec((LANES, TM_ROW), tab_t),
            pl.BlockSpec((LANES, TM_ROW), tab_t),
            pl.BlockSpec((TM_ROW, LANES), tab),
            pl.BlockSpec((TM_ROW, LANES), tab),
            pl.BlockSpec((TM_ROW, LANES), tab),
        ],
        out_specs=[
            pl.BlockSpec((None, hq, TM_ROW), colblk),
            pl.BlockSpec((TM_ROW, hq), row),
            pl.BlockSpec((None, C_HEADS * C_V, TM_ROW), colblk),
        ],
        out_shape=[
            jax.ShapeDtypeStruct((BATCH, hq, SEQ), BF16),
            jax.ShapeDtypeStruct((t, hq), BF16),
            jax.ShapeDtypeStruct((BATCH, C_HEADS * C_V, SEQ), BF16),
        ],
        compiler_params=_cparams(("parallel",)),
        name="c_prep",
    )(h, gain, win, qnorm, kvnorm, wuq_t, wuk, wuv_t, *tables_t, *tables)


def _mix_ffn_kernel(y_ref, wo_ref, gmix_ref, h_ref, gpre_ref, wg_ref, wu_ref, wd_ref, gpost_ref,
                    o_ref, h1_ref, xn_ref, acc_ref):
    f = pl.program_id(1)

    @pl.when(f == 0)
    def _():
        z = jnp.dot(y_ref[...], wo_ref[...], preferred_element_type=F32)
        h1 = h_ref[...] + _rms(z, gmix_ref[...])
        h1_ref[...] = h1
        xn_ref[...] = _rms(h1, gpre_ref[...]).astype(BF16)

    xn = xn_ref[...]
    gate = jnp.dot(xn, wg_ref[...], preferred_element_type=F32)
    up = jnp.dot(xn, wu_ref[...], preferred_element_type=F32)
    act = (gate * jax.nn.sigmoid(gate)) * up
    part = jnp.dot(act.astype(BF16), wd_ref[...], preferred_element_type=F32)

    @pl.when(f == 0)
    def _():
        acc_ref[...] = part

    @pl.when(f > 0)
    def _():
        acc_ref[...] += part

    @pl.when(f == pl.num_programs(1) - 1)
    def _():
        o_ref[...] = h1_ref[...] + _rms(acc_ref[...], gpost_ref[...])


def mix_ffn(y, wo, gmix, h, gpre, wg, wu, wd, layer, gpost):
    t, d = h.shape
    row = lambda i, f: (i, 0)
    fixed = lambda i, f: (0, 0)
    wmode = pl.Buffered(1) if TF_FFN == D_FF else None
    return pl.pallas_call(
        _mix_ffn_kernel,
        grid=(t // TM_ROW, D_FF // TF_FFN),
        in_specs=[
            pl.BlockSpec((TM_ROW, d), row),
            pl.BlockSpec((d, d), fixed),
            pl.BlockSpec((1, d), fixed),
            pl.BlockSpec((TM_ROW, d), row),
            pl.BlockSpec((1, d), fixed),
            pl.BlockSpec((None, d, TF_FFN), lambda i, f: (layer, 0, f), pipeline_mode=wmode),
            pl.BlockSpec((None, d, TF_FFN), lambda i, f: (layer, 0, f), pipeline_mode=wmode),
            pl.BlockSpec((None, TF_FFN, d), lambda i, f: (layer, f, 0), pipeline_mode=wmode),
            pl.BlockSpec((1, d), fixed),
        ],
        out_specs=pl.BlockSpec((TM_ROW, d), row),
        out_shape=jax.ShapeDtypeStruct((t, d), F32),
        scratch_shapes=[
            pltpu.VMEM((TM_ROW, d), F32),
            pltpu.VMEM((TM_ROW, d), BF16),
            pltpu.VMEM((TM_ROW, d), F32),
        ],
        compiler_params=_cparams(("parallel", "arbitrary")),
        name="mix_ffn",
    )(y, wo, gmix, h, gpre, wg, wu, wd, gpost)


def _rope_tables(pos_by_lane, freq_idx, dim, theta, active, first_half):
    freqs = jnp.power(jnp.float32(theta), -freq_idx.astype(F32) * 2.0 / dim)
    ang = pos_by_lane * freqs[None, :]
    cos = jnp.where(active[None, :], jnp.cos(ang), 1.0)
    sin = jnp.sin(ang)
    sin_a = jnp.where((active & first_half)[None, :], -sin, 0.0)
    sin_b = jnp.where((active & ~first_half)[None, :], sin, 0.0)
    return cos.astype(F32), sin_a.astype(F32), sin_b.astype(F32)


def _b_tables():
    lane = np.arange(LANES)
    half = B_HEAD_DIM // 2
    quarter = half // 2
    t = jnp.arange(SEQ)
    row = (t // GRID_W).astype(F32)
    col = (t % GRID_W).astype(F32)
    pos = jnp.where(jnp.asarray(lane < half)[None, :], row[:, None], col[:, None])
    return _rope_tables(pos, jnp.asarray(lane % quarter), half, B_ROPE_THETA,
                        jnp.asarray(np.ones(LANES, bool)), jnp.asarray((lane % half) < quarter))


def _c_tables():
    lane = np.arange(LANES)
    half = C_ROPE // 2
    active = (lane >= C_NOPE) & (lane < C_NOPE + C_ROPE)
    pos = jnp.broadcast_to(jnp.arange(SEQ, dtype=F32)[:, None], (SEQ, LANES))
    return _rope_tables(pos, jnp.asarray((lane - C_NOPE) % half), C_ROPE, C_ROPE_THETA,
                        jnp.asarray(active), jnp.asarray((lane - C_NOPE) < half))


def _rope_tables_t(pos_by_row, freq_idx, dim, theta, active, first_half, scale):
    freqs = jnp.power(jnp.float32(theta), -freq_idx.astype(F32) * 2.0 / dim)
    ang = freqs[:, None] * pos_by_row
    cos = jnp.where(active[:, None], jnp.cos(ang), 1.0) * scale
    sin = jnp.sin(ang) * scale
    sin = jnp.where(active[:, None], jnp.where(first_half[:, None], -sin, sin), 0.0)
    return cos.astype(F32), sin.astype(F32)


def _b_tables_t():
    dim = np.arange(LANES)
    half = B_HEAD_DIM // 2
    quarter = half // 2
    t = jnp.arange(SEQ)
    row = (t // GRID_W).astype(F32)
    col = (t % GRID_W).astype(F32)
    pos = jnp.where(jnp.asarray(dim < half)[:, None], row[None, :], col[None, :])
    return _rope_tables_t(pos, jnp.asarray(dim % quarter), half, B_ROPE_THETA,
                          jnp.asarray(np.ones(LANES, bool)), jnp.asarray((dim % half) < quarter),
                          B_HEAD_DIM ** -0.5 * LOG2E)


def _c_tables_t():
    dim = np.arange(LANES)
    half = C_ROPE // 2
    active = (dim >= C_NOPE) & (dim < C_NOPE + C_ROPE)
    pos = jnp.broadcast_to(jnp.arange(SEQ, dtype=F32)[None, :], (LANES, SEQ))
    return _rope_tables_t(pos, jnp.asarray((dim - C_NOPE) % half), C_ROPE, C_ROPE_THETA,
                          jnp.asarray(active), jnp.asarray((dim - C_NOPE) < half),
                          (C_NOPE + C_ROPE) ** -0.5 * LOG2E)


def _alibi_slopes():
    n = A_NG * A_HEADS
    return jnp.asarray(2.0 ** (-8.0 * np.arange(1, n + 1) / n), dtype=F32)


def _mixer_a(h, gain, wqkv_stack, layer):
    qkv = norm_matmul(h, gain, wqkv_stack, layer)
    return a_attention(qkv.reshape(BATCH, SEQ, A_QKV), _alibi_slopes())


def _mixer_b(h, gain, wqkv, qnorm, knorm):
    nq = B_HEADS * B_HEAD_DIM
    nkv = B_KV_HEADS * B_HEAD_DIM
    w = wqkv.astype(BF16)
    qgain_t = jnp.broadcast_to(qnorm[:, None], (B_HEAD_DIM, TM_ROW))
    qt, k, vt = b_prep(h, gain, w[:, :nq].T, w[:, nq:nq + nkv], w[:, nq + nkv:].T, qgain_t,
                       knorm[None, :], _b_tables_t(), _b_tables())
    return b_attention(qt, k.reshape(BATCH, SEQ, -1), vt)


def _mixer_c(h, gain, win, qnorm, kvnorm, wuq, wukv):
    win_p = jnp.zeros((D_MODEL, C_IN_PAD), F32)
    win_p = win_p.at[:, :C_Q_LORA + C_KV_LORA].set(win[:, :C_Q_LORA + C_KV_LORA])
    kr0 = C_Q_LORA + C_KV_LORA + C_NOPE
    win_p = win_p.at[:, kr0:kr0 + C_ROPE].set(win[:, C_Q_LORA + C_KV_LORA:])
    wuq_p = jnp.pad(wuq.reshape(C_Q_LORA, C_HEADS, C_NOPE + C_ROPE),
                    ((0, 0), (0, 0), (0, LANES - C_NOPE - C_ROPE))).reshape(C_Q_LORA, C_HEADS * LANES)
    wukv3 = wukv.reshape(C_KV_LORA, C_HEADS, C_NOPE + C_V)
    wuk_p = jnp.pad(wukv3[:, :, :C_NOPE],
                    ((0, 0), (0, 0), (0, LANES - C_NOPE))).reshape(C_KV_LORA, C_HEADS * LANES)
    wuv = wukv3[:, :, C_NOPE:].reshape(C_KV_LORA, C_HEADS * C_V)
    qt, k, vt = c_prep(h, gain, win_p.astype(BF16), qnorm[None, :], kvnorm[None, :],
                       wuq_p.astype(BF16).T, wuk_p.astype(BF16), wuv.astype(BF16).T,
                       _c_tables_t(), _c_tables())
    return c_attention(qt, k.reshape(BATCH, SEQ, -1), vt)


def kernel(x, norm_mix_pre, norm_mix_post, norm_ffn_pre, norm_ffn_post, ffn_wg, ffn_wu, ffn_wd,
           a_wqkv, a_wo, b_wqkv, b_qnorm, b_knorm, b_wo,
           c_win, c_qnorm, c_kvnorm, c_wuq, c_wukv, c_wo):
    h = x.reshape(BATCH * SEQ, D_MODEL)
    a_wqkv_b = a_wqkv.astype(BF16)
    wg_b, wu_b, wd_b = ffn_wg.astype(BF16), ffn_wu.astype(BF16), ffn_wd.astype(BF16)
    for i in range(DEPTH):
        kind = i % N_MIXERS
        j = i // N_MIXERS
        gpre = norm_mix_pre[i][None, :]
        if kind == 0:
            y, wo = _mixer_a(h, gpre, a_wqkv_b, j), a_wo[j]
        elif kind == 1:
            y, wo = _mixer_b(h, gpre, b_wqkv[j], b_qnorm[j], b_knorm[j]), b_wo[j]
        else:
            y, wo = _mixer_c(h, gpre, c_win[j], c_qnorm[j], c_kvnorm[j], c_wuq[j], c_wukv[j]), c_wo[j]
        h = mix_ffn(y.reshape(BATCH * SEQ, D_MODEL), wo.astype(BF16), norm_mix_post[i][None, :], h,
                    norm_ffn_pre[i][None, :], wg_b, wu_b, wd_b, i, norm_ffn_post[i][None, :])
    return h.reshape(BATCH, SEQ, D_MODEL)
```

```python
import functools

import numpy as np
import jax
import jax.numpy as jnp
from jax import lax
from jax.experimental import pallas as pl
from jax.experimental.pallas import tpu as pltpu

F32 = jnp.float32
BF16 = jnp.bfloat16

D_MODEL = 1024
BATCH = 2
SEQ = 8192
DEPTH = 4
N_MIXERS = 3
GRID_W = 64
D_FF = 2816
NORM_EPS = 1e-6
NEG_INF = -1e30

A_GROUPS = ((128, 1), (512, 4), (2048, 16))
A_NG = 3
A_HEAD_DIM = 64
A_HEADS = 16
A_QKV = 3 * A_NG * A_HEADS * A_HEAD_DIM
A_RADIUS = 64

B_HEAD_DIM = 128
B_HEADS = 8
B_KV_HEADS = 2
B_ROPE_THETA = 10000.0
B_QKV = (B_HEADS + 2 * B_KV_HEADS) * B_HEAD_DIM

C_HEADS = 16
C_Q_LORA = 384
C_KV_LORA = 256
C_NOPE = 64
C_ROPE = 32
C_V = 64
C_ROPE_THETA = 10000.0
C_IN_PAD = 768

LANES = 128
ONES_ROWS = 16
LOG2E = 1.4426950408889634
VMEM_LIMIT = 56 * 1024 * 1024

TM_PROJ = 256
TN_PROJ = 1024
TM_ROW = 512
TF_FFN = 2816
TQ = 512
TK = 512
FLASH_UNROLL = 4
A_QB = 256
A_W = A_QB + 2 * A_RADIUS
A_UNROLL = 4


def _cparams(sem):
    return pltpu.CompilerParams(dimension_semantics=sem, vmem_limit_bytes=VMEM_LIMIT)


def _rms(x, gain):
    ms = jnp.mean(x * x, axis=-1, keepdims=True)
    return (x * lax.rsqrt(ms + NORM_EPS)) * gain


def _norm_matmul_kernel(x_ref, g_ref, w_ref, o_ref):
    xn = _rms(x_ref[...], g_ref[...]).astype(BF16)
    for c in range(o_ref.shape[1] // TN_PROJ):
        cols = slice(c * TN_PROJ, (c + 1) * TN_PROJ)
        o_ref[:, cols] = jnp.dot(xn, w_ref[:, cols], preferred_element_type=F32)


def norm_matmul(h, gain, w_stack, layer):
    t, d = h.shape
    n = w_stack.shape[2]
    return pl.pallas_call(
        _norm_matmul_kernel,
        grid=(t // TM_PROJ,),
        in_specs=[
            pl.BlockSpec((TM_PROJ, d), lambda i: (i, 0)),
            pl.BlockSpec((1, d), lambda i: (0, 0)),
            pl.BlockSpec((None, d, n), lambda i: (layer, 0, 0), pipeline_mode=pl.Buffered(1)),
        ],
        out_specs=pl.BlockSpec((TM_PROJ, n), lambda i: (i, 0)),
        out_shape=jax.ShapeDtypeStruct((t, n), F32),
        compiler_params=_cparams(("parallel",)),
        name="a_qkv_proj",
    )(h, gain, w_stack)


def _a_attn_kernel(slopes_ref, q_ref, k_ref, v_ref, o_ref, m_ref, l_ref, acc_ref,
                   s_ref, mf_ref, bias_ref):
    hp = pl.program_id(1)
    g = pl.program_id(2)
    lane = lax.broadcasted_iota(jnp.int32, (A_QB, LANES), 1)
    lo = lane < A_HEAD_DIM
    head_lanes = (lo, jnp.logical_not(lo))
    lo_w = lax.broadcasted_iota(jnp.int32, (A_W, LANES), 1) < A_HEAD_DIM
    head_lanes_w = (lo_w, jnp.logical_not(lo_w))
    rel = (lax.broadcasted_iota(jnp.int32, (A_QB, A_W), 1)
           - lax.broadcasted_iota(jnp.int32, (A_QB, A_W), 0))

    def group_body(step, gi, d):
        first = step == 0
        final = step == A_NG - 1
        cls_len = SEQ // d
        nblk = cls_len // A_QB
        nit = d * nblk

        for oi in range(3):
            dist = jnp.abs(rel - oi * A_RADIUS)
            for h2 in range(2):
                slope = slopes_ref[gi * A_HEADS + hp * 2 + h2] * (d * LOG2E)
                bias_ref[oi, h2] = jnp.where(dist <= A_RADIUS, -slope * dist.astype(F32), NEG_INF)

        def geometry(it):
            r = it // nblk
            i0 = (it % nblk) * A_QB
            if isinstance(it, int):
                kstart = min(max(i0 - A_RADIUS, 0), cls_len - A_W)
            else:
                kstart = jnp.clip(i0 - A_RADIUS, 0, cls_len - A_W)
            oi = (i0 - kstart) // A_RADIUS
            if d == 1 and isinstance(it, int):
                q_rows = pl.ds(i0, A_QB)
                k_rows = pl.ds(kstart, A_W)
            elif d == 1:
                q_rows = pl.ds(pl.multiple_of(i0, A_QB), A_QB)
                k_rows = pl.ds(pl.multiple_of(kstart, A_RADIUS), A_W)
            else:
                q_rows = pl.ds(r + i0 * d, A_QB, stride=d)
                k_rows = pl.ds(r + kstart * d, A_W, stride=d)
            return q_rows, k_rows, oi

        def load_qk(it):
            q_rows, k_rows, oi = geometry(it)
            q = (q_ref[q_rows, :] * (A_HEAD_DIM ** -0.5 * LOG2E)).astype(BF16)
            return q, k_ref[k_rows, :].astype(BF16), oi

        def scores(qk, slot, h2):
            q, k, oi = qk
            qm = jnp.where(head_lanes[h2], q, jnp.zeros_like(q))
            s = lax.dot_general(qm, k, (((1,), (1,)), ((), ())), preferred_element_type=F32)
            s = s + bias_ref[oi, h2]
            s_ref[slot, h2] = s
            mf_ref[slot, h2] = jnp.broadcast_to(jnp.max(s, axis=1, keepdims=True), (A_QB, LANES))

        def weighted_values(v, slot, h2):
            va = jnp.where(head_lanes_w[h2], v, 1.0).astype(BF16)
            mfull = mf_ref[slot, h2]
            p = jnp.exp2(s_ref[slot, h2] - jnp.concatenate([mfull] * (A_W // LANES), axis=1))
            return jnp.dot(p.astype(BF16), va, preferred_element_type=F32)

        def merge(q_rows, slot, outs):
            m_blk = jnp.where(lo, mf_ref[slot, 0], mf_ref[slot, 1])
            pv_blk = jnp.where(lo, outs[0], outs[1])
            l_blk = pltpu.roll(jnp.where(lo, outs[1], outs[0]), A_HEAD_DIM, 1)
            if first:
                m_ref[q_rows, :] = m_blk
                l_ref[q_rows, :] = l_blk
                acc_ref[q_rows, :] = pv_blk
            else:
                m_old = m_ref[q_rows, :]
                m_new = jnp.maximum(m_old, m_blk)
                a_old = jnp.exp2(m_old - m_new)
                a_blk = jnp.exp2(m_blk - m_new)
                l_new = a_old * l_ref[q_rows, :] + a_blk * l_blk
                acc_new = a_old * acc_ref[q_rows, :] + a_blk * pv_blk
                if final:
                    acc_ref[q_rows, :] = acc_new / l_new
                else:
                    m_ref[q_rows, :] = m_new
                    l_ref[q_rows, :] = l_new
                    acc_ref[q_rows, :] = acc_new

        qk0 = load_qk(0)
        for h2 in range(2):
            scores(qk0, 0, h2)

        def blocks(it0, last):
            for u in range(A_UNROLL):
                has_next = not (last and u == A_UNROLL - 1)
                qk = load_qk(it0 + u + 1) if has_next else None
                q_rows, k_rows, _ = geometry(it0 + u)
                v = v_ref[k_rows, :]
                outs = []
                for h2 in range(2):
                    if has_next:
                        scores(qk, (u + 1) % 2, h2)
                    outs.append(weighted_values(v, u % 2, h2))
                merge(q_rows, u % 2, outs)

        def body(jj, carry):
            blocks(A_UNROLL * jj, False)
            return carry

        lax.fori_loop(0, nit // A_UNROLL - 1, body, 0)
        blocks(nit - A_UNROLL, True)
        if final:
            o_ref[...] = acc_ref[...].astype(o_ref.dtype)

    for step in range(A_NG):
        gi = A_NG - 1 - step
        pl.when(g == step)(functools.partial(group_body, step, gi, A_GROUPS[gi][1]))


def a_attention(qkv, slopes):
    npairs = A_HEADS // 2

    def col(which):
        return lambda b, hp, g: (b, 0, (which * A_NG + (A_NG - 1 - g)) * npairs + hp)

    return pl.pallas_call(
        _a_attn_kernel,
        grid=(BATCH, npairs, A_NG),
        in_specs=[
            pl.BlockSpec(memory_space=pltpu.SMEM),
            pl.BlockSpec((None, SEQ, LANES), col(0)),
            pl.BlockSpec((None, SEQ, LANES), col(1)),
            pl.BlockSpec((None, SEQ, LANES), col(2)),
        ],
        out_specs=pl.BlockSpec((None, SEQ, LANES), lambda b, hp, g: (b, 0, hp)),
        out_shape=jax.ShapeDtypeStruct((BATCH, SEQ, D_MODEL), BF16),
        scratch_shapes=[
            pltpu.VMEM((SEQ, LANES), F32),
            pltpu.VMEM((SEQ, LANES), F32),
            pltpu.VMEM((SEQ, LANES), F32),
            pltpu.VMEM((2, 2, A_QB, A_W), F32),
            pltpu.VMEM((2, 2, A_QB, LANES), F32),
            pltpu.VMEM((3, 2, A_QB, A_W), F32),
        ],
        compiler_params=_cparams(("parallel", "parallel", "arbitrary")),
        name="a_attention",
    )(slopes, qkv, qkv, qkv)


def _flash_kernel(qt_ref, k_ref, vt_ref, o_ref, m_ref, acc_ref, s_ref, mc_ref, *, nh, dv, shared_kv):
    m_ref[...] = jnp.full(m_ref.shape, -jnp.inf, F32)
    acc_ref[...] = jnp.zeros(acc_ref.shape, F32)
    ones = jnp.ones((ONES_ROWS, TK), BF16)
    nchunk = SEQ // TK

    def chunk_rows(chunk):
        start = chunk * TK
        return pl.ds(start if isinstance(start, int) else pl.multiple_of(start, TK), TK)

    def scores(chunk, slot, h):
        rows = chunk_rows(chunk)
        k = k_ref[rows, :] if shared_kv else k_ref[rows, h * LANES:(h + 1) * LANES]
        st = jnp.dot(k, qt_ref[h * LANES:(h + 1) * LANES, :], preferred_element_type=F32)
        s_ref[slot, h, :, :TQ] = st
        mc_ref[slot, h] = jnp.max(st, axis=0, keepdims=True)

    def update(chunk, slot, h):
        rows = chunk_rows(chunk)
        vt = vt_ref[:, rows] if shared_kv else vt_ref[h * dv:(h + 1) * dv, rows]
        m_prev = m_ref[h]
        m_new = jnp.maximum(m_prev, mc_ref[slot, h])
        alpha = jnp.exp2(m_prev - m_new)
        pt = jnp.exp2(s_ref[slot, h, :, :TQ] - m_new).astype(BF16)
        pv = jnp.dot(jnp.concatenate([vt, ones], axis=0), pt, preferred_element_type=F32)
        acc_ref[h] = alpha * acc_ref[h] + pv
        m_ref[h] = m_new

    for h in range(nh):
        scores(0, 0, h)

    def chunks(j, last):
        for u in range(FLASH_UNROLL):
            for h in range(nh):
                if not (last and u == FLASH_UNROLL - 1):
                    scores(j + u + 1, (u + 1) % 2, h)
                update(j + u, u % 2, h)

    def body(jj, carry):
        chunks(FLASH_UNROLL * jj, False)
        return carry

    lax.fori_loop(0, nchunk // FLASH_UNROLL - 1, body, 0)
    chunks(nchunk - FLASH_UNROLL, True)

    outs = []
    for h in range(nh):
        acc = acc_ref[h]
        outs.append(acc[:dv, :] / acc[dv:dv + 1, :])
    ot = outs[0] if nh == 1 else jnp.concatenate(outs, axis=0)
    o_ref[...] = ot.T.astype(o_ref.dtype)


def _flash_call(kernel, grid, nh, dv, in_specs, out_spec, args, name):
    return pl.pallas_call(
        kernel,
        grid=grid,
        in_specs=in_specs,
        out_specs=out_spec,
        out_shape=jax.ShapeDtypeStruct((BATCH, SEQ, D_MODEL), BF16),
        scratch_shapes=[
            pltpu.VMEM((nh, 1, TQ), F32),
            pltpu.VMEM((nh, dv + ONES_ROWS, TQ), F32),
            pltpu.VMEM((2, nh, TK, TQ + LANES), F32),
            pltpu.VMEM((2, nh, 1, TQ), F32),
        ],
        compiler_params=_cparams(("parallel",) * (len(grid) - 1) + ("arbitrary",)),
        name=name,
    )(*args)


def b_attention(qt, k, vt):
    grp = B_HEADS // B_KV_HEADS
    nh = 2
    kernel = functools.partial(_flash_kernel, nh=nh, dv=B_HEAD_DIM, shared_kv=True)
    in_specs = [
        pl.BlockSpec((None, nh * LANES, TQ), lambda b, kv, g, i: (b, kv * (grp // nh) + g, i)),
        pl.BlockSpec((None, SEQ, LANES), lambda b, kv, g, i: (b, 0, kv)),
        pl.BlockSpec((None, LANES, SEQ), lambda b, kv, g, i: (b, kv, 0)),
    ]
    out_spec = pl.BlockSpec((None, TQ, nh * LANES), lambda b, kv, g, i: (b, i, kv * (grp // nh) + g))
    grid = (BATCH, B_KV_HEADS, grp // nh, SEQ // TQ)
    return _flash_call(kernel, grid, nh, B_HEAD_DIM, in_specs, out_spec, (qt, k, vt), "b_attention")


def c_attention(qt, k, vt):
    nh = 2
    kernel = functools.partial(_flash_kernel, nh=nh, dv=C_V, shared_kv=False)
    in_specs = [
        pl.BlockSpec((None, nh * LANES, TQ), lambda b, p, i: (b, p, i)),
        pl.BlockSpec((None, SEQ, nh * LANES), lambda b, p, i: (b, 0, p)),
        pl.BlockSpec((None, nh * C_V, SEQ), lambda b, p, i: (b, p, 0)),
    ]
    out_spec = pl.BlockSpec((None, TQ, nh * C_V), lambda b, p, i: (b, i, p))
    grid = (BATCH, C_HEADS // nh, SEQ // TQ)
    return _flash_call(kernel, grid, nh, C_V, in_specs, out_spec, (qt, k, vt), "c_attention")


def _rope_lanes(x, cos, sin_a, sin_b, shift):
    return (x * cos + pltpu.roll(x, LANES - shift, 1) * sin_a + pltpu.roll(x, shift, 1) * sin_b)


def _nt_dot(a, b):
    return lax.dot_general(a, b, (((1,), (1,)), ((), ())), preferred_element_type=F32)


def _swap_rows(x, start, half, groups):
    pieces = [x[:start]] if start else []
    for g0 in range(start, start + 2 * half * groups, 2 * half):
        pieces += [x[g0 + half:g0 + 2 * half], x[g0:g0 + half]]
    if start + 2 * half * groups < x.shape[0]:
        pieces.append(x[start + 2 * half * groups:])
    return jnp.concatenate(pieces, axis=0)


def _b_prep_kernel(h_ref, g_ref, wqt_ref, wk_ref, wvt_ref, qg_ref, kn_ref, cost_ref, sint_ref,
                   cos_ref, sa_ref, sb_ref, qt_ref, k_ref, vt_ref):
    xn = _rms(h_ref[...], g_ref[...]).astype(BF16)
    cos_t, sin_t, qgain = cost_ref[...], sint_ref[...], qg_ref[...]
    cos, sa, sb = cos_ref[...], sa_ref[...], sb_ref[...]
    half_rows = B_HEADS * LANES // 2

    def q_heads(qt_half, first_head):
        for hd in range(B_HEADS // 2):
            x = qt_half[hd * LANES:(hd + 1) * LANES]
            ms = jnp.mean(x * x, axis=0, keepdims=True)
            x = (x * lax.rsqrt(ms + NORM_EPS)) * qgain
            x = x * cos_t + _swap_rows(x, 0, B_HEAD_DIM // 4, 2) * sin_t
            out_rows = slice((first_head + hd) * LANES, (first_head + hd + 1) * LANES)
            qt_ref[out_rows, :] = x.astype(BF16)

    kk = jnp.dot(xn, wk_ref[...], preferred_element_type=F32)
    qt_a = _nt_dot(wqt_ref[:half_rows, :], xn)
    for j in range(B_KV_HEADS):
        cols = slice(j * LANES, (j + 1) * LANES)
        x = _rope_lanes(_rms(kk[:, cols], kn_ref[...]), cos, sa, sb, B_HEAD_DIM // 4)
        k_ref[:, cols] = x.astype(BF16)
    qt_b = _nt_dot(wqt_ref[half_rows:, :], xn)
    q_heads(qt_a, 0)
    vt = _nt_dot(wvt_ref[...], xn)
    q_heads(qt_b, B_HEADS // 2)
    vt_ref[...] = vt.astype(BF16)


def b_prep(h, gain, wq_t, wk, wv_t, qgain_t, knorm, tables_t, tables):
    t, d = h.shape
    nrow = SEQ // TM_ROW
    row = lambda i: (i, 0)
    fixed = lambda i: (0, 0)
    tab = lambda i: (i % nrow, 0)
    tab_t = lambda i: (0, i % nrow)
    colblk = lambda i: (i // nrow, 0, i % nrow)
    nq = B_HEADS * B_HEAD_DIM
    nkv = B_KV_HEADS * B_HEAD_DIM
    return pl.pallas_call(
        _b_prep_kernel,
        grid=(t // TM_ROW,),
        in_specs=[
            pl.BlockSpec((TM_ROW, d), row),
            pl.BlockSpec((1, d), fixed),
            pl.BlockSpec((nq, d), fixed),
            pl.BlockSpec((d, nkv), fixed),
            pl.BlockSpec((nkv, d), fixed),
            pl.BlockSpec((B_HEAD_DIM, TM_ROW), fixed),
            pl.BlockSpec((1, B_HEAD_DIM), fixed),
            pl.BlockSpec((LANES, TM_ROW), tab_t),
            pl.BlockSpec((LANES, TM_ROW), tab_t),
            pl.BlockSpec((TM_ROW, LANES), tab),
            pl.BlockSpec((TM_ROW, LANES), tab),
            pl.BlockSpec((TM_ROW, LANES), tab),
        ],
        out_specs=[
            pl.BlockSpec((None, nq, TM_ROW), colblk),
            pl.BlockSpec((TM_ROW, nkv), row),
            pl.BlockSpec((None, nkv, TM_ROW), colblk),
        ],
        out_shape=[
            jax.ShapeDtypeStruct((BATCH, nq, SEQ), BF16),
            jax.ShapeDtypeStruct((t, nkv), BF16),
            jax.ShapeDtypeStruct((BATCH, nkv, SEQ), BF16),
        ],
        compiler_params=_cparams(("parallel",)),
        name="b_prep",
    )(h, gain, wq_t, wk, wv_t, qgain_t, knorm, *tables_t, *tables)


def _c_prep_kernel(h_ref, g_ref, win_ref, qn_ref, kvn_ref, wuqt_ref, wuk_ref, wuvt_ref,
                   cost_ref, sint_ref, cos_ref, sa_ref, sb_ref, q_ref, k_ref, v_ref):
    xn = _rms(h_ref[...], g_ref[...]).astype(BF16)
    c = jnp.dot(xn, win_ref[...], preferred_element_type=F32)
    cq = _rms(c[:, :C_Q_LORA], qn_ref[...]).astype(BF16)
    ckv = _rms(c[:, C_Q_LORA:C_Q_LORA + C_KV_LORA], kvn_ref[...]).astype(BF16)
    cos, sa, sb = cos_ref[...], sa_ref[...], sb_ref[...]
    cos_t, sin_t = cost_ref[...], sint_ref[...]
    half_rows = C_HEADS * LANES // 2

    def q_heads(qt_half, first_head):
        for hd in range(C_HEADS // 2):
            x = qt_half[hd * LANES:(hd + 1) * LANES]
            x = x * cos_t + _swap_rows(x, C_NOPE, C_ROPE // 2, 1) * sin_t
            out_rows = slice((first_head + hd) * LANES, (first_head + hd + 1) * LANES)
            q_ref[out_rows, :] = x.astype(BF16)

    kn = jnp.dot(ckv, wuk_ref[...], preferred_element_type=F32)
    qt_a = _nt_dot(wuqt_ref[:half_rows, :], cq)
    kr = _rope_lanes(c[:, C_Q_LORA + C_KV_LORA:], cos, sa, sb, C_ROPE // 2)
    for hd in range(C_HEADS):
        cols = slice(hd * LANES, (hd + 1) * LANES)
        k_ref[:, cols] = (kn[:, cols] + kr).astype(BF16)
    qt_b = _nt_dot(wuqt_ref[half_rows:, :], cq)
    q_heads(qt_a, 0)
    vt = _nt_dot(wuvt_ref[...], ckv)
    q_heads(qt_b, C_HEADS // 2)
    v_ref[...] = vt.astype(BF16)


def c_prep(h, gain, win, qnorm, kvnorm, wuq_t, wuk, wuv_t, tables_t, tables):
    t, d = h.shape
    nrow = SEQ // TM_ROW
    row = lambda i: (i, 0)
    fixed = lambda i: (0, 0)
    tab = lambda i: (i % nrow, 0)
    tab_t = lambda i: (0, i % nrow)
    colblk = lambda i: (i // nrow, 0, i % nrow)
    hq = C_HEADS * LANES
    return pl.pallas_call(
        _c_prep_kernel,
        grid=(t // TM_ROW,),
        in_specs=[
            pl.BlockSpec((TM_ROW, d), row),
            pl.BlockSpec((1, d), fixed),
            pl.BlockSpec((d, C_IN_PAD), fixed),
            pl.BlockSpec((1, C_Q_LORA), fixed),
            pl.BlockSpec((1, C_KV_LORA), fixed),
            pl.BlockSpec((hq, C_Q_LORA), fixed),
            pl.BlockSpec((C_KV_LORA, hq), fixed),
            pl.BlockSpec((C_HEADS * C_V, C_KV_LORA), fixed),
            pl.BlockSpec((LANES, TM_ROW), tab_t),
            pl.BlockSpec((LANES, TM_ROW), tab_t),
            pl.BlockSpec((TM_ROW, LANES), tab),
            pl.BlockSpec((TM_ROW, LANES), tab),
            pl.BlockSpec((TM_ROW, LANES), tab),
        ],
        out_specs=[
            pl.BlockSpec((None, hq, TM_ROW), colblk),
            pl.BlockSpec((TM_ROW, hq), row),
            pl.BlockSpec((None, C_HEADS * C_V, TM_ROW), colblk),
        ],
        out_shape=[
            jax.ShapeDtypeStruct((BATCH, hq, SEQ), BF16),
            jax.ShapeDtypeStruct((t, hq), BF16),
            jax.ShapeDtypeStruct((BATCH, C_HEADS * C_V, SEQ), BF16),
        ],
        compiler_params=_cparams(("parallel",)),
        name="c_prep",
    )(h, gain, win, qnorm, kvnorm, wuq_t, wuk, wuv_t, *tables_t, *tables)


def _mix_ffn_kernel(y_ref, wo_ref, gmix_ref, h_ref, gpre_ref, wg_ref, wu_ref, wd_ref, gpost_ref,
                    o_ref, h1_ref, xn_ref, acc_ref):
    f = pl.program_id(1)

    @pl.when(f == 0)
    def _():
        z = jnp.dot(y_ref[...], wo_ref[...], preferred_element_type=F32)
        h1 = h_ref[...] + _rms(z, gmix_ref[...])
        h1_ref[...] = h1
        xn_ref[...] = _rms(h1, gpre_ref[...]).astype(BF16)

    xn = xn_ref[...]
    gate = jnp.dot(xn, wg_ref[...], preferred_element_type=F32)
    up = jnp.dot(xn, wu_ref[...], preferred_element_type=F32)
    act = (gate * jax.nn.sigmoid(gate)) * up
    part = jnp.dot(act.astype(BF16), wd_ref[...], preferred_element_type=F32)

    @pl.when(f == 0)
    def _():
        acc_ref[...] = part

    @pl.when(f > 0)
    def _():
        acc_ref[...] += part

    @pl.when(f == pl.num_programs(1) - 1)
    def _():
        o_ref[...] = h1_ref[...] + _rms(acc_ref[...], gpost_ref[...])


def mix_ffn(y, wo, gmix, h, gpre, wg, wu, wd, layer, gpost):
    t, d = h.shape
    row = lambda i, f: (i, 0)
    fixed = lambda i, f: (0, 0)
    wmode = pl.Buffered(1) if TF_FFN == D_FF else None
    return pl.pallas_call(
        _mix_ffn_kernel,
        grid=(t // TM_ROW, D_FF // TF_FFN),
        in_specs=[
            pl.BlockSpec((TM_ROW, d), row),
            pl.BlockSpec((d, d), fixed),
            pl.BlockSpec((1, d), fixed),
            pl.BlockSpec((TM_ROW, d), row),
            pl.BlockSpec((1, d), fixed),
            pl.BlockSpec((None, d, TF_FFN), lambda i, f: (layer, 0, f), pipeline_mode=wmode),
            pl.BlockSpec((None, d, TF_FFN), lambda i, f: (layer, 0, f), pipeline_mode=wmode),
            pl.BlockSpec((None, TF_FFN, d), lambda i, f: (layer, f, 0), pipeline_mode=wmode),
            pl.BlockSpec((1, d), fixed),
        ],
        out_specs=pl.BlockSpec((TM_ROW, d), row),
        out_shape=jax.ShapeDtypeStruct((t, d), F32),
        scratch_shapes=[
            pltpu.VMEM((TM_ROW, d), F32),
            pltpu.VMEM((TM_ROW, d), BF16),
            pltpu.VMEM((TM_ROW, d), F32),
        ],
        compiler_params=_cparams(("parallel", "arbitrary")),
        name="mix_ffn",
    )(y, wo, gmix, h, gpre, wg, wu, wd, gpost)


def _rope_tables(pos_by_lane, freq_idx, dim, theta, active, first_half):
    freqs = jnp.power(jnp.float32(theta), -freq_idx.astype(F32) * 2.0 / dim)
    ang = pos_by_lane * freqs[None, :]
    cos = jnp.where(active[None, :], jnp.cos(ang), 1.0)
    sin = jnp.sin(ang)
    sin_a = jnp.where((active & first_half)[None, :], -sin, 0.0)
    sin_b = jnp.where((active & ~first_half)[None, :], sin, 0.0)
    return cos.astype(F32), sin_a.astype(F32), sin_b.astype(F32)


def _b_tables():
    lane = np.arange(LANES)
    half = B_HEAD_DIM // 2
    quarter = half // 2
    t = jnp.arange(SEQ)
    row = (t // GRID_W).astype(F32)
    col = (t % GRID_W).astype(F32)
    pos = jnp.where(jnp.asarray(lane < half)[None, :], row[:, None], col[:, None])
    return _rope_tables(pos, jnp.asarray(lane % quarter), half, B_ROPE_THETA,
                        jnp.asarray(np.ones(LANES, bool)), jnp.asarray((lane % half) < quarter))


def _c_tables():
    lane = np.arange(LANES)
    half = C_ROPE // 2
    active = (lane >= C_NOPE) & (lane < C_NOPE + C_ROPE)
    pos = jnp.broadcast_to(jnp.arange(SEQ, dtype=F32)[:, None], (SEQ, LANES))
    return _rope_tables(pos, jnp.asarray((lane - C_NOPE) % half), C_ROPE, C_ROPE_THETA,
                        jnp.asarray(active), jnp.asarray((lane - C_NOPE) < half))


def _rope_tables_t(pos_by_row, freq_idx, dim, theta, active, first_half, scale):
    freqs = jnp.power(jnp.float32(theta), -freq_idx.astype(F32) * 2.0 / dim)
    ang = freqs[:, None] * pos_by_row
    cos = jnp.where(active[:, None], jnp.cos(ang), 1.0) * scale
    sin = jnp.sin(ang) * scale
    sin = jnp.where(active[:, None], jnp.where(first_half[:, None], -sin, sin), 0.0)
    return cos.astype(F32), sin.astype(F32)


def _b_tables_t():
    dim = np.arange(LANES)
    half = B_HEAD_DIM // 2
    quarter = half // 2
    t = jnp.arange(SEQ)
    row = (t // GRID_W).astype(F32)
    col = (t % GRID_W).astype(F32)
    pos = jnp.where(jnp.asarray(dim < half)[:, None], row[None, :], col[None, :])
    return _rope_tables_t(pos, jnp.asarray(dim % quarter), half, B_ROPE_THETA,
                          jnp.asarray(np.ones(LANES, bool)), jnp.asarray((dim % half) < quarter),
                          B_HEAD_DIM ** -0.5 * LOG2E)


def _c_tables_t():
    dim = np.arange(LANES)
    half = C_ROPE // 2
    active = (dim >= C_NOPE) & (dim < C_NOPE + C_ROPE)
    pos = jnp.broadcast_to(jnp.arange(SEQ, dtype=F32)[None, :], (LANES, SEQ))
    return _rope_tables_t(pos, jnp.asarray((dim - C_NOPE) % half), C_ROPE, C_ROPE_THETA,
                          jnp.asarray(active), jnp.asarray((dim - C_NOPE) < half),
                          (C_NOPE + C_ROPE) ** -0.5 * LOG2E)


def _alibi_slopes():
    n = A_NG * A_HEADS
    return jnp.asarray(2.0 ** (-8.0 * np.arange(1, n + 1) / n), dtype=F32)


def _mixer_a(h, gain, wqkv_stack, layer):
    qkv = norm_matmul(h, gain, wqkv_stack, layer)
    return a_attention(qkv.reshape(BATCH, SEQ, A_QKV), _alibi_slopes())


def _mixer_b(h, gain, wqkv, qnorm, knorm):
    nq = B_HEADS * B_HEAD_DIM
    nkv = B_KV_HEADS * B_HEAD_DIM
    w = wqkv.astype(BF16)
    qgain_t = jnp.broadcast_to(qnorm[:, None], (B_HEAD_DIM, TM_ROW))
    qt, k, vt = b_prep(h, gain, w[:, :nq].T, w[:, nq:nq + nkv], w[:, nq + nkv:].T, qgain_t,
                       knorm[None, :], _b_tables_t(), _b_tables())
    return b_attention(qt, k.reshape(BATCH, SEQ, -1), vt)


def _mixer_c(h, gain, win, qnorm, kvnorm, wuq, wukv):
    win_p = jnp.zeros((D_MODEL, C_IN_PAD), F32)
    win_p = win_p.at[:, :C_Q_LORA + C_KV_LORA].set(win[:, :C_Q_LORA + C_KV_LORA])
    kr0 = C_Q_LORA + C_KV_LORA + C_NOPE
    win_p = win_p.at[:, kr0:kr0 + C_ROPE].set(win[:, C_Q_LORA + C_KV_LORA:])
    wuq_p = jnp.pad(wuq.reshape(C_Q_LORA, C_HEADS, C_NOPE + C_ROPE),
                    ((0, 0), (0, 0), (0, LANES - C_NOPE - C_ROPE))).reshape(C_Q_LORA, C_HEADS * LANES)
    wukv3 = wukv.reshape(C_KV_LORA, C_HEADS, C_NOPE + C_V)
    wuk_p = jnp.pad(wukv3[:, :, :C_NOPE],
                    ((0, 0), (0, 0), (0, LANES - C_NOPE))).reshape(C_KV_LORA, C_HEADS * LANES)
    wuv = wukv3[:, :, C_NOPE:].reshape(C_KV_LORA, C_HEADS * C_V)
    qt, k, vt = c_prep(h, gain, win_p.astype(BF16), qnorm[None, :], kvnorm[None, :],
                       wuq_p.astype(BF16).T, wuk_p.astype(BF16), wuv.astype(BF16).T,
                       _c_tables_t(), _c_tables())
    return c_attention(qt, k.reshape(BATCH, SEQ, -1), vt)


def kernel(x, norm_mix_pre, norm_mix_post, norm_ffn_pre, norm_ffn_post, ffn_wg, ffn_wu, ffn_wd,
           a_wqkv, a_wo, b_wqkv, b_qnorm, b_knorm, b_wo,
           c_win, c_qnorm, c_kvnorm, c_wuq, c_wukv, c_wo):
    h = x.reshape(BATCH * SEQ, D_MODEL)
    a_wqkv_b = a_wqkv.astype(BF16)
    wg_b, wu_b, wd_b = ffn_wg.astype(BF16), ffn_wu.astype(BF16), ffn_wd.astype(BF16)
    for i in range(DEPTH):
        kind = i % N_MIXERS
        j = i // N_MIXERS
        gpre = norm_mix_pre[i][None, :]
        if kind == 0:
            y, wo = _mixer_a(h, gpre, a_wqkv_b, j), a_wo[j]
        elif kind == 1:
            y, wo = _mixer_b(h, gpre, b_wqkv[j], b_qnorm[j], b_knorm[j]), b_wo[j]
        else:
            y, wo = _mixer_c(h, gpre, c_win[j], c_qnorm[j], c_kvnorm[j], c_wuq[j], c_wukv[j]), c_wo[j]
        h = mix_ffn(y.reshape(BATCH * SEQ, D_MODEL), wo.astype(BF16), norm_mix_post[i][None, :], h,
                    norm_ffn_pre[i][None, :], wg_b, wu_b, wd_b, i, norm_ffn_post[i][None, :])
    return h.reshape(BATCH, SEQ, D_MODEL)
```

```python
import functools

import numpy as np
import jax
import jax.numpy as jnp
from jax import lax
from jax.experimental import pallas as pl
from jax.experimental.pallas import tpu as pltpu

F32 = jnp.float32
BF16 = jnp.bfloat16

D_MODEL = 1024
BATCH = 2
SEQ = 8192
DEPTH = 4
N_MIXERS = 3
GRID_W = 64
D_FF = 2816
NORM_EPS = 1e-6
NEG_INF = -1e30

A_GROUPS = ((128, 1), (512, 4), (2048, 16))
A_NG = 3
A_HEAD_DIM = 64
A_HEADS = 16
A_QKV = 3 * A_NG * A_HEADS * A_HEAD_DIM
A_RADIUS = 64

B_HEAD_DIM = 128
B_HEADS = 8
B_KV_HEADS = 2
B_ROPE_THETA = 10000.0
B_QKV = (B_HEADS + 2 * B_KV_HEADS) * B_HEAD_DIM

C_HEADS = 16
C_Q_LORA = 384
C_KV_LORA = 256
C_NOPE = 64
C_ROPE = 32
C_V = 64
C_ROPE_THETA = 10000.0
C_IN_PAD = 768

LANES = 128
ONES_ROWS = 16
LOG2E = 1.4426950408889634
VMEM_LIMIT = 56 * 1024 * 1024

TM_PROJ = 256
TN_PROJ = 1024
TM_ROW = 512
TF_FFN = 2816
TQ = 512
TK = 512
FLASH_UNROLL = 4
A_QB = 256
A_W = A_QB + 2 * A_RADIUS
A_UNROLL = 4


def _cparams(sem):
    return pltpu.CompilerParams(dimension_semantics=sem, vmem_limit_bytes=VMEM_LIMIT)


def _rms(x, gain):
    ms = jnp.mean(x * x, axis=-1, keepdims=True)
    return (x * lax.rsqrt(ms + NORM_EPS)) * gain


def _norm_matmul_kernel(x_ref, g_ref, w_ref, o_ref):
    xn = _rms(x_ref[...], g_ref[...]).astype(BF16)
    for c in range(o_ref.shape[1] // TN_PROJ):
        cols = slice(c * TN_PROJ, (c + 1) * TN_PROJ)
        o_ref[:, cols] = jnp.dot(xn, w_ref[:, cols], preferred_element_type=F32)


def norm_matmul(h, gain, w_stack, layer):
    t, d = h.shape
    n = w_stack.shape[2]
    return pl.pallas_call(
        _norm_matmul_kernel,
        grid=(t // TM_PROJ,),
        in_specs=[
            pl.BlockSpec((TM_PROJ, d), lambda i: (i, 0)),
            pl.BlockSpec((1, d), lambda i: (0, 0)),
            pl.BlockSpec((None, d, n), lambda i: (layer, 0, 0), pipeline_mode=pl.Buffered(1)),
        ],
        out_specs=pl.BlockSpec((TM_PROJ, n), lambda i: (i, 0)),
        out_shape=jax.ShapeDtypeStruct((t, n), F32),
        compiler_params=_cparams(("parallel",)),
        name="a_qkv_proj",
    )(h, gain, w_stack)


def _a_attn_kernel(slopes_ref, q_ref, k_ref, v_ref, o_ref, m_ref, l_ref, acc_ref,
                   s_ref, mf_ref, bias_ref):
    hp = pl.program_id(1)
    g = pl.program_id(2)
    lane = lax.broadcasted_iota(jnp.int32, (A_QB, LANES), 1)
    lo = lane < A_HEAD_DIM
    head_lanes = (lo, jnp.logical_not(lo))
    lo_w = lax.broadcasted_iota(jnp.int32, (A_W, LANES), 1) < A_HEAD_DIM
    head_lanes_w = (lo_w, jnp.logical_not(lo_w))
    rel = (lax.broadcasted_iota(jnp.int32, (A_QB, A_W), 1)
           - lax.broadcasted_iota(jnp.int32, (A_QB, A_W), 0))

    def group_body(step, gi, d):
        first = step == 0
        final = step == A_NG - 1
        cls_len = SEQ // d
        nblk = cls_len // A_QB
        nit = d * nblk

        for oi in range(3):
            dist = jnp.abs(rel - oi * A_RADIUS)
            for h2 in range(2):
                slope = slopes_ref[gi * A_HEADS + hp * 2 + h2] * (d * LOG2E)
                bias_ref[oi, h2] = jnp.where(dist <= A_RADIUS, -slope * dist.astype(F32), NEG_INF)

        def geometry(it):
            r = it // nblk
            i0 = (it % nblk) * A_QB
            if isinstance(it, int):
                kstart = min(max(i0 - A_RADIUS, 0), cls_len - A_W)
            else:
                kstart = jnp.clip(i0 - A_RADIUS, 0, cls_len - A_W)
            oi = (i0 - kstart) // A_RADIUS
            if d == 1 and isinstance(it, int):
                q_rows = pl.ds(i0, A_QB)
                k_rows = pl.ds(kstart, A_W)
            elif d == 1:
                q_rows = pl.ds(pl.multiple_of(i0, A_QB), A_QB)
                k_rows = pl.ds(pl.multiple_of(kstart, A_RADIUS), A_W)
            else:
                q_rows = pl.ds(r + i0 * d, A_QB, stride=d)
                k_rows = pl.ds(r + kstart * d, A_W, stride=d)
            return q_rows, k_rows, oi

        def load_qk(it):
            q_rows, k_rows, oi = geometry(it)
            q = (q_ref[q_rows, :] * (A_HEAD_DIM ** -0.5 * LOG2E)).astype(BF16)
            return q, k_ref[k_rows, :].astype(BF16), oi

        def scores(qk, slot, h2):
            q, k, oi = qk
            qm = jnp.where(head_lanes[h2], q, jnp.zeros_like(q))
            s = lax.dot_general(qm, k, (((1,), (1,)), ((), ())), preferred_element_type=F32)
            s = s + bias_ref[oi, h2]
            s_ref[slot, h2] = s
            mf_ref[slot, h2] = jnp.broadcast_to(jnp.max(s, axis=1, keepdims=True), (A_QB, LANES))

        def weighted_values(v, slot, h2):
            va = jnp.where(head_lanes_w[h2], v, 1.0).astype(BF16)
            mfull = mf_ref[slot, h2]
            p = jnp.exp2(s_ref[slot, h2] - jnp.concatenate([mfull] * (A_W // LANES), axis=1))
            return jnp.dot(p.astype(BF16), va, preferred_element_type=F32)

        def merge(q_rows, slot, outs):
            m_blk = jnp.where(lo, mf_ref[slot, 0], mf_ref[slot, 1])
            pv_blk = jnp.where(lo, outs[0], outs[1])
            l_blk = pltpu.roll(jnp.where(lo, outs[1], outs[0]), A_HEAD_DIM, 1)
            if first:
                m_ref[q_rows, :] = m_blk
                l_ref[q_rows, :] = l_blk
                acc_ref[q_rows, :] = pv_blk
            else:
                m_old = m_ref[q_rows, :]
                m_new = jnp.maximum(m_old, m_blk)
                a_old = jnp.exp2(m_old - m_new)
                a_blk = jnp.exp2(m_blk - m_new)
                l_new = a_old * l_ref[q_rows, :] + a_blk * l_blk
                acc_new = a_old * acc_ref[q_rows, :] + a_blk * pv_blk
                if final:
                    acc_ref[q_rows, :] = acc_new / l_new
                else:
                    m_ref[q_rows, :] = m_new
                    l_ref[q_rows, :] = l_new
                    acc_ref[q_rows, :] = acc_new

        qk0 = load_qk(0)
        for h2 in range(2):
            scores(qk0, 0, h2)

        def blocks(it0, last):
            for u in range(A_UNROLL):
                has_next = not (last and u == A_UNROLL - 1)
                qk = load_qk(it0 + u + 1) if has_next else None
                q_rows, k_rows, _ = geometry(it0 + u)
                v = v_ref[k_rows, :]
                outs = []
                for h2 in range(2):
                    if has_next:
                        scores(qk, (u + 1) % 2, h2)
                    outs.append(weighted_values(v, u % 2, h2))
                merge(q_rows, u % 2, outs)

        def body(jj, carry):
            blocks(A_UNROLL * jj, False)
            return carry

        lax.fori_loop(0, nit // A_UNROLL - 1, body, 0)
        blocks(nit - A_UNROLL, True)
        if final:
            o_ref[...] = acc_ref[...].astype(o_ref.dtype)

    for step in range(A_NG):
        gi = A_NG - 1 - step
        pl.when(g == step)(functools.partial(group_body, step, gi, A_GROUPS[gi][1]))


def a_attention(qkv, slopes):
    npairs = A_HEADS // 2

    def col(which):
        return lambda b, hp, g: (b, 0, (which * A_NG + (A_NG - 1 - g)) * npairs + hp)

    return pl.pallas_call(
        _a_attn_kernel,
        grid=(BATCH, npairs, A_NG),
        in_specs=[
            pl.BlockSpec(memory_space=pltpu.SMEM),
            pl.BlockSpec((None, SEQ, LANES), col(0)),
            pl.BlockSpec((None, SEQ, LANES), col(1)),
            pl.BlockSpec((None, SEQ, LANES), col(2)),
        ],
        out_specs=pl.BlockSpec((None, SEQ, LANES), lambda b, hp, g: (b, 0, hp)),
        out_shape=jax.ShapeDtypeStruct((BATCH, SEQ, D_MODEL), BF16),
        scratch_shapes=[
            pltpu.VMEM((SEQ, LANES), F32),
            pltpu.VMEM((SEQ, LANES), F32),
            pltpu.VMEM((SEQ, LANES), F32),
            pltpu.VMEM((2, 2, A_QB, A_W), F32),
            pltpu.VMEM((2, 2, A_QB, LANES), F32),
            pltpu.VMEM((3, 2, A_QB, A_W), F32),
        ],
        compiler_params=_cparams(("parallel", "parallel", "arbitrary")),
        name="a_attention",
    )(slopes, qkv, qkv, qkv)


def _flash_kernel(qt_ref, k_ref, vt_ref, o_ref, m_ref, acc_ref, s_ref, mc_ref, *, nh, dv, shared_kv):
    ones = jnp.ones((ONES_ROWS, TK), BF16)
    nchunk = SEQ // TK

    def chunk_rows(chunk):
        start = chunk * TK
        return pl.ds(start if isinstance(start, int) else pl.multiple_of(start, TK), TK)

    def query_block(qb, carry):
        q_lanes = pl.ds(pl.multiple_of(qb * TQ, TQ), TQ)
        m_ref[...] = jnp.full(m_ref.shape, -jnp.inf, F32)
        acc_ref[...] = jnp.zeros(acc_ref.shape, F32)

        def scores(chunk, slot, h):
            rows = chunk_rows(chunk)
            k = k_ref[rows, :] if shared_kv else k_ref[rows, h * LANES:(h + 1) * LANES]
            st = jnp.dot(k, qt_ref[h * LANES:(h + 1) * LANES, q_lanes], preferred_element_type=F32)
            s_ref[slot, h] = st
            mc_ref[slot, h] = jnp.max(st, axis=0, keepdims=True)

        def update(chunk, slot, h):
            rows = chunk_rows(chunk)
            vt = vt_ref[:, rows] if shared_kv else vt_ref[h * dv:(h + 1) * dv, rows]
            m_prev = m_ref[h]
            m_new = jnp.maximum(m_prev, mc_ref[slot, h])
            alpha = jnp.exp2(m_prev - m_new)
            pt = jnp.exp2(s_ref[slot, h] - m_new).astype(BF16)
            pv = jnp.dot(jnp.concatenate([vt, ones], axis=0), pt, preferred_element_type=F32)
            acc_ref[h] = alpha * acc_ref[h] + pv
            m_ref[h] = m_new

        for h in range(nh):
            scores(0, 0, h)

        def chunks(j, last):
            for u in range(FLASH_UNROLL):
                for h in range(nh):
                    if not (last and u == FLASH_UNROLL - 1):
                        scores(j + u + 1, (u + 1) % 2, h)
                    update(j + u, u % 2, h)

        def body(jj, c):
            chunks(FLASH_UNROLL * jj, False)
            return c

        lax.fori_loop(0, nchunk // FLASH_UNROLL - 1, body, 0)
        chunks(nchunk - FLASH_UNROLL, True)

        outs = []
        for h in range(nh):
            acc = acc_ref[h]
            outs.append(acc[:dv, :] / acc[dv:dv + 1, :])
        ot = outs[0] if nh == 1 else jnp.concatenate(outs, axis=0)
        o_ref[pl.ds(pl.multiple_of(qb * TQ, TQ), TQ), :] = ot.T.astype(o_ref.dtype)
        return carry

    lax.fori_loop(0, SEQ // TQ, query_block, 0)


def _flash_call(kernel, grid, nh, dv, in_specs, out_spec, args, name):
    return pl.pallas_call(
        kernel,
        grid=grid,
        in_specs=in_specs,
        out_specs=out_spec,
        out_shape=jax.ShapeDtypeStruct((BATCH, SEQ, D_MODEL), BF16),
        scratch_shapes=[
            pltpu.VMEM((nh, 1, TQ), F32),
            pltpu.VMEM((nh, dv + ONES_ROWS, TQ), F32),
            pltpu.VMEM((2, nh, TK, TQ), F32),
            pltpu.VMEM((2, nh, 1, TQ), F32),
        ],
        compiler_params=_cparams(("parallel",) * len(grid)),
        name=name,
    )(*args)


def b_attention(qt, k, vt):
    grp = B_HEADS // B_KV_HEADS
    nh = 2
    kernel = functools.partial(_flash_kernel, nh=nh, dv=B_HEAD_DIM, shared_kv=True)
    in_specs = [
        pl.BlockSpec((None, nh * LANES, SEQ), lambda b, kv, g: (b, kv * (grp // nh) + g, 0)),
        pl.BlockSpec((None, SEQ, LANES), lambda b, kv, g: (b, 0, kv)),
        pl.BlockSpec((None, LANES, SEQ), lambda b, kv, g: (b, kv, 0)),
    ]
    out_spec = pl.BlockSpec((None, SEQ, nh * LANES), lambda b, kv, g: (b, 0, kv * (grp // nh) + g))
    grid = (BATCH, B_KV_HEADS, grp // nh)
    return _flash_call(kernel, grid, nh, B_HEAD_DIM, in_specs, out_spec, (qt, k, vt), "b_attention")


def c_attention(qt, k, vt):
    nh = 2
    kernel = functools.partial(_flash_kernel, nh=nh, dv=C_V, shared_kv=False)
    in_specs = [
        pl.BlockSpec((None, nh * LANES, SEQ), lambda b, p: (b, p, 0)),
        pl.BlockSpec((None, SEQ, nh * LANES), lambda b, p: (b, 0, p)),
        pl.BlockSpec((None, nh * C_V, SEQ), lambda b, p: (b, p, 0)),
    ]
    out_spec = pl.BlockSpec((None, SEQ, nh * C_V), lambda b, p: (b, 0, p))
    grid = (BATCH, C_HEADS // nh)
    return _flash_call(kernel, grid, nh, C_V, in_specs, out_spec, (qt, k, vt), "c_attention")


def _rope_lanes(x, cos, sin_a, sin_b, shift):
    return (x * cos + pltpu.roll(x, LANES - shift, 1) * sin_a + pltpu.roll(x, shift, 1) * sin_b)


def _nt_dot(a, b):
    return lax.dot_general(a, b, (((1,), (1,)), ((), ())), preferred_element_type=F32)


def _swap_rows(x, start, half, groups):
    pieces = [x[:start]] if start else []
    for g0 in range(start, start + 2 * half * groups, 2 * half):
        pieces += [x[g0 + half:g0 + 2 * half], x[g0:g0 + half]]
    if start + 2 * half * groups < x.shape[0]:
        pieces.append(x[start + 2 * half * groups:])
    return jnp.concatenate(pieces, axis=0)


def _b_prep_kernel(h_ref, g_ref, wqt_ref, wk_ref, wvt_ref, qg_ref, kn_ref, cost_ref, sint_ref,
                   cos_ref, sa_ref, sb_ref, qt_ref, k_ref, vt_ref):
    xn = _rms(h_ref[...], g_ref[...]).astype(BF16)
    cos_t, sin_t, qgain = cost_ref[...], sint_ref[...], qg_ref[...]
    cos, sa, sb = cos_ref[...], sa_ref[...], sb_ref[...]
    half_rows = B_HEADS * LANES // 2

    def q_heads(qt_half, first_head):
        for hd in range(B_HEADS // 2):
            x = qt_half[hd * LANES:(hd + 1) * LANES]
            ms = jnp.mean(x * x, axis=0, keepdims=True)
            x = (x * lax.rsqrt(ms + NORM_EPS)) * qgain
            x = x * cos_t + _swap_rows(x, 0, B_HEAD_DIM // 4, 2) * sin_t
            out_rows = slice((first_head + hd) * LANES, (first_head + hd + 1) * LANES)
            qt_ref[out_rows, :] = x.astype(BF16)

    kk = jnp.dot(xn, wk_ref[...], preferred_element_type=F32)
    qt_a = _nt_dot(wqt_ref[:half_rows, :], xn)
    for j in range(B_KV_HEADS):
        cols = slice(j * LANES, (j + 1) * LANES)
        x = _rope_lanes(_rms(kk[:, cols], kn_ref[...]), cos, sa, sb, B_HEAD_DIM // 4)
        k_ref[:, cols] = x.astype(BF16)
    qt_b = _nt_dot(wqt_ref[half_rows:, :], xn)
    q_heads(qt_a, 0)
    vt = _nt_dot(wvt_ref[...], xn)
    q_heads(qt_b, B_HEADS // 2)
    vt_ref[...] = vt.astype(BF16)


def b_prep(h, gain, wq_t, wk, wv_t, qgain_t, knorm, tables_t, tables):
    t, d = h.shape
    nrow = SEQ // TM_ROW
    row = lambda i: (i, 0)
    fixed = lambda i: (0, 0)
    tab = lambda i: (i % nrow, 0)
    tab_t = lambda i: (0, i % nrow)
    colblk = lambda i: (i // nrow, 0, i % nrow)
    nq = B_HEADS * B_HEAD_DIM
    nkv = B_KV_HEADS * B_HEAD_DIM
    return pl.pallas_call(
        _b_prep_kernel,
        grid=(t // TM_ROW,),
        in_specs=[
            pl.BlockSpec((TM_ROW, d), row),
            pl.BlockSpec((1, d), fixed),
            pl.BlockSpec((nq, d), fixed),
            pl.BlockSpec((d, nkv), fixed),
            pl.BlockSpec((nkv, d), fixed),
            pl.BlockSpec((B_HEAD_DIM, TM_ROW), fixed),
            pl.BlockSpec((1, B_HEAD_DIM), fixed),
            pl.BlockSpec((LANES, TM_ROW), tab_t),
            pl.BlockSpec((LANES, TM_ROW), tab_t),
            pl.BlockSpec((TM_ROW, LANES), tab),
            pl.BlockSpec((TM_ROW, LANES), tab),
            pl.BlockSpec((TM_ROW, LANES), tab),
        ],
        out_specs=[
            pl.BlockSpec((None, nq, TM_ROW), colblk),
            pl.BlockSpec((TM_ROW, nkv), row),
            pl.BlockSpec((None, nkv, TM_ROW), colblk),
        ],
        out_shape=[
            jax.ShapeDtypeStruct((BATCH, nq, SEQ), BF16),
            jax.ShapeDtypeStruct((t, nkv), BF16),
            jax.ShapeDtypeStruct((BATCH, nkv, SEQ), BF16),
        ],
        compiler_params=_cparams(("parallel",)),
        name="b_prep",
    )(h, gain, wq_t, wk, wv_t, qgain_t, knorm, *tables_t, *tables)


def _c_prep_kernel(h_ref, g_ref, win_ref, qn_ref, kvn_ref, wuqt_ref, wuk_ref, wuvt_ref,
                   cost_ref, sint_ref, cos_ref, sa_ref, sb_ref, q_ref, k_ref, v_ref):
    xn = _rms(h_ref[...], g_ref[...]).astype(BF16)
    c = jnp.dot(xn, win_ref[...], preferred_element_type=F32)
    cq = _rms(c[:, :C_Q_LORA], qn_ref[...]).astype(BF16)
    ckv = _rms(c[:, C_Q_LORA:C_Q_LORA + C_KV_LORA], kvn_ref[...]).astype(BF16)
    cos, sa, sb = cos_ref[...], sa_ref[...], sb_ref[...]
    cos_t, sin_t = cost_ref[...], sint_ref[...]
    half_rows = C_HEADS * LANES // 2

    def q_heads(qt_half, first_head):
        for hd in range(C_HEADS // 2):
            x = qt_half[hd * LANES:(hd + 1) * LANES]
            x = x * cos_t + _swap_rows(x, C_NOPE, C_ROPE // 2, 1) * sin_t
            out_rows = slice((first_head + hd) * LANES, (first_head + hd + 1) * LANES)
            q_ref[out_rows, :] = x.astype(BF16)

    kn = jnp.dot(ckv, wuk_ref[...], preferred_element_type=F32)
    qt_a = _nt_dot(wuqt_ref[:half_rows, :], cq)
    kr = _rope_lanes(c[:, C_Q_LORA + C_KV_LORA:], cos, sa, sb, C_ROPE // 2)
    for hd in range(C_HEADS):
        cols = slice(hd * LANES, (hd + 1) * LANES)
        k_ref[:, cols] = (kn[:, cols] + kr).astype(BF16)
    qt_b = _nt_dot(wuqt_ref[half_rows:, :], cq)
    q_heads(qt_a, 0)
    vt = _nt_dot(wuvt_ref[...], ckv)
    q_heads(qt_b, C_HEADS // 2)
    v_ref[...] = vt.astype(BF16)


def c_prep(h, gain, win, qnorm, kvnorm, wuq_t, wuk, wuv_t, tables_t, tables):
    t, d = h.shape
    nrow = SEQ // TM_ROW
    row = lambda i: (i, 0)
    fixed = lambda i: (0, 0)
    tab = lambda i: (i % nrow, 0)
    tab_t = lambda i: (0, i % nrow)
    colblk = lambda i: (i // nrow, 0, i % nrow)
    hq = C_HEADS * LANES
    return pl.pallas_call(
        _c_prep_kernel,
        grid=(t // TM_ROW,),
        in_specs=[
            pl.BlockSpec((TM_ROW, d), row),
            pl.BlockSpec((1, d), fixed),
            pl.BlockSpec((d, C_IN_PAD), fixed),
            pl.BlockSpec((1, C_Q_LORA), fixed),
            pl.BlockSpec((1, C_KV_LORA), fixed),
            pl.BlockSpec((hq, C_Q_LORA), fixed),
            pl.BlockSpec((C_KV_LORA, hq), fixed),
            pl.BlockSpec((C_HEADS * C_V, C_KV_LORA), fixed),
            pl.BlockSpec((LANES, TM_ROW), tab_t),
            pl.BlockSpec((LANES, TM_ROW), tab_t),
            pl.BlockSpec((TM_ROW, LANES), tab),
            pl.BlockSpec((TM_ROW, LANES), tab),
            pl.BlockSpec((TM_ROW, LANES), tab),
        ],
        out_specs=[
            pl.BlockSpec((None, hq, TM_ROW), colblk),
            pl.BlockSpec((TM_ROW, hq), row),
            pl.BlockSpec((None, C_HEADS * C_V, TM_ROW), colblk),
        ],
        out_shape=[
            jax.ShapeDtypeStruct((BATCH, hq, SEQ), BF16),
            jax.ShapeDtypeStruct((t, hq), BF16),
            jax.ShapeDtypeStruct((BATCH, C_HEADS * C_V, SEQ), BF16),
        ],
        compiler_params=_cparams(("parallel",)),
        name="c_prep",
    )(h, gain, win, qnorm, kvnorm, wuq_t, wuk, wuv_t, *tables_t, *tables)


def _mix_ffn_kernel(y_ref, wo_ref, gmix_ref, h_ref, gpre_ref, wg_ref, wu_ref, wd_ref, gpost_ref,
                    o_ref, h1_ref, xn_ref, acc_ref):
    f = pl.program_id(1)

    @pl.when(f == 0)
    def _():
        z = jnp.dot(y_ref[...], wo_ref[...], preferred_element_type=F32)
        h1 = h_ref[...] + _rms(z, gmix_ref[...])
        h1_ref[...] = h1
        xn_ref[...] = _rms(h1, gpre_ref[...]).astype(BF16)

    xn = xn_ref[...]
    gate = jnp.dot(xn, wg_ref[...], preferred_element_type=F32)
    up = jnp.dot(xn, wu_ref[...], preferred_element_type=F32)
    act = (gate * jax.nn.sigmoid(gate)) * up
    part = jnp.dot(act.astype(BF16), wd_ref[...], preferred_element_type=F32)

    @pl.when(f == 0)
    def _():
        acc_ref[...] = part

    @pl.when(f > 0)
    def _():
        acc_ref[...] += part

    @pl.when(f == pl.num_programs(1) - 1)
    def _():
        o_ref[...] = h1_ref[...] + _rms(acc_ref[...], gpost_ref[...])


def mix_ffn(y, wo, gmix, h, gpre, wg, wu, wd, layer, gpost):
    t, d = h.shape
    row = lambda i, f: (i, 0)
    fixed = lambda i, f: (0, 0)
    wmode = pl.Buffered(1) if TF_FFN == D_FF else None
    return pl.pallas_call(
        _mix_ffn_kernel,
        grid=(t // TM_ROW, D_FF // TF_FFN),
        in_specs=[
            pl.BlockSpec((TM_ROW, d), row),
            pl.BlockSpec((d, d), fixed),
            pl.BlockSpec((1, d), fixed),
            pl.BlockSpec((TM_ROW, d), row),
            pl.BlockSpec((1, d), fixed),
            pl.BlockSpec((None, d, TF_FFN), lambda i, f: (layer, 0, f), pipeline_mode=wmode),
            pl.BlockSpec((None, d, TF_FFN), lambda i, f: (layer, 0, f), pipeline_mode=wmode),
            pl.BlockSpec((None, TF_FFN, d), lambda i, f: (layer, f, 0), pipeline_mode=wmode),
            pl.BlockSpec((1, d), fixed),
        ],
        out_specs=pl.BlockSpec((TM_ROW, d), row),
        out_shape=jax.ShapeDtypeStruct((t, d), F32),
        scratch_shapes=[
            pltpu.VMEM((TM_ROW, d), F32),
            pltpu.VMEM((TM_ROW, d), BF16),
            pltpu.VMEM((TM_ROW, d), F32),
        ],
        compiler_params=_cparams(("parallel", "arbitrary")),
        name="mix_ffn",
    )(y, wo, gmix, h, gpre, wg, wu, wd, gpost)


def _rope_tables(pos_by_lane, freq_idx, dim, theta, active, first_half):
    freqs = jnp.power(jnp.float32(theta), -freq_idx.astype(F32) * 2.0 / dim)
    ang = pos_by_lane * freqs[None, :]
    cos = jnp.where(active[None, :], jnp.cos(ang), 1.0)
    sin = jnp.sin(ang)
    sin_a = jnp.where((active & first_half)[None, :], -sin, 0.0)
    sin_b = jnp.where((active & ~first_half)[None, :], sin, 0.0)
    return cos.astype(F32), sin_a.astype(F32), sin_b.astype(F32)


def _b_tables():
    lane = np.arange(LANES)
    half = B_HEAD_DIM // 2
    quarter = half // 2
    t = jnp.arange(SEQ)
    row = (t // GRID_W).astype(F32)
    col = (t % GRID_W).astype(F32)
    pos = jnp.where(jnp.asarray(lane < half)[None, :], row[:, None], col[:, None])
    return _rope_tables(pos, jnp.asarray(lane % quarter), half, B_ROPE_THETA,
                        jnp.asarray(np.ones(LANES, bool)), jnp.asarray((lane % half) < quarter))


def _c_tables():
    lane = np.arange(LANES)
    half = C_ROPE // 2
    active = (lane >= C_NOPE) & (lane < C_NOPE + C_ROPE)
    pos = jnp.broadcast_to(jnp.arange(SEQ, dtype=F32)[:, None], (SEQ, LANES))
    return _rope_tables(pos, jnp.asarray((lane - C_NOPE) % half), C_ROPE, C_ROPE_THETA,
                        jnp.asarray(active), jnp.asarray((lane - C_NOPE) < half))


def _rope_tables_t(pos_by_row, freq_idx, dim, theta, active, first_half, scale):
    freqs = jnp.power(jnp.float32(theta), -freq_idx.astype(F32) * 2.0 / dim)
    ang = freqs[:, None] * pos_by_row
    cos = jnp.where(active[:, None], jnp.cos(ang), 1.0) * scale
    sin = jnp.sin(ang) * scale
    sin = jnp.where(active[:, None], jnp.where(first_half[:, None], -sin, sin), 0.0)
    return cos.astype(F32), sin.astype(F32)


def _b_tables_t():
    dim = np.arange(LANES)
    half = B_HEAD_DIM // 2
    quarter = half // 2
    t = jnp.arange(SEQ)
    row = (t // GRID_W).astype(F32)
    col = (t % GRID_W).astype(F32)
    pos = jnp.where(jnp.asarray(dim < half)[:, None], row[None, :], col[None, :])
    return _rope_tables_t(pos, jnp.asarray(dim % quarter), half, B_ROPE_THETA,
                          jnp.asarray(np.ones(LANES, bool)), jnp.asarray((dim % half) < quarter),
                          B_HEAD_DIM ** -0.5 * LOG2E)


def _c_tables_t():
    dim = np.arange(LANES)
    half = C_ROPE // 2
    active = (dim >= C_NOPE) & (dim < C_NOPE + C_ROPE)
    pos = jnp.broadcast_to(jnp.arange(SEQ, dtype=F32)[None, :], (LANES, SEQ))
    return _rope_tables_t(pos, jnp.asarray((dim - C_NOPE) % half), C_ROPE, C_ROPE_THETA,
                          jnp.asarray(active), jnp.asarray((dim - C_NOPE) < half),
                          (C_NOPE + C_ROPE) ** -0.5 * LOG2E)


def _alibi_slopes():
    n = A_NG * A_HEADS
    return jnp.asarray(2.0 ** (-8.0 * np.arange(1, n + 1) / n), dtype=F32)


def _mixer_a(h, gain, wqkv_stack, layer):
    qkv = norm_matmul(h, gain, wqkv_stack, layer)
    return a_attention(qkv.reshape(BATCH, SEQ, A_QKV), _alibi_slopes())


def _mixer_b(h, gain, wqkv, qnorm, knorm):
    nq = B_HEADS * B_HEAD_DIM
    nkv = B_KV_HEADS * B_HEAD_DIM
    w = wqkv.astype(BF16)
    qgain_t = jnp.broadcast_to(qnorm[:, None], (B_HEAD_DIM, TM_ROW))
    qt, k, vt = b_prep(h, gain, w[:, :nq].T, w[:, nq:nq + nkv], w[:, nq + nkv:].T, qgain_t,
                       knorm[None, :], _b_tables_t(), _b_tables())
    return b_attention(qt, k.reshape(BATCH, SEQ, -1), vt)


def _mixer_c(h, gain, win, qnorm, kvnorm, wuq, wukv):
    win_p = jnp.zeros((D_MODEL, C_IN_PAD), F32)
    win_p = win_p.at[:, :C_Q_LORA + C_KV_LORA].set(win[:, :C_Q_LORA + C_KV_LORA])
    kr0 = C_Q_LORA + C_KV_LORA + C_NOPE
    win_p = win_p.at[:, kr0:kr0 + C_ROPE].set(win[:, C_Q_LORA + C_KV_LORA:])
    wuq_p = jnp.pad(wuq.reshape(C_Q_LORA, C_HEADS, C_NOPE + C_ROPE),
                    ((0, 0), (0, 0), (0, LANES - C_NOPE - C_ROPE))).reshape(C_Q_LORA, C_HEADS * LANES)
    wukv3 = wukv.reshape(C_KV_LORA, C_HEADS, C_NOPE + C_V)
    wuk_p = jnp.pad(wukv3[:, :, :C_NOPE],
                    ((0, 0), (0, 0), (0, LANES - C_NOPE))).reshape(C_KV_LORA, C_HEADS * LANES)
    wuv = wukv3[:, :, C_NOPE:].reshape(C_KV_LORA, C_HEADS * C_V)
    qt, k, vt = c_prep(h, gain, win_p.astype(BF16), qnorm[None, :], kvnorm[None, :],
                       wuq_p.astype(BF16).T, wuk_p.astype(BF16), wuv.astype(BF16).T,
                       _c_tables_t(), _c_tables())
    return c_attention(qt, k.reshape(BATCH, SEQ, -1), vt)


def kernel(x, norm_mix_pre, norm_mix_post, norm_ffn_pre, norm_ffn_post, ffn_wg, ffn_wu, ffn_wd,
           a_wqkv, a_wo, b_wqkv, b_qnorm, b_knorm, b_wo,
           c_win, c_qnorm, c_kvnorm, c_wuq, c_wukv, c_wo):
    h = x.reshape(BATCH * SEQ, D_MODEL)
    a_wqkv_b = a_wqkv.astype(BF16)
    wg_b, wu_b, wd_b = ffn_wg.astype(BF16), ffn_wu.astype(BF16), ffn_wd.astype(BF16)
    for i in range(DEPTH):
        kind = i % N_MIXERS
        j = i // N_MIXERS
        gpre = norm_mix_pre[i][None, :]
        if kind == 0:
            y, wo = _mixer_a(h, gpre, a_wqkv_b, j), a_wo[j]
        elif kind == 1:
            y, wo = _mixer_b(h, gpre, b_wqkv[j], b_qnorm[j], b_knorm[j]), b_wo[j]
        else:
            y, wo = _mixer_c(h, gpre, c_win[j], c_qnorm[j], c_kvnorm[j], c_wuq[j], c_wukv[j]), c_wo[j]
        h = mix_ffn(y.reshape(BATCH * SEQ, D_MODEL), wo.astype(BF16), norm_mix_post[i][None, :], h,
                    norm_ffn_pre[i][None, :], wg_b, wu_b, wd_b, i, norm_ffn_post[i][None, :])
    return h.reshape(BATCH, SEQ, D_MODEL)
```

```python
import functools

import numpy as np
import jax
import jax.numpy as jnp
from jax import lax
from jax.experimental import pallas as pl
from jax.experimental.pallas import tpu as pltpu

F32 = jnp.float32
BF16 = jnp.bfloat16

D_MODEL = 1024
BATCH = 2
SEQ = 8192
DEPTH = 4
N_MIXERS = 3
GRID_W = 64
D_FF = 2816
NORM_EPS = 1e-6
NEG_INF = -1e30

A_GROUPS = ((128, 1), (512, 4), (2048, 16))
A_NG = 3
A_HEAD_DIM = 64
A_HEADS = 16
A_QKV = 3 * A_NG * A_HEADS * A_HEAD_DIM
A_RADIUS = 64

B_HEAD_DIM = 128
B_HEADS = 8
B_KV_HEADS = 2
B_ROPE_THETA = 10000.0
B_QKV = (B_HEADS + 2 * B_KV_HEADS) * B_HEAD_DIM

C_HEADS = 16
C_Q_LORA = 384
C_KV_LORA = 256
C_NOPE = 64
C_ROPE = 32
C_V = 64
C_ROPE_THETA = 10000.0
C_IN_PAD = 768

LANES = 128
ONES_ROWS = 16
LOG2E = 1.4426950408889634
VMEM_LIMIT = 56 * 1024 * 1024

TM_PROJ = 256
TN_PROJ = 1024
TM_ROW = 512
FFN_SPLIT = 2
TQ = 512
TK = 512
FLASH_UNROLL = 4
A_QB = 256
A_W = A_QB + 2 * A_RADIUS
A_UNROLL = 4


def _cparams(sem):
    return pltpu.CompilerParams(dimension_semantics=sem, vmem_limit_bytes=VMEM_LIMIT)


def _rms(x, gain):
    ms = jnp.mean(x * x, axis=-1, keepdims=True)
    return (x * lax.rsqrt(ms + NORM_EPS)) * gain


def _norm_matmul_kernel(x_ref, g_ref, w_ref, o_ref):
    xn = _rms(x_ref[...], g_ref[...]).astype(BF16)
    for c in range(o_ref.shape[1] // TN_PROJ):
        cols = slice(c * TN_PROJ, (c + 1) * TN_PROJ)
        o_ref[:, cols] = jnp.dot(xn, w_ref[:, cols], preferred_element_type=F32)


def norm_matmul(h, gain, w_stack, layer):
    t, d = h.shape
    n = w_stack.shape[2]
    return pl.pallas_call(
        _norm_matmul_kernel,
        grid=(t // TM_PROJ,),
        in_specs=[
            pl.BlockSpec((TM_PROJ, d), lambda i: (i, 0)),
            pl.BlockSpec((1, d), lambda i: (0, 0)),
            pl.BlockSpec((None, d, n), lambda i: (layer, 0, 0), pipeline_mode=pl.Buffered(1)),
        ],
        out_specs=pl.BlockSpec((TM_PROJ, n), lambda i: (i, 0)),
        out_shape=jax.ShapeDtypeStruct((t, n), F32),
        compiler_params=_cparams(("parallel",)),
        name="a_qkv_proj",
    )(h, gain, w_stack)


def _a_attn_kernel(slopes_ref, q_ref, k_ref, v_ref, o_ref, m_ref, l_ref, acc_ref,
                   s_ref, mf_ref, bias_ref):
    hp = pl.program_id(1)
    g = pl.program_id(2)
    lane = lax.broadcasted_iota(jnp.int32, (A_QB, LANES), 1)
    lo = lane < A_HEAD_DIM
    head_lanes = (lo, jnp.logical_not(lo))
    lo_w = lax.broadcasted_iota(jnp.int32, (A_W, LANES), 1) < A_HEAD_DIM
    head_lanes_w = (lo_w, jnp.logical_not(lo_w))
    rel = (lax.broadcasted_iota(jnp.int32, (A_QB, A_W), 1)
           - lax.broadcasted_iota(jnp.int32, (A_QB, A_W), 0))

    def group_body(step, gi, d):
        first = step == 0
        final = step == A_NG - 1
        cls_len = SEQ // d
        nblk = cls_len // A_QB
        nit = d * nblk

        for oi in range(3):
            dist = jnp.abs(rel - oi * A_RADIUS)
            for h2 in range(2):
                slope = slopes_ref[gi * A_HEADS + hp * 2 + h2] * (d * LOG2E)
                bias_ref[oi, h2] = jnp.where(dist <= A_RADIUS, -slope * dist.astype(F32), NEG_INF)

        def geometry(it):
            r = it // nblk
            i0 = (it % nblk) * A_QB
            if isinstance(it, int):
                kstart = min(max(i0 - A_RADIUS, 0), cls_len - A_W)
            else:
                kstart = jnp.clip(i0 - A_RADIUS, 0, cls_len - A_W)
            oi = (i0 - kstart) // A_RADIUS
            if d == 1 and isinstance(it, int):
                q_rows = pl.ds(i0, A_QB)
                k_rows = pl.ds(kstart, A_W)
            elif d == 1:
                q_rows = pl.ds(pl.multiple_of(i0, A_QB), A_QB)
                k_rows = pl.ds(pl.multiple_of(kstart, A_RADIUS), A_W)
            else:
                q_rows = pl.ds(r + i0 * d, A_QB, stride=d)
                k_rows = pl.ds(r + kstart * d, A_W, stride=d)
            return q_rows, k_rows, oi

        def load_qk(it):
            q_rows, k_rows, oi = geometry(it)
            q = (q_ref[q_rows, :] * (A_HEAD_DIM ** -0.5 * LOG2E)).astype(BF16)
            return q, k_ref[k_rows, :].astype(BF16), oi

        def scores(qk, slot, h2):
            q, k, oi = qk
            qm = jnp.where(head_lanes[h2], q, jnp.zeros_like(q))
            s = lax.dot_general(qm, k, (((1,), (1,)), ((), ())), preferred_element_type=F32)
            s = s + bias_ref[oi, h2]
            s_ref[slot, h2] = s
            mf_ref[slot, h2] = jnp.broadcast_to(jnp.max(s, axis=1, keepdims=True), (A_QB, LANES))

        def weighted_values(v, slot, h2):
            va = jnp.where(head_lanes_w[h2], v, 1.0).astype(BF16)
            mfull = mf_ref[slot, h2]
            p = jnp.exp2(s_ref[slot, h2] - jnp.concatenate([mfull] * (A_W // LANES), axis=1))
            return jnp.dot(p.astype(BF16), va, preferred_element_type=F32)

        def merge(q_rows, slot, outs):
            m_blk = jnp.where(lo, mf_ref[slot, 0], mf_ref[slot, 1])
            pv_blk = jnp.where(lo, outs[0], outs[1])
            l_blk = pltpu.roll(jnp.where(lo, outs[1], outs[0]), A_HEAD_DIM, 1)
            if first:
                m_ref[q_rows, :] = m_blk
                l_ref[q_rows, :] = l_blk
                acc_ref[q_rows, :] = pv_blk
            else:
                m_old = m_ref[q_rows, :]
                m_new = jnp.maximum(m_old, m_blk)
                a_old = jnp.exp2(m_old - m_new)
                a_blk = jnp.exp2(m_blk - m_new)
                l_new = a_old * l_ref[q_rows, :] + a_blk * l_blk
                acc_new = a_old * acc_ref[q_rows, :] + a_blk * pv_blk
                if final:
                    acc_ref[q_rows, :] = acc_new / l_new
                else:
                    m_ref[q_rows, :] = m_new
                    l_ref[q_rows, :] = l_new
                    acc_ref[q_rows, :] = acc_new

        qk0 = load_qk(0)
        for h2 in range(2):
            scores(qk0, 0, h2)

        def blocks(it0, last):
            for u in range(A_UNROLL):
                has_next = not (last and u == A_UNROLL - 1)
                qk = load_qk(it0 + u + 1) if has_next else None
                q_rows, k_rows, _ = geometry(it0 + u)
                v = v_ref[k_rows, :]
                outs = []
                for h2 in range(2):
                    if has_next:
                        scores(qk, (u + 1) % 2, h2)
                    outs.append(weighted_values(v, u % 2, h2))
                merge(q_rows, u % 2, outs)

        def body(jj, carry):
            blocks(A_UNROLL * jj, False)
            return carry

        lax.fori_loop(0, nit // A_UNROLL - 1, body, 0)
        blocks(nit - A_UNROLL, True)
        if final:
            o_ref[...] = acc_ref[...].astype(o_ref.dtype)

    for step in range(A_NG):
        gi = A_NG - 1 - step
        pl.when(g == step)(functools.partial(group_body, step, gi, A_GROUPS[gi][1]))


def a_attention(qkv, slopes):
    npairs = A_HEADS // 2

    def col(which):
        return lambda b, hp, g: (b, 0, (which * A_NG + (A_NG - 1 - g)) * npairs + hp)

    return pl.pallas_call(
        _a_attn_kernel,
        grid=(BATCH, npairs, A_NG),
        in_specs=[
            pl.BlockSpec(memory_space=pltpu.SMEM),
            pl.BlockSpec((None, SEQ, LANES), col(0)),
            pl.BlockSpec((None, SEQ, LANES), col(1)),
            pl.BlockSpec((None, SEQ, LANES), col(2)),
        ],
        out_specs=pl.BlockSpec((None, SEQ, LANES), lambda b, hp, g: (b, 0, hp)),
        out_shape=jax.ShapeDtypeStruct((BATCH, SEQ, D_MODEL), BF16),
        scratch_shapes=[
            pltpu.VMEM((SEQ, LANES), F32),
            pltpu.VMEM((SEQ, LANES), F32),
            pltpu.VMEM((SEQ, LANES), F32),
            pltpu.VMEM((2, 2, A_QB, A_W), F32),
            pltpu.VMEM((2, 2, A_QB, LANES), F32),
            pltpu.VMEM((3, 2, A_QB, A_W), F32),
        ],
        compiler_params=_cparams(("parallel", "parallel", "arbitrary")),
        name="a_attention",
    )(slopes, qkv, qkv, qkv)


def _flash_kernel(qt_ref, k_ref, vt_ref, o_ref, m_ref, acc_ref, s_ref, mc_ref, *, nh, dv, shared_kv):
    m_ref[...] = jnp.full(m_ref.shape, -jnp.inf, F32)
    acc_ref[...] = jnp.zeros(acc_ref.shape, F32)
    ones = jnp.ones((ONES_ROWS, TK), BF16)
    nchunk = SEQ // TK

    def chunk_rows(chunk):
        start = chunk * TK
        return pl.ds(start if isinstance(start, int) else pl.multiple_of(start, TK), TK)

    def scores(chunk, slot, h):
        rows = chunk_rows(chunk)
        k = k_ref[rows, :] if shared_kv else k_ref[rows, h * LANES:(h + 1) * LANES]
        st = jnp.dot(k, qt_ref[h * LANES:(h + 1) * LANES, :], preferred_element_type=F32)
        s_ref[slot, h] = st
        mc_ref[slot, h] = jnp.max(st, axis=0, keepdims=True)

    def update(chunk, slot, h):
        rows = chunk_rows(chunk)
        vt = vt_ref[:, rows] if shared_kv else vt_ref[h * dv:(h + 1) * dv, rows]
        m_prev = m_ref[h]
        m_new = jnp.maximum(m_prev, mc_ref[slot, h])
        alpha = jnp.exp2(m_prev - m_new)
        pt = jnp.exp2(s_ref[slot, h] - m_new).astype(BF16)
        pv = jnp.dot(jnp.concatenate([vt, ones], axis=0), pt, preferred_element_type=F32)
        acc_ref[h] = alpha * acc_ref[h] + pv
        m_ref[h] = m_new

    for h in range(nh):
        scores(0, 0, h)

    def chunks(j, last):
        for u in range(FLASH_UNROLL):
            for h in range(nh):
                if not (last and u == FLASH_UNROLL - 1):
                    scores(j + u + 1, (u + 1) % 2, h)
                update(j + u, u % 2, h)

    def body(jj, carry):
        chunks(FLASH_UNROLL * jj, False)
        return carry

    lax.fori_loop(0, nchunk // FLASH_UNROLL - 1, body, 0)
    chunks(nchunk - FLASH_UNROLL, True)

    outs = []
    for h in range(nh):
        acc = acc_ref[h]
        outs.append(acc[:dv, :] / acc[dv:dv + 1, :])
    ot = outs[0] if nh == 1 else jnp.concatenate(outs, axis=0)
    o_ref[...] = ot.T.astype(o_ref.dtype)


def _flash_call(kernel, grid, nh, dv, in_specs, out_spec, args, name):
    return pl.pallas_call(
        kernel,
        grid=grid,
        in_specs=in_specs,
        out_specs=out_spec,
        out_shape=jax.ShapeDtypeStruct((BATCH, SEQ, D_MODEL), BF16),
        scratch_shapes=[
            pltpu.VMEM((nh, 1, TQ), F32),
            pltpu.VMEM((nh, dv + ONES_ROWS, TQ), F32),
            pltpu.VMEM((2, nh, TK, TQ), F32),
            pltpu.VMEM((2, nh, 1, TQ), F32),
        ],
        compiler_params=_cparams(("parallel",) * (len(grid) - 1) + ("arbitrary",)),
        name=name,
    )(*args)


def b_attention(qt, k, vt):
    grp = B_HEADS // B_KV_HEADS
    nh = 2
    kernel = functools.partial(_flash_kernel, nh=nh, dv=B_HEAD_DIM, shared_kv=True)
    in_specs = [
        pl.BlockSpec((None, nh * LANES, TQ), lambda b, kv, g, i: (b, kv * (grp // nh) + g, i)),
        pl.BlockSpec((None, SEQ, LANES), lambda b, kv, g, i: (b, 0, kv)),
        pl.BlockSpec((None, LANES, SEQ), lambda b, kv, g, i: (b, kv, 0)),
    ]
    out_spec = pl.BlockSpec((None, TQ, nh * LANES), lambda b, kv, g, i: (b, i, kv * (grp // nh) + g))
    grid = (BATCH, B_KV_HEADS, grp // nh, SEQ // TQ)
    return _flash_call(kernel, grid, nh, B_HEAD_DIM, in_specs, out_spec, (qt, k, vt), "b_attention")


def c_attention(qt, k, vt):
    nh = 2
    kernel = functools.partial(_flash_kernel, nh=nh, dv=C_V, shared_kv=False)
    in_specs = [
        pl.BlockSpec((None, nh * LANES, TQ), lambda b, p, i: (b, p, i)),
        pl.BlockSpec((None, SEQ, nh * LANES), lambda b, p, i: (b, 0, p)),
        pl.BlockSpec((None, nh * C_V, SEQ), lambda b, p, i: (b, p, 0)),
    ]
    out_spec = pl.BlockSpec((None, TQ, nh * C_V), lambda b, p, i: (b, i, p))
    grid = (BATCH, C_HEADS // nh, SEQ // TQ)
    return _flash_call(kernel, grid, nh, C_V, in_specs, out_spec, (qt, k, vt), "c_attention")


def _rope_lanes(x, cos, sin_a, sin_b, shift):
    return (x * cos + pltpu.roll(x, LANES - shift, 1) * sin_a + pltpu.roll(x, shift, 1) * sin_b)


def _nt_dot(a, b):
    return lax.dot_general(a, b, (((1,), (1,)), ((), ())), preferred_element_type=F32)


def _swap_rows(x, start, half, groups):
    pieces = [x[:start]] if start else []
    for g0 in range(start, start + 2 * half * groups, 2 * half):
        pieces += [x[g0 + half:g0 + 2 * half], x[g0:g0 + half]]
    if start + 2 * half * groups < x.shape[0]:
        pieces.append(x[start + 2 * half * groups:])
    return jnp.concatenate(pieces, axis=0)


def _b_prep_kernel(h_ref, g_ref, wqt_ref, wk_ref, wvt_ref, qg_ref, kn_ref, cost_ref, sint_ref,
                   cos_ref, sa_ref, sb_ref, qt_ref, k_ref, vt_ref):
    xn = _rms(h_ref[...], g_ref[...]).astype(BF16)
    cos_t, sin_t, qgain = cost_ref[...], sint_ref[...], qg_ref[...]
    cos, sa, sb = cos_ref[...], sa_ref[...], sb_ref[...]
    half_rows = B_HEADS * LANES // 2

    def q_heads(qt_half, first_head):
        for hd in range(B_HEADS // 2):
            x = qt_half[hd * LANES:(hd + 1) * LANES]
            ms = jnp.mean(x * x, axis=0, keepdims=True)
            x = (x * lax.rsqrt(ms + NORM_EPS)) * qgain
            x = x * cos_t + _swap_rows(x, 0, B_HEAD_DIM // 4, 2) * sin_t
            out_rows = slice((first_head + hd) * LANES, (first_head + hd + 1) * LANES)
            qt_ref[out_rows, :] = x.astype(BF16)

    kk = jnp.dot(xn, wk_ref[...], preferred_element_type=F32)
    qt_a = _nt_dot(wqt_ref[:half_rows, :], xn)
    for j in range(B_KV_HEADS):
        cols = slice(j * LANES, (j + 1) * LANES)
        x = _rope_lanes(_rms(kk[:, cols], kn_ref[...]), cos, sa, sb, B_HEAD_DIM // 4)
        k_ref[:, cols] = x.astype(BF16)
    qt_b = _nt_dot(wqt_ref[half_rows:, :], xn)
    q_heads(qt_a, 0)
    vt = _nt_dot(wvt_ref[...], xn)
    q_heads(qt_b, B_HEADS // 2)
    vt_ref[...] = vt.astype(BF16)


def b_prep(h, gain, wq_t, wk, wv_t, qgain_t, knorm, tables_t, tables):
    t, d = h.shape
    nrow = SEQ // TM_ROW
    row = lambda i: (i, 0)
    fixed = lambda i: (0, 0)
    tab = lambda i: (i % nrow, 0)
    tab_t = lambda i: (0, i % nrow)
    colblk = lambda i: (i // nrow, 0, i % nrow)
    nq = B_HEADS * B_HEAD_DIM
    nkv = B_KV_HEADS * B_HEAD_DIM
    return pl.pallas_call(
        _b_prep_kernel,
        grid=(t // TM_ROW,),
        in_specs=[
            pl.BlockSpec((TM_ROW, d), row),
            pl.BlockSpec((1, d), fixed),
            pl.BlockSpec((nq, d), fixed),
            pl.BlockSpec((d, nkv), fixed),
            pl.BlockSpec((nkv, d), fixed),
            pl.BlockSpec((B_HEAD_DIM, TM_ROW), fixed),
            pl.BlockSpec((1, B_HEAD_DIM), fixed),
            pl.BlockSpec((LANES, TM_ROW), tab_t),
            pl.BlockSpec((LANES, TM_ROW), tab_t),
            pl.BlockSpec((TM_ROW, LANES), tab),
            pl.BlockSpec((TM_ROW, LANES), tab),
            pl.BlockSpec((TM_ROW, LANES), tab),
        ],
        out_specs=[
            pl.BlockSpec((None, nq, TM_ROW), colblk),
            pl.BlockSpec((TM_ROW, nkv), row),
            pl.BlockSpec((None, nkv, TM_ROW), colblk),
        ],
        out_shape=[
            jax.ShapeDtypeStruct((BATCH, nq, SEQ), BF16),
            jax.ShapeDtypeStruct((t, nkv), BF16),
            jax.ShapeDtypeStruct((BATCH, nkv, SEQ), BF16),
        ],
        compiler_params=_cparams(("parallel",)),
        name="b_prep",
    )(h, gain, wq_t, wk, wv_t, qgain_t, knorm, *tables_t, *tables)


def _c_prep_kernel(h_ref, g_ref, win_ref, qn_ref, kvn_ref, wuqt_ref, wuk_ref, wuvt_ref,
                   cost_ref, sint_ref, cos_ref, sa_ref, sb_ref, q_ref, k_ref, v_ref):
    xn = _rms(h_ref[...], g_ref[...]).astype(BF16)
    c = jnp.dot(xn, win_ref[...], preferred_element_type=F32)
    cq = _rms(c[:, :C_Q_LORA], qn_ref[...]).astype(BF16)
    ckv = _rms(c[:, C_Q_LORA:C_Q_LORA + C_KV_LORA], kvn_ref[...]).astype(BF16)
    cos, sa, sb = cos_ref[...], sa_ref[...], sb_ref[...]
    cos_t, sin_t = cost_ref[...], sint_ref[...]
    half_rows = C_HEADS * LANES // 2

    def q_heads(qt_half, first_head):
        for hd in range(C_HEADS // 2):
            x = qt_half[hd * LANES:(hd + 1) * LANES]
            x = x * cos_t + _swap_rows(x, C_NOPE, C_ROPE // 2, 1) * sin_t
            out_rows = slice((first_head + hd) * LANES, (first_head + hd + 1) * LANES)
            q_ref[out_rows, :] = x.astype(BF16)

    kn = jnp.dot(ckv, wuk_ref[...], preferred_element_type=F32)
    qt_a = _nt_dot(wuqt_ref[:half_rows, :], cq)
    kr = _rope_lanes(c[:, C_Q_LORA + C_KV_LORA:], cos, sa, sb, C_ROPE // 2)
    for hd in range(C_HEADS):
        cols = slice(hd * LANES, (hd + 1) * LANES)
        k_ref[:, cols] = (kn[:, cols] + kr).astype(BF16)
    qt_b = _nt_dot(wuqt_ref[half_rows:, :], cq)
    q_heads(qt_a, 0)
    vt = _nt_dot(wuvt_ref[...], ckv)
    q_heads(qt_b, C_HEADS // 2)
    v_ref[...] = vt.astype(BF16)


def c_prep(h, gain, win, qnorm, kvnorm, wuq_t, wuk, wuv_t, tables_t, tables):
    t, d = h.shape
    nrow = SEQ // TM_ROW
    row = lambda i: (i, 0)
    fixed = lambda i: (0, 0)
    tab = lambda i: (i % nrow, 0)
    tab_t = lambda i: (0, i % nrow)
    colblk = lambda i: (i // nrow, 0, i % nrow)
    hq = C_HEADS * LANES
    return pl.pallas_call(
        _c_prep_kernel,
        grid=(t // TM_ROW,),
        in_specs=[
            pl.BlockSpec((TM_ROW, d), row),
            pl.BlockSpec((1, d), fixed),
            pl.BlockSpec((d, C_IN_PAD), fixed),
            pl.BlockSpec((1, C_Q_LORA), fixed),
            pl.BlockSpec((1, C_KV_LORA), fixed),
            pl.BlockSpec((hq, C_Q_LORA), fixed),
            pl.BlockSpec((C_KV_LORA, hq), fixed),
            pl.BlockSpec((C_HEADS * C_V, C_KV_LORA), fixed),
            pl.BlockSpec((LANES, TM_ROW), tab_t),
            pl.BlockSpec((LANES, TM_ROW), tab_t),
            pl.BlockSpec((TM_ROW, LANES), tab),
            pl.BlockSpec((TM_ROW, LANES), tab),
            pl.BlockSpec((TM_ROW, LANES), tab),
        ],
        out_specs=[
            pl.BlockSpec((None, hq, TM_ROW), colblk),
            pl.BlockSpec((TM_ROW, hq), row),
            pl.BlockSpec((None, C_HEADS * C_V, TM_ROW), colblk),
        ],
        out_shape=[
            jax.ShapeDtypeStruct((BATCH, hq, SEQ), BF16),
            jax.ShapeDtypeStruct((t, hq), BF16),
            jax.ShapeDtypeStruct((BATCH, C_HEADS * C_V, SEQ), BF16),
        ],
        compiler_params=_cparams(("parallel",)),
        name="c_prep",
    )(h, gain, win, qnorm, kvnorm, wuq_t, wuk, wuv_t, *tables_t, *tables)


def _mix_ffn_kernel(y_ref, wo_ref, gmix_ref, h_ref, gpre_ref, wg_ref, wu_ref, wd_ref, gpost_ref,
                    o_ref):
    sub = TM_ROW // FFN_SPLIT
    parts = [slice(i * sub, (i + 1) * sub) for i in range(FFN_SPLIT)]
    z = [jnp.dot(y_ref[p, :], wo_ref[...], preferred_element_type=F32) for p in parts]
    h1, gate, up, down = [], [], [], []
    for i, p in enumerate(parts):
        h1.append(h_ref[p, :] + _rms(z[i], gmix_ref[...]))
        xn = _rms(h1[i], gpre_ref[...]).astype(BF16)
        gate.append(jnp.dot(xn, wg_ref[...], preferred_element_type=F32))
        up.append(jnp.dot(xn, wu_ref[...], preferred_element_type=F32))
    for i in range(FFN_SPLIT):
        act = (gate[i] * jax.nn.sigmoid(gate[i])) * up[i]
        down.append(jnp.dot(act.astype(BF16), wd_ref[...], preferred_element_type=F32))
    for i, p in enumerate(parts):
        o_ref[p, :] = h1[i] + _rms(down[i], gpost_ref[...])


def mix_ffn(y, wo, gmix, h, gpre, wg, wu, wd, layer, gpost):
    t, d = h.shape
    row = lambda i: (i, 0)
    fixed = lambda i: (0, 0)
    layer_blk = lambda i: (layer, 0, 0)
    once = pl.Buffered(1)
    return pl.pallas_call(
        _mix_ffn_kernel,
        grid=(t // TM_ROW,),
        in_specs=[
            pl.BlockSpec((TM_ROW, d), row),
            pl.BlockSpec((d, d), fixed, pipeline_mode=once),
            pl.BlockSpec((1, d), fixed),
            pl.BlockSpec((TM_ROW, d), row),
            pl.BlockSpec((1, d), fixed),
            pl.BlockSpec((None, d, D_FF), layer_blk, pipeline_mode=once),
            pl.BlockSpec((None, d, D_FF), layer_blk, pipeline_mode=once),
            pl.BlockSpec((None, D_FF, d), layer_blk, pipeline_mode=once),
            pl.BlockSpec((1, d), fixed),
        ],
        out_specs=pl.BlockSpec((TM_ROW, d), row),
        out_shape=jax.ShapeDtypeStruct((t, d), F32),
        compiler_params=_cparams(("parallel",)),
        name="mix_ffn",
    )(y, wo, gmix, h, gpre, wg, wu, wd, gpost)


def _angles(pos, n_freq, dim, theta):
    freqs = jnp.power(jnp.float32(theta), -jnp.arange(n_freq, dtype=F32) * 2.0 / dim)
    ang = freqs[:, None] * pos[None, :]
    return lax.optimization_barrier((jnp.cos(ang), jnp.sin(ang)))


def _b_tables():
    quarter = B_HEAD_DIM // 4
    rows = SEQ // GRID_W
    rc, rs = _angles(jnp.arange(rows, dtype=F32), quarter, B_HEAD_DIM // 2, B_ROPE_THETA)
    cc, cs = _angles(jnp.arange(GRID_W, dtype=F32), quarter, B_HEAD_DIM // 2, B_ROPE_THETA)
    by_row = lambda x: jnp.repeat(x, GRID_W, axis=1)
    by_col = lambda x: jnp.tile(x, (1, rows))
    rc, rs, cc, cs = by_row(rc), by_row(rs), by_col(cc), by_col(cs)
    zero = jnp.zeros_like(rs)
    cos_t = jnp.concatenate([rc, rc, cc, cc], axis=0)
    sin_t = jnp.concatenate([-rs, rs, -cs, cs], axis=0)
    sin_a = jnp.concatenate([-rs, zero, -cs, zero], axis=0)
    sin_b = jnp.concatenate([zero, rs, zero, cs], axis=0)
    scale = B_HEAD_DIM ** -0.5 * LOG2E
    return (cos_t * scale, sin_t * scale), (cos_t.T, sin_a.T, sin_b.T)


def _c_tables():
    half = C_ROPE // 2
    c, s = _angles(jnp.arange(SEQ, dtype=F32), half, C_ROPE, C_ROPE_THETA)
    one_lo = jnp.ones((C_NOPE, SEQ), F32)
    one_hi = jnp.ones((LANES - C_NOPE - C_ROPE, SEQ), F32)
    zero = jnp.zeros_like(s)
    cos_t = jnp.concatenate([one_lo, c, c, one_hi], axis=0)
    sin_t = jnp.concatenate([0 * one_lo, -s, s, 0 * one_hi], axis=0)
    sin_a = jnp.concatenate([0 * one_lo, -s, zero, 0 * one_hi], axis=0)
    sin_b = jnp.concatenate([0 * one_lo, zero, s, 0 * one_hi], axis=0)
    scale = (C_NOPE + C_ROPE) ** -0.5 * LOG2E
    return (cos_t * scale, sin_t * scale), (cos_t.T, sin_a.T, sin_b.T)


def _alibi_slopes():
    n = A_NG * A_HEADS
    return jnp.asarray(2.0 ** (-8.0 * np.arange(1, n + 1) / n), dtype=F32)


def _mixer_a(h, gain, wqkv_stack, layer):
    qkv = norm_matmul(h, gain, wqkv_stack, layer)
    return a_attention(qkv.reshape(BATCH, SEQ, A_QKV), _alibi_slopes())


def _mixer_b(h, gain, wqkv, qnorm, knorm):
    nq = B_HEADS * B_HEAD_DIM
    nkv = B_KV_HEADS * B_HEAD_DIM
    w = wqkv.astype(BF16)
    qgain_t = jnp.broadcast_to(qnorm[:, None], (B_HEAD_DIM, TM_ROW))
    tables_t, tables = _b_tables()
    qt, k, vt = b_prep(h, gain, w[:, :nq].T, w[:, nq:nq + nkv], w[:, nq + nkv:].T, qgain_t,
                       knorm[None, :], tables_t, tables)
    return b_attention(qt, k.reshape(BATCH, SEQ, -1), vt)


def _mixer_c(h, gain, win, qnorm, kvnorm, wuq, wukv):
    win_p = jnp.zeros((D_MODEL, C_IN_PAD), F32)
    win_p = win_p.at[:, :C_Q_LORA + C_KV_LORA].set(win[:, :C_Q_LORA + C_KV_LORA])
    kr0 = C_Q_LORA + C_KV_LORA + C_NOPE
    win_p = win_p.at[:, kr0:kr0 + C_ROPE].set(win[:, C_Q_LORA + C_KV_LORA:])
    wuq_p = jnp.pad(wuq.reshape(C_Q_LORA, C_HEADS, C_NOPE + C_ROPE),
                    ((0, 0), (0, 0), (0, LANES - C_NOPE - C_ROPE))).reshape(C_Q_LORA, C_HEADS * LANES)
    wukv3 = wukv.reshape(C_KV_LORA, C_HEADS, C_NOPE + C_V)
    wuk_p = jnp.pad(wukv3[:, :, :C_NOPE],
                    ((0, 0), (0, 0), (0, LANES - C_NOPE))).reshape(C_KV_LORA, C_HEADS * LANES)
    wuv = wukv3[:, :, C_NOPE:].reshape(C_KV_LORA, C_HEADS * C_V)
    tables_t, tables = _c_tables()
    qt, k, vt = c_prep(h, gain, win_p.astype(BF16), qnorm[None, :], kvnorm[None, :],
                       wuq_p.astype(BF16).T, wuk_p.astype(BF16), wuv.astype(BF16).T,
                       tables_t, tables)
    return c_attention(qt, k.reshape(BATCH, SEQ, -1), vt)


def kernel(x, norm_mix_pre, norm_mix_post, norm_ffn_pre, norm_ffn_post, ffn_wg, ffn_wu, ffn_wd,
           a_wqkv, a_wo, b_wqkv, b_qnorm, b_knorm, b_wo,
           c_win, c_qnorm, c_kvnorm, c_wuq, c_wukv, c_wo):
    h = x.reshape(BATCH * SEQ, D_MODEL)
    a_wqkv_b = a_wqkv.astype(BF16)
    wg_b, wu_b, wd_b = ffn_wg.astype(BF16), ffn_wu.astype(BF16), ffn_wd.astype(BF16)
    for i in range(DEPTH):
        kind = i % N_MIXERS
        j = i // N_MIXERS
        gpre = norm_mix_pre[i][None, :]
        if kind == 0:
            y, wo = _mixer_a(h, gpre, a_wqkv_b, j), a_wo[j]
        elif kind == 1:
            y, wo = _mixer_b(h, gpre, b_wqkv[j], b_qnorm[j], b_knorm[j]), b_wo[j]
        else:
            y, wo = _mixer_c(h, gpre, c_win[j], c_qnorm[j], c_kvnorm[j], c_wuq[j], c_wukv[j]), c_wo[j]
        h = mix_ffn(y.reshape(BATCH * SEQ, D_MODEL), wo.astype(BF16), norm_mix_post[i][None, :], h,
                    norm_ffn_pre[i][None, :], wg_b, wu_b, wd_b, i, norm_ffn_post[i][None, :])
    return h.reshape(BATCH, SEQ, D_MODEL)
```

```python
import functools

import numpy as np
import jax
import jax.numpy as jnp
from jax import lax
from jax.experimental import pallas as pl
from jax.experimental.pallas import tpu as pltpu

F32 = jnp.float32
BF16 = jnp.bfloat16

D_MODEL = 1024
BATCH = 2
SEQ = 8192
DEPTH = 4
N_MIXERS = 3
GRID_W = 64
D_FF = 2816
NORM_EPS = 1e-6
NEG_INF = -1e30

A_GROUPS = ((128, 1), (512, 4), (2048, 16))
A_NG = 3
A_HEAD_DIM = 64
A_HEADS = 16
A_QKV = 3 * A_NG * A_HEADS * A_HEAD_DIM
A_RADIUS = 64

B_HEAD_DIM = 128
B_HEADS = 8
B_KV_HEADS = 2
B_ROPE_THETA = 10000.0
B_QKV = (B_HEADS + 2 * B_KV_HEADS) * B_HEAD_DIM

C_HEADS = 16
C_Q_LORA = 384
C_KV_LORA = 256
C_NOPE = 64
C_ROPE = 32
C_V = 64
C_ROPE_THETA = 10000.0
C_IN_PAD = 768

LANES = 128
ONES_ROWS = 16
LOG2E = 1.4426950408889634
VMEM_LIMIT = 56 * 1024 * 1024

TM_PROJ = 256
TN_PROJ = 1024
TM_ROW = 512
FFN_SPLIT = 2
TQ = 512
TK = 512
FLASH_UNROLL = 4
FLASH_HEADS = 4
A_QB = 256
A_W = A_QB + 2 * A_RADIUS
A_UNROLL = 4


def _cparams(sem):
    return pltpu.CompilerParams(dimension_semantics=sem, vmem_limit_bytes=VMEM_LIMIT)


def _rms(x, gain):
    ms = jnp.mean(x * x, axis=-1, keepdims=True)
    return (x * lax.rsqrt(ms + NORM_EPS)) * gain


def _norm_matmul_kernel(x_ref, g_ref, w_ref, o_ref):
    xn = _rms(x_ref[...], g_ref[...]).astype(BF16)
    for c in range(o_ref.shape[1] // TN_PROJ):
        cols = slice(c * TN_PROJ, (c + 1) * TN_PROJ)
        o_ref[:, cols] = jnp.dot(xn, w_ref[:, cols], preferred_element_type=F32)


def norm_matmul(h, gain, w_stack, layer):
    t, d = h.shape
    n = w_stack.shape[2]
    return pl.pallas_call(
        _norm_matmul_kernel,
        grid=(t // TM_PROJ,),
        in_specs=[
            pl.BlockSpec((TM_PROJ, d), lambda i: (i, 0)),
            pl.BlockSpec((1, d), lambda i: (0, 0)),
            pl.BlockSpec((None, d, n), lambda i: (layer, 0, 0), pipeline_mode=pl.Buffered(1)),
        ],
        out_specs=pl.BlockSpec((TM_PROJ, n), lambda i: (i, 0)),
        out_shape=jax.ShapeDtypeStruct((t, n), F32),
        compiler_params=_cparams(("parallel",)),
        name="a_qkv_proj",
    )(h, gain, w_stack)


def _a_attn_kernel(slopes_ref, q_ref, k_ref, v_ref, o_ref, m_ref, l_ref, acc_ref,
                   s_ref, mf_ref, bias_ref):
    hp = pl.program_id(1)
    g = pl.program_id(2)
    lane = lax.broadcasted_iota(jnp.int32, (A_QB, LANES), 1)
    lo = lane < A_HEAD_DIM
    head_lanes = (lo, jnp.logical_not(lo))
    lo_w = lax.broadcasted_iota(jnp.int32, (A_W, LANES), 1) < A_HEAD_DIM
    head_lanes_w = (lo_w, jnp.logical_not(lo_w))
    rel = (lax.broadcasted_iota(jnp.int32, (A_QB, A_W), 1)
           - lax.broadcasted_iota(jnp.int32, (A_QB, A_W), 0))

    def group_body(step, gi, d):
        first = step == 0
        final = step == A_NG - 1
        cls_len = SEQ // d
        nblk = cls_len // A_QB
        nit = d * nblk

        for oi in range(3):
            dist = jnp.abs(rel - oi * A_RADIUS)
            for h2 in range(2):
                slope = slopes_ref[gi * A_HEADS + hp * 2 + h2] * (d * LOG2E)
                bias_ref[oi, h2] = jnp.where(dist <= A_RADIUS, -slope * dist.astype(F32), NEG_INF)

        def geometry(it):
            r = it // nblk
            i0 = (it % nblk) * A_QB
            if isinstance(it, int):
                kstart = min(max(i0 - A_RADIUS, 0), cls_len - A_W)
            else:
                kstart = jnp.clip(i0 - A_RADIUS, 0, cls_len - A_W)
            oi = (i0 - kstart) // A_RADIUS
            if d == 1 and isinstance(it, int):
                q_rows = pl.ds(i0, A_QB)
                k_rows = pl.ds(kstart, A_W)
            elif d == 1:
                q_rows = pl.ds(pl.multiple_of(i0, A_QB), A_QB)
                k_rows = pl.ds(pl.multiple_of(kstart, A_RADIUS), A_W)
            else:
                q_rows = pl.ds(r + i0 * d, A_QB, stride=d)
                k_rows = pl.ds(r + kstart * d, A_W, stride=d)
            return q_rows, k_rows, oi

        def load_qk(it):
            q_rows, k_rows, oi = geometry(it)
            q = (q_ref[q_rows, :] * (A_HEAD_DIM ** -0.5 * LOG2E)).astype(BF16)
            return q, k_ref[k_rows, :].astype(BF16), oi

        def scores(qk, slot, h2):
            q, k, oi = qk
            qm = jnp.where(head_lanes[h2], q, jnp.zeros_like(q))
            s = lax.dot_general(qm, k, (((1,), (1,)), ((), ())), preferred_element_type=F32)
            s = s + bias_ref[oi, h2]
            s_ref[slot, h2] = s
            mf_ref[slot, h2] = jnp.broadcast_to(jnp.max(s, axis=1, keepdims=True), (A_QB, LANES))

        def weighted_values(v, slot, h2):
            va = jnp.where(head_lanes_w[h2], v, 1.0).astype(BF16)
            mfull = mf_ref[slot, h2]
            p = jnp.exp2(s_ref[slot, h2] - jnp.concatenate([mfull] * (A_W // LANES), axis=1))
            return jnp.dot(p.astype(BF16), va, preferred_element_type=F32)

        def merge(q_rows, slot, outs):
            m_blk = jnp.where(lo, mf_ref[slot, 0], mf_ref[slot, 1])
            pv_blk = jnp.where(lo, outs[0], outs[1])
            l_blk = pltpu.roll(jnp.where(lo, outs[1], outs[0]), A_HEAD_DIM, 1)
            if first:
                m_ref[q_rows, :] = m_blk
                l_ref[q_rows, :] = l_blk
                acc_ref[q_rows, :] = pv_blk
            else:
                m_old = m_ref[q_rows, :]
                m_new = jnp.maximum(m_old, m_blk)
                a_old = jnp.exp2(m_old - m_new)
                a_blk = jnp.exp2(m_blk - m_new)
                l_new = a_old * l_ref[q_rows, :] + a_blk * l_blk
                acc_new = a_old * acc_ref[q_rows, :] + a_blk * pv_blk
                if final:
                    acc_ref[q_rows, :] = acc_new / l_new
                else:
                    m_ref[q_rows, :] = m_new
                    l_ref[q_rows, :] = l_new
                    acc_ref[q_rows, :] = acc_new

        qk0 = load_qk(0)
        for h2 in range(2):
            scores(qk0, 0, h2)

        def blocks(it0, last):
            for u in range(A_UNROLL):
                has_next = not (last and u == A_UNROLL - 1)
                qk = load_qk(it0 + u + 1) if has_next else None
                q_rows, k_rows, _ = geometry(it0 + u)
                v = v_ref[k_rows, :]
                outs = []
                for h2 in range(2):
                    if has_next:
                        scores(qk, (u + 1) % 2, h2)
                    outs.append(weighted_values(v, u % 2, h2))
                merge(q_rows, u % 2, outs)

        def body(jj, carry):
            blocks(A_UNROLL * jj, False)
            return carry

        lax.fori_loop(0, nit // A_UNROLL - 1, body, 0)
        blocks(nit - A_UNROLL, True)
        if final:
            o_ref[...] = acc_ref[...].astype(o_ref.dtype)

    for step in range(A_NG):
        gi = A_NG - 1 - step
        pl.when(g == step)(functools.partial(group_body, step, gi, A_GROUPS[gi][1]))


def a_attention(qkv, slopes):
    npairs = A_HEADS // 2

    def col(which):
        return lambda b, hp, g: (b, 0, (which * A_NG + (A_NG - 1 - g)) * npairs + hp)

    return pl.pallas_call(
        _a_attn_kernel,
        grid=(BATCH, npairs, A_NG),
        in_specs=[
            pl.BlockSpec(memory_space=pltpu.SMEM),
            pl.BlockSpec((None, SEQ, LANES), col(0)),
            pl.BlockSpec((None, SEQ, LANES), col(1)),
            pl.BlockSpec((None, SEQ, LANES), col(2)),
        ],
        out_specs=pl.BlockSpec((None, SEQ, LANES), lambda b, hp, g: (b, 0, hp)),
        out_shape=jax.ShapeDtypeStruct((BATCH, SEQ, D_MODEL), BF16),
        scratch_shapes=[
            pltpu.VMEM((SEQ, LANES), F32),
            pltpu.VMEM((SEQ, LANES), F32),
            pltpu.VMEM((SEQ, LANES), F32),
            pltpu.VMEM((2, 2, A_QB, A_W), F32),
            pltpu.VMEM((2, 2, A_QB, LANES), F32),
            pltpu.VMEM((3, 2, A_QB, A_W), F32),
        ],
        compiler_params=_cparams(("parallel", "parallel", "arbitrary")),
        name="a_attention",
    )(slopes, qkv, qkv, qkv)


def _flash_kernel(qt_ref, k_ref, vt_ref, o_ref, m_ref, acc_ref, s_ref, mc_ref, *, nh, dv, shared_kv):
    m_ref[...] = jnp.full(m_ref.shape, -jnp.inf, F32)
    acc_ref[...] = jnp.zeros(acc_ref.shape, F32)
    ones = jnp.ones((ONES_ROWS, TK), BF16)
    nchunk = SEQ // TK

    def chunk_rows(chunk):
        start = chunk * TK
        return pl.ds(start if isinstance(start, int) else pl.multiple_of(start, TK), TK)

    def scores(chunk, slot, h):
        rows = chunk_rows(chunk)
        k = k_ref[rows, :] if shared_kv else k_ref[rows, h * LANES:(h + 1) * LANES]
        st = jnp.dot(k, qt_ref[h * LANES:(h + 1) * LANES, :], preferred_element_type=F32)
        s_ref[slot, h] = st
        mc_ref[slot, h] = jnp.max(st, axis=0, keepdims=True)

    def update(chunk, slot, h):
        rows = chunk_rows(chunk)
        vt = vt_ref[:, rows] if shared_kv else vt_ref[h * dv:(h + 1) * dv, rows]
        m_prev = m_ref[h]
        m_new = jnp.maximum(m_prev, mc_ref[slot, h])
        alpha = jnp.exp2(m_prev - m_new)
        pt = jnp.exp2(s_ref[slot, h] - m_new).astype(BF16)
        pv = jnp.dot(jnp.concatenate([vt, ones], axis=0), pt, preferred_element_type=F32)
        acc_ref[h] = alpha * acc_ref[h] + pv
        m_ref[h] = m_new

    for h in range(nh):
        scores(0, 0, h)

    def chunks(j, last):
        for u in range(FLASH_UNROLL):
            for h in range(nh):
                if not (last and u == FLASH_UNROLL - 1):
                    scores(j + u + 1, (u + 1) % 2, h)
                update(j + u, u % 2, h)

    def body(jj, carry):
        chunks(FLASH_UNROLL * jj, False)
        return carry

    lax.fori_loop(0, nchunk // FLASH_UNROLL - 1, body, 0)
    chunks(nchunk - FLASH_UNROLL, True)

    outs = []
    for h in range(nh):
        acc = acc_ref[h]
        outs.append(acc[:dv, :] / acc[dv:dv + 1, :])
    ot = outs[0] if nh == 1 else jnp.concatenate(outs, axis=0)
    o_ref[...] = ot.T.astype(o_ref.dtype)


def _flash_call(kernel, grid, nh, dv, in_specs, out_spec, args, name):
    return pl.pallas_call(
        kernel,
        grid=grid,
        in_specs=in_specs,
        out_specs=out_spec,
        out_shape=jax.ShapeDtypeStruct((BATCH, SEQ, D_MODEL), BF16),
        scratch_shapes=[
            pltpu.VMEM((nh, 1, TQ), F32),
            pltpu.VMEM((nh, dv + ONES_ROWS, TQ), F32),
            pltpu.VMEM((2, nh, TK, TQ), F32),
            pltpu.VMEM((2, nh, 1, TQ), F32),
        ],
        compiler_params=_cparams(("parallel",) * (len(grid) - 1) + ("arbitrary",)),
        name=name,
    )(*args)


def b_attention(qt, k, vt):
    grp = B_HEADS // B_KV_HEADS
    nh = FLASH_HEADS
    kernel = functools.partial(_flash_kernel, nh=nh, dv=B_HEAD_DIM, shared_kv=True)
    in_specs = [
        pl.BlockSpec((None, nh * LANES, TQ), lambda b, kv, g, i: (b, kv * (grp // nh) + g, i)),
        pl.BlockSpec((None, SEQ, LANES), lambda b, kv, g, i: (b, 0, kv)),
        pl.BlockSpec((None, LANES, SEQ), lambda b, kv, g, i: (b, kv, 0)),
    ]
    out_spec = pl.BlockSpec((None, TQ, nh * LANES), lambda b, kv, g, i: (b, i, kv * (grp // nh) + g))
    grid = (BATCH, B_KV_HEADS, grp // nh, SEQ // TQ)
    return _flash_call(kernel, grid, nh, B_HEAD_DIM, in_specs, out_spec, (qt, k, vt), "b_attention")


def c_attention(qt, k, vt):
    nh = FLASH_HEADS
    kernel = functools.partial(_flash_kernel, nh=nh, dv=C_V, shared_kv=False)
    in_specs = [
        pl.BlockSpec((None, nh * LANES, TQ), lambda b, p, i: (b, p, i)),
        pl.BlockSpec((None, SEQ, nh * LANES), lambda b, p, i: (b, 0, p)),
        pl.BlockSpec((None, nh * C_V, SEQ), lambda b, p, i: (b, p, 0)),
    ]
    out_spec = pl.BlockSpec((None, TQ, nh * C_V), lambda b, p, i: (b, i, p))
    grid = (BATCH, C_HEADS // nh, SEQ // TQ)
    return _flash_call(kernel, grid, nh, C_V, in_specs, out_spec, (qt, k, vt), "c_attention")


def _rope_lanes(x, cos, sin_a, sin_b, shift):
    return (x * cos + pltpu.roll(x, LANES - shift, 1) * sin_a + pltpu.roll(x, shift, 1) * sin_b)


def _nt_dot(a, b):
    return lax.dot_general(a, b, (((1,), (1,)), ((), ())), preferred_element_type=F32)


def _swap_rows(x, start, half, groups):
    pieces = [x[:start]] if start else []
    for g0 in range(start, start + 2 * half * groups, 2 * half):
        pieces += [x[g0 + half:g0 + 2 * half], x[g0:g0 + half]]
    if start + 2 * half * groups < x.shape[0]:
        pieces.append(x[start + 2 * half * groups:])
    return jnp.concatenate(pieces, axis=0)


def _b_prep_kernel(h_ref, g_ref, wqt_ref, wk_ref, wvt_ref, qg_ref, kn_ref, cost_ref, sint_ref,
                   cos_ref, sa_ref, sb_ref, qt_ref, k_ref, vt_ref):
    xn = _rms(h_ref[...], g_ref[...]).astype(BF16)
    cos_t, sin_t, qgain = cost_ref[...], sint_ref[...], qg_ref[...]
    cos, sa, sb = cos_ref[...], sa_ref[...], sb_ref[...]
    half_rows = B_HEADS * LANES // 2

    def q_heads(qt_half, first_head):
        for hd in range(B_HEADS // 2):
            x = qt_half[hd * LANES:(hd + 1) * LANES]
            ms = jnp.mean(x * x, axis=0, keepdims=True)
            x = (x * lax.rsqrt(ms + NORM_EPS)) * qgain
            x = x * cos_t + _swap_rows(x, 0, B_HEAD_DIM // 4, 2) * sin_t
            out_rows = slice((first_head + hd) * LANES, (first_head + hd + 1) * LANES)
            qt_ref[out_rows, :] = x.astype(BF16)

    kk = jnp.dot(xn, wk_ref[...], preferred_element_type=F32)
    qt_a = _nt_dot(wqt_ref[:half_rows, :], xn)
    for j in range(B_KV_HEADS):
        cols = slice(j * LANES, (j + 1) * LANES)
        x = _rope_lanes(_rms(kk[:, cols], kn_ref[...]), cos, sa, sb, B_HEAD_DIM // 4)
        k_ref[:, cols] = x.astype(BF16)
    qt_b = _nt_dot(wqt_ref[half_rows:, :], xn)
    q_heads(qt_a, 0)
    vt = _nt_dot(wvt_ref[...], xn)
    q_heads(qt_b, B_HEADS // 2)
    vt_ref[...] = vt.astype(BF16)


def b_prep(h, gain, wq_t, wk, wv_t, qgain_t, knorm, tables_t, tables):
    t, d = h.shape
    nrow = SEQ // TM_ROW
    row = lambda i: (i, 0)
    fixed = lambda i: (0, 0)
    tab = lambda i: (i % nrow, 0)
    tab_t = lambda i: (0, i % nrow)
    colblk = lambda i: (i // nrow, 0, i % nrow)
    nq = B_HEADS * B_HEAD_DIM
    nkv = B_KV_HEADS * B_HEAD_DIM
    return pl.pallas_call(
        _b_prep_kernel,
        grid=(t // TM_ROW,),
        in_specs=[
            pl.BlockSpec((TM_ROW, d), row),
            pl.BlockSpec((1, d), fixed),
            pl.BlockSpec((nq, d), fixed),
            pl.BlockSpec((d, nkv), fixed),
            pl.BlockSpec((nkv, d), fixed),
            pl.BlockSpec((B_HEAD_DIM, TM_ROW), fixed),
            pl.BlockSpec((1, B_HEAD_DIM), fixed),
            pl.BlockSpec((LANES, TM_ROW), tab_t),
            pl.BlockSpec((LANES, TM_ROW), tab_t),
            pl.BlockSpec((TM_ROW, LANES), tab),
            pl.BlockSpec((TM_ROW, LANES), tab),
            pl.BlockSpec((TM_ROW, LANES), tab),
        ],
        out_specs=[
            pl.BlockSpec((None, nq, TM_ROW), colblk),
            pl.BlockSpec((TM_ROW, nkv), row),
            pl.BlockSpec((None, nkv, TM_ROW), colblk),
        ],
        out_shape=[
            jax.ShapeDtypeStruct((BATCH, nq, SEQ), BF16),
            jax.ShapeDtypeStruct((t, nkv), BF16),
            jax.ShapeDtypeStruct((BATCH, nkv, SEQ), BF16),
        ],
        compiler_params=_cparams(("parallel",)),
        name="b_prep",
    )(h, gain, wq_t, wk, wv_t, qgain_t, knorm, *tables_t, *tables)


def _c_prep_kernel(h_ref, g_ref, win_ref, qn_ref, kvn_ref, wuqt_ref, wuk_ref, wuvt_ref,
                   cost_ref, sint_ref, cos_ref, sa_ref, sb_ref, q_ref, k_ref, v_ref):
    xn = _rms(h_ref[...], g_ref[...]).astype(BF16)
    c = jnp.dot(xn, win_ref[...], preferred_element_type=F32)
    cq = _rms(c[:, :C_Q_LORA], qn_ref[...]).astype(BF16)
    ckv = _rms(c[:, C_Q_LORA:C_Q_LORA + C_KV_LORA], kvn_ref[...]).astype(BF16)
    cos, sa, sb = cos_ref[...], sa_ref[...], sb_ref[...]
    cos_t, sin_t = cost_ref[...], sint_ref[...]
    half_rows = C_HEADS * LANES // 2

    def q_heads(qt_half, first_head):
        for hd in range(C_HEADS // 2):
            x = qt_half[hd * LANES:(hd + 1) * LANES]
            x = x * cos_t + _swap_rows(x, C_NOPE, C_ROPE // 2, 1) * sin_t
            out_rows = slice((first_head + hd) * LANES, (first_head + hd + 1) * LANES)
            q_ref[out_rows, :] = x.astype(BF16)

    kn = jnp.dot(ckv, wuk_ref[...], preferred_element_type=F32)
    qt_a = _nt_dot(wuqt_ref[:half_rows, :], cq)
    kr = _rope_lanes(c[:, C_Q_LORA + C_KV_LORA:], cos, sa, sb, C_ROPE // 2)
    for hd in range(C_HEADS):
        cols = slice(hd * LANES, (hd + 1) * LANES)
        k_ref[:, cols] = (kn[:, cols] + kr).astype(BF16)
    qt_b = _nt_dot(wuqt_ref[half_rows:, :], cq)
    q_heads(qt_a, 0)
    vt = _nt_dot(wuvt_ref[...], ckv)
    q_heads(qt_b, C_HEADS // 2)
    v_ref[...] = vt.astype(BF16)


def c_prep(h, gain, win, qnorm, kvnorm, wuq_t, wuk, wuv_t, tables_t, tables):
    t, d = h.shape
    nrow = SEQ // TM_ROW
    row = lambda i: (i, 0)
    fixed = lambda i: (0, 0)
    tab = lambda i: (i % nrow, 0)
    tab_t = lambda i: (0, i % nrow)
    colblk = lambda i: (i // nrow, 0, i % nrow)
    hq = C_HEADS * LANES
    return pl.pallas_call(
        _c_prep_kernel,
        grid=(t // TM_ROW,),
        in_specs=[
            pl.BlockSpec((TM_ROW, d), row),
            pl.BlockSpec((1, d), fixed),
            pl.BlockSpec((d, C_IN_PAD), fixed),
            pl.BlockSpec((1, C_Q_LORA), fixed),
            pl.BlockSpec((1, C_KV_LORA), fixed),
            pl.BlockSpec((hq, C_Q_LORA), fixed),
            pl.BlockSpec((C_KV_LORA, hq), fixed),
            pl.BlockSpec((C_HEADS * C_V, C_KV_LORA), fixed),
            pl.BlockSpec((LANES, TM_ROW), tab_t),
            pl.BlockSpec((LANES, TM_ROW), tab_t),
            pl.BlockSpec((TM_ROW, LANES), tab),
            pl.BlockSpec((TM_ROW, LANES), tab),
            pl.BlockSpec((TM_ROW, LANES), tab),
        ],
        out_specs=[
            pl.BlockSpec((None, hq, TM_ROW), colblk),
            pl.BlockSpec((TM_ROW, hq), row),
            pl.BlockSpec((None, C_HEADS * C_V, TM_ROW), colblk),
        ],
        out_shape=[
            jax.ShapeDtypeStruct((BATCH, hq, SEQ), BF16),
            jax.ShapeDtypeStruct((t, hq), BF16),
            jax.ShapeDtypeStruct((BATCH, C_HEADS * C_V, SEQ), BF16),
        ],
        compiler_params=_cparams(("parallel",)),
        name="c_prep",
    )(h, gain, win, qnorm, kvnorm, wuq_t, wuk, wuv_t, *tables_t, *tables)


def _mix_ffn_kernel(y_ref, wo_ref, gmix_ref, h_ref, gpre_ref, wg_ref, wu_ref, wd_ref, gpost_ref,
                    o_ref):
    sub = TM_ROW // FFN_SPLIT
    parts = [slice(i * sub, (i + 1) * sub) for i in range(FFN_SPLIT)]
    z = [jnp.dot(y_ref[p, :], wo_ref[...], preferred_element_type=F32) for p in parts]
    h1, gate, up, down = [], [], [], []
    for i, p in enumerate(parts):
        h1.append(h_ref[p, :] + _rms(z[i], gmix_ref[...]))
        xn = _rms(h1[i], gpre_ref[...]).astype(BF16)
        gate.append(jnp.dot(xn, wg_ref[...], preferred_element_type=F32))
        up.append(jnp.dot(xn, wu_ref[...], preferred_element_type=F32))
    for i in range(FFN_SPLIT):
        act = (gate[i] * jax.nn.sigmoid(gate[i])) * up[i]
        down.append(jnp.dot(act.astype(BF16), wd_ref[...], preferred_element_type=F32))
    for i, p in enumerate(parts):
        o_ref[p, :] = h1[i] + _rms(down[i], gpost_ref[...])


def mix_ffn(y, wo, gmix, h, gpre, wg, wu, wd, layer, gpost):
    t, d = h.shape
    row = lambda i: (i, 0)
    fixed = lambda i: (0, 0)
    layer_blk = lambda i: (layer, 0, 0)
    once = pl.Buffered(1)
    return pl.pallas_call(
        _mix_ffn_kernel,
        grid=(t // TM_ROW,),
        in_specs=[
            pl.BlockSpec((TM_ROW, d), row),
            pl.BlockSpec((d, d), fixed, pipeline_mode=once),
            pl.BlockSpec((1, d), fixed),
            pl.BlockSpec((TM_ROW, d), row),
            pl.BlockSpec((1, d), fixed),
            pl.BlockSpec((None, d, D_FF), layer_blk, pipeline_mode=once),
            pl.BlockSpec((None, d, D_FF), layer_blk, pipeline_mode=once),
            pl.BlockSpec((None, D_FF, d), layer_blk, pipeline_mode=once),
            pl.BlockSpec((1, d), fixed),
        ],
        out_specs=pl.BlockSpec((TM_ROW, d), row),
        out_shape=jax.ShapeDtypeStruct((t, d), F32),
        compiler_params=_cparams(("parallel",)),
        name="mix_ffn",
    )(y, wo, gmix, h, gpre, wg, wu, wd, gpost)


def _angles(pos, n_freq, dim, theta):
    freqs = jnp.power(jnp.float32(theta), -jnp.arange(n_freq, dtype=F32) * 2.0 / dim)
    ang = freqs[:, None] * pos[None, :]
    return lax.optimization_barrier((jnp.cos(ang), jnp.sin(ang)))


def _b_tables():
    quarter = B_HEAD_DIM // 4
    rows = SEQ // GRID_W
    rc, rs = _angles(jnp.arange(rows, dtype=F32), quarter, B_HEAD_DIM // 2, B_ROPE_THETA)
    cc, cs = _angles(jnp.arange(GRID_W, dtype=F32), quarter, B_HEAD_DIM // 2, B_ROPE_THETA)
    by_row = lambda x: jnp.repeat(x, GRID_W, axis=1)
    by_col = lambda x: jnp.tile(x, (1, rows))
    rc, rs, cc, cs = by_row(rc), by_row(rs), by_col(cc), by_col(cs)
    zero = jnp.zeros_like(rs)
    cos_t = jnp.concatenate([rc, rc, cc, cc], axis=0)
    sin_t = jnp.concatenate([-rs, rs, -cs, cs], axis=0)
    sin_a = jnp.concatenate([-rs, zero, -cs, zero], axis=0)
    sin_b = jnp.concatenate([zero, rs, zero, cs], axis=0)
    scale = B_HEAD_DIM ** -0.5 * LOG2E
    return (cos_t * scale, sin_t * scale), (cos_t.T, sin_a.T, sin_b.T)


def _c_tables():
    half = C_ROPE // 2
    c, s = _angles(jnp.arange(SEQ, dtype=F32), half, C_ROPE, C_ROPE_THETA)
    one_lo = jnp.ones((C_NOPE, SEQ), F32)
    one_hi = jnp.ones((LANES - C_NOPE - C_ROPE, SEQ), F32)
    zero = jnp.zeros_like(s)
    cos_t = jnp.concatenate([one_lo, c, c, one_hi], axis=0)
    sin_t = jnp.concatenate([0 * one_lo, -s, s, 0 * one_hi], axis=0)
    sin_a = jnp.concatenate([0 * one_lo, -s, zero, 0 * one_hi], axis=0)
    sin_b = jnp.concatenate([0 * one_lo, zero, s, 0 * one_hi], axis=0)
    scale = (C_NOPE + C_ROPE) ** -0.5 * LOG2E
    return (cos_t * scale, sin_t * scale), (cos_t.T, sin_a.T, sin_b.T)


def _alibi_slopes():
    n = A_NG * A_HEADS
    return jnp.asarray(2.0 ** (-8.0 * np.arange(1, n + 1) / n), dtype=F32)


def _mixer_a(h, gain, wqkv_stack, layer):
    qkv = norm_matmul(h, gain, wqkv_stack, layer)
    return a_attention(qkv.reshape(BATCH, SEQ, A_QKV), _alibi_slopes())


def _mixer_b(h, gain, wqkv, qnorm, knorm):
    nq = B_HEADS * B_HEAD_DIM
    nkv = B_KV_HEADS * B_HEAD_DIM
    w = wqkv.astype(BF16)
    qgain_t = jnp.broadcast_to(qnorm[:, None], (B_HEAD_DIM, TM_ROW))
    tables_t, tables = _b_tables()
    qt, k, vt = b_prep(h, gain, w[:, :nq].T, w[:, nq:nq + nkv], w[:, nq + nkv:].T, qgain_t,
                       knorm[None, :], tables_t, tables)
    return b_attention(qt, k.reshape(BATCH, SEQ, -1), vt)


def _mixer_c(h, gain, win, qnorm, kvnorm, wuq, wukv):
    win_p = jnp.zeros((D_MODEL, C_IN_PAD), F32)
    win_p = win_p.at[:, :C_Q_LORA + C_KV_LORA].set(win[:, :C_Q_LORA + C_KV_LORA])
    kr0 = C_Q_LORA + C_KV_LORA + C_NOPE
    win_p = win_p.at[:, kr0:kr0 + C_ROPE].set(win[:, C_Q_LORA + C_KV_LORA:])
    wuq_p = jnp.pad(wuq.reshape(C_Q_LORA, C_HEADS, C_NOPE + C_ROPE),
                    ((0, 0), (0, 0), (0, LANES - C_NOPE - C_ROPE))).reshape(C_Q_LORA, C_HEADS * LANES)
    wukv3 = wukv.reshape(C_KV_LORA, C_HEADS, C_NOPE + C_V)
    wuk_p = jnp.pad(wukv3[:, :, :C_NOPE],
                    ((0, 0), (0, 0), (0, LANES - C_NOPE))).reshape(C_KV_LORA, C_HEADS * LANES)
    wuv = wukv3[:, :, C_NOPE:].reshape(C_KV_LORA, C_HEADS * C_V)
    tables_t, tables = _c_tables()
    qt, k, vt = c_prep(h, gain, win_p.astype(BF16), qnorm[None, :], kvnorm[None, :],
                       wuq_p.astype(BF16).T, wuk_p.astype(BF16), wuv.astype(BF16).T,
                       tables_t, tables)
    return c_attention(qt, k.reshape(BATCH, SEQ, -1), vt)


def kernel(x, norm_mix_pre, norm_mix_post, norm_ffn_pre, norm_ffn_post, ffn_wg, ffn_wu, ffn_wd,
           a_wqkv, a_wo, b_wqkv, b_qnorm, b_knorm, b_wo,
           c_win, c_qnorm, c_kvnorm, c_wuq, c_wukv, c_wo):
    h = x.reshape(BATCH * SEQ, D_MODEL)
    a_wqkv_b = a_wqkv.astype(BF16)
    wg_b, wu_b, wd_b = ffn_wg.astype(BF16), ffn_wu.astype(BF16), ffn_wd.astype(BF16)
    for i in range(DEPTH):
        kind = i % N_MIXERS
        j = i // N_MIXERS
        gpre = norm_mix_pre[i][None, :]
        if kind == 0:
            y, wo = _mixer_a(h, gpre, a_wqkv_b, j), a_wo[j]
        elif kind == 1:
            y, wo = _mixer_b(h, gpre, b_wqkv[j], b_qnorm[j], b_knorm[j]), b_wo[j]
        else:
            y, wo = _mixer_c(h, gpre, c_win[j], c_qnorm[j], c_kvnorm[j], c_wuq[j], c_wukv[j]), c_wo[j]
        h = mix_ffn(y.reshape(BATCH * SEQ, D_MODEL), wo.astype(BF16), norm_mix_post[i][None, :], h,
                    norm_ffn_pre[i][None, :], wg_b, wu_b, wd_b, i, norm_ffn_post[i][None, :])
    return h.reshape(BATCH, SEQ, D_MODEL)
```

```python
import functools

import numpy as np
import jax
import jax.numpy as jnp
from jax import lax
from jax.experimental import pallas as pl
from jax.experimental.pallas import tpu as pltpu

F32 = jnp.float32
BF16 = jnp.bfloat16

D_MODEL = 1024
BATCH = 2
SEQ = 8192
DEPTH = 4
N_MIXERS = 3
GRID_W = 64
D_FF = 2816
NORM_EPS = 1e-6
NEG_INF = -1e30

A_GROUPS = ((128, 1), (512, 4), (2048, 16))
A_NG = 3
A_HEAD_DIM = 64
A_HEADS = 16
A_QKV = 3 * A_NG * A_HEADS * A_HEAD_DIM
A_RADIUS = 64

B_HEAD_DIM = 128
B_HEADS = 8
B_KV_HEADS = 2
B_ROPE_THETA = 10000.0
B_QKV = (B_HEADS + 2 * B_KV_HEADS) * B_HEAD_DIM

C_HEADS = 16
C_Q_LORA = 384
C_KV_LORA = 256
C_NOPE = 64
C_ROPE = 32
C_V = 64
C_ROPE_THETA = 10000.0
C_IN_PAD = 768

LANES = 128
ONES_ROWS = 16
LOG2E = 1.4426950408889634
VMEM_LIMIT = 56 * 1024 * 1024

TM_PROJ = 256
TN_PROJ = 1024
TM_ROW = 512
FFN_SPLIT = 2
TQ = 512
TK = 512
FLASH_UNROLL = 4
FLASH_HEADS = 4
A_QB = 256
A_W = A_QB + 2 * A_RADIUS
A_HB = A_QB // 2
A_HW = A_HB + 2 * A_RADIUS
A_UNROLL = 4


def _cparams(sem):
    return pltpu.CompilerParams(dimension_semantics=sem, vmem_limit_bytes=VMEM_LIMIT)


def _rms(x, gain):
    ms = jnp.mean(x * x, axis=-1, keepdims=True)
    return (x * lax.rsqrt(ms + NORM_EPS)) * gain


def _norm_matmul_kernel(x_ref, g_ref, w_ref, o_ref):
    xn = _rms(x_ref[...], g_ref[...]).astype(BF16)
    for c in range(o_ref.shape[1] // TN_PROJ):
        cols = slice(c * TN_PROJ, (c + 1) * TN_PROJ)
        o_ref[:, cols] = jnp.dot(xn, w_ref[:, cols], preferred_element_type=F32)


def norm_matmul(h, gain, w_stack, layer):
    t, d = h.shape
    n = w_stack.shape[2]
    return pl.pallas_call(
        _norm_matmul_kernel,
        grid=(t // TM_PROJ,),
        in_specs=[
            pl.BlockSpec((TM_PROJ, d), lambda i: (i, 0)),
            pl.BlockSpec((1, d), lambda i: (0, 0)),
            pl.BlockSpec((None, d, n), lambda i: (layer, 0, 0), pipeline_mode=pl.Buffered(1)),
        ],
        out_specs=pl.BlockSpec((TM_PROJ, n), lambda i: (i, 0)),
        out_shape=jax.ShapeDtypeStruct((t, n), F32),
        compiler_params=_cparams(("parallel",)),
        name="a_qkv_proj",
    )(h, gain, w_stack)


def _a_attn_kernel(slopes_ref, q_ref, k_ref, v_ref, o_ref, m_ref, l_ref, acc_ref,
                   s_ref, mf_ref, bias_ref, kbuf_ref, vbuf_ref):
    hp = pl.program_id(1)
    g = pl.program_id(2)
    lane = lax.broadcasted_iota(jnp.int32, (A_QB, LANES), 1)
    lo = lane < A_HEAD_DIM
    head_lanes = (lo, jnp.logical_not(lo))
    lo_w = lax.broadcasted_iota(jnp.int32, (A_W, LANES), 1) < A_HEAD_DIM
    head_lanes_w = (lo_w, jnp.logical_not(lo_w))
    rel = (lax.broadcasted_iota(jnp.int32, (A_HB, A_HW), 1)
           - lax.broadcasted_iota(jnp.int32, (A_HB, A_HW), 0))

    def group_body(step, gi, d):
        first = step == 0
        final = step == A_NG - 1
        cls_len = SEQ // d
        nblk = cls_len // A_QB
        nit = d * nblk

        for oi in range(3):
            dist = jnp.abs(rel - oi * A_RADIUS)
            for h2 in range(2):
                slope = slopes_ref[gi * A_HEADS + hp * 2 + h2] * (d * LOG2E)
                bias_ref[oi, h2] = jnp.where(dist <= A_RADIUS, -slope * dist.astype(F32), NEG_INF)

        def geometry(it):
            r = it // nblk
            i0 = (it % nblk) * A_QB
            if isinstance(it, int):
                kstart = min(max(i0 - A_RADIUS, 0), cls_len - A_W)
            else:
                kstart = jnp.clip(i0 - A_RADIUS, 0, cls_len - A_W)
            oi = (i0 - kstart) // A_RADIUS
            if d == 1 and isinstance(it, int):
                q_rows = pl.ds(i0, A_QB)
                k_rows = pl.ds(kstart, A_W)
            elif d == 1:
                q_rows = pl.ds(pl.multiple_of(i0, A_QB), A_QB)
                k_rows = pl.ds(pl.multiple_of(kstart, A_RADIUS), A_W)
            else:
                q_rows = pl.ds(r + i0 * d, A_QB, stride=d)
                k_rows = pl.ds(r + kstart * d, A_W, stride=d)
            return q_rows, k_rows, oi

        def half_window(t, oi):
            lo_row = t * A_HB + (oi - 1) * A_RADIUS
            if isinstance(oi, int):
                start = min(max(lo_row, 0), A_W - A_HW)
                return pl.ds(start, A_HW), oi + 2 * t - start // A_RADIUS
            start = jnp.clip(lo_row, 0, A_W - A_HW)
            return pl.ds(pl.multiple_of(start, A_RADIUS), A_HW), oi + 2 * t - start // A_RADIUS

        def load_qk(it):
            q_rows, k_rows, oi = geometry(it)
            kbuf_ref[...] = k_ref[k_rows, :].astype(BF16)
            return (q_ref[q_rows, :] * (A_HEAD_DIM ** -0.5 * LOG2E)).astype(BF16), oi

        def scores(qk, slot, h2):
            q, oi = qk
            qm = jnp.where(head_lanes[h2], q, jnp.zeros_like(q))
            for t in range(A_QB // A_HB):
                rows = slice(t * A_HB, (t + 1) * A_HB)
                k_rows, boff = half_window(t, oi)
                s = lax.dot_general(qm[rows], kbuf_ref[k_rows, :], (((1,), (1,)), ((), ())),
                                    preferred_element_type=F32)
                s = s + bias_ref[boff, h2]
                s_ref[slot, h2, rows] = s
                mf_ref[slot, h2, rows] = jnp.broadcast_to(jnp.max(s, axis=1, keepdims=True),
                                                           (A_HB, LANES))

        def load_v(k_rows):
            v = v_ref[k_rows, :]
            for h2 in range(2):
                vbuf_ref[h2] = jnp.where(head_lanes_w[h2], v, 1.0).astype(BF16)

        def weighted_values(oi, slot, h2):
            outs = []
            for t in range(A_QB // A_HB):
                rows = slice(t * A_HB, (t + 1) * A_HB)
                k_rows, _ = half_window(t, oi)
                mfull = mf_ref[slot, h2, rows]
                p = jnp.exp2(s_ref[slot, h2, rows] - jnp.concatenate([mfull] * (A_HW // LANES), axis=1))
                outs.append(jnp.dot(p.astype(BF16), vbuf_ref[h2, k_rows, :], preferred_element_type=F32))
            return jnp.concatenate(outs, axis=0)

        def merge(q_rows, slot, outs):
            m_blk = jnp.where(lo, mf_ref[slot, 0], mf_ref[slot, 1])
            pv_blk = jnp.where(lo, outs[0], outs[1])
            l_blk = pltpu.roll(jnp.where(lo, outs[1], outs[0]), A_HEAD_DIM, 1)
            if first:
                m_ref[q_rows, :] = m_blk
                l_ref[q_rows, :] = l_blk
                acc_ref[q_rows, :] = pv_blk
            else:
                m_old = m_ref[q_rows, :]
                m_new = jnp.maximum(m_old, m_blk)
                a_old = jnp.exp2(m_old - m_new)
                a_blk = jnp.exp2(m_blk - m_new)
                l_new = a_old * l_ref[q_rows, :] + a_blk * l_blk
                acc_new = a_old * acc_ref[q_rows, :] + a_blk * pv_blk
                if final:
                    acc_ref[q_rows, :] = acc_new / l_new
                else:
                    m_ref[q_rows, :] = m_new
                    l_ref[q_rows, :] = l_new
                    acc_ref[q_rows, :] = acc_new

        qk0 = load_qk(0)
        for h2 in range(2):
            scores(qk0, 0, h2)

        def blocks(it0, last):
            for u in range(A_UNROLL):
                has_next = not (last and u == A_UNROLL - 1)
                qk = load_qk(it0 + u + 1) if has_next else None
                q_rows, k_rows, oi = geometry(it0 + u)
                load_v(k_rows)
                outs = []
                for h2 in range(2):
                    if has_next:
                        scores(qk, (u + 1) % 2, h2)
                    outs.append(weighted_values(oi, u % 2, h2))
                merge(q_rows, u % 2, outs)

        def body(jj, carry):
            blocks(A_UNROLL * jj, False)
            return carry

        lax.fori_loop(0, nit // A_UNROLL - 1, body, 0)
        blocks(nit - A_UNROLL, True)
        if final:
            o_ref[...] = acc_ref[...].astype(o_ref.dtype)

    for step in range(A_NG):
        gi = A_NG - 1 - step
        pl.when(g == step)(functools.partial(group_body, step, gi, A_GROUPS[gi][1]))


def a_attention(qkv, slopes):
    npairs = A_HEADS // 2

    def col(which):
        return lambda b, hp, g: (b, 0, (which * A_NG + (A_NG - 1 - g)) * npairs + hp)

    return pl.pallas_call(
        _a_attn_kernel,
        grid=(BATCH, npairs, A_NG),
        in_specs=[
            pl.BlockSpec(memory_space=pltpu.SMEM),
            pl.BlockSpec((None, SEQ, LANES), col(0)),
            pl.BlockSpec((None, SEQ, LANES), col(1)),
            pl.BlockSpec((None, SEQ, LANES), col(2)),
        ],
        out_specs=pl.BlockSpec((None, SEQ, LANES), lambda b, hp, g: (b, 0, hp)),
        out_shape=jax.ShapeDtypeStruct((BATCH, SEQ, D_MODEL), BF16),
        scratch_shapes=[
            pltpu.VMEM((SEQ, LANES), F32),
            pltpu.VMEM((SEQ, LANES), F32),
            pltpu.VMEM((SEQ, LANES), F32),
            pltpu.VMEM((2, 2, A_QB, A_HW), F32),
            pltpu.VMEM((2, 2, A_QB, LANES), F32),
            pltpu.VMEM((3, 2, A_HB, A_HW), F32),
            pltpu.VMEM((A_W, LANES), BF16),
            pltpu.VMEM((2, A_W, LANES), BF16),
        ],
        compiler_params=_cparams(("parallel", "parallel", "arbitrary")),
        name="a_attention",
    )(slopes, qkv, qkv, qkv)


def _flash_kernel(qt_ref, k_ref, vt_ref, o_ref, m_ref, acc_ref, s_ref, mc_ref, *, nh, dv, shared_kv):
    m_ref[...] = jnp.full(m_ref.shape, -jnp.inf, F32)
    acc_ref[...] = jnp.zeros(acc_ref.shape, F32)
    ones = jnp.ones((ONES_ROWS, TK), BF16)
    nchunk = SEQ // TK

    def chunk_rows(chunk):
        start = chunk * TK
        return pl.ds(start if isinstance(start, int) else pl.multiple_of(start, TK), TK)

    def scores(chunk, slot, h):
        rows = chunk_rows(chunk)
        k = k_ref[rows, :] if shared_kv else k_ref[rows, h * LANES:(h + 1) * LANES]
        st = jnp.dot(k, qt_ref[h * LANES:(h + 1) * LANES, :], preferred_element_type=F32)
        s_ref[slot, h] = st
        mc_ref[slot, h] = jnp.max(st, axis=0, keepdims=True)

    def update(chunk, slot, h):
        rows = chunk_rows(chunk)
        vt = vt_ref[:, rows] if shared_kv else vt_ref[h * dv:(h + 1) * dv, rows]
        m_prev = m_ref[h]
        m_new = jnp.maximum(m_prev, mc_ref[slot, h])
        alpha = jnp.exp2(m_prev - m_new)
        pt = jnp.exp2(s_ref[slot, h] - m_new).astype(BF16)
        pv = jnp.dot(jnp.concatenate([vt, ones], axis=0), pt, preferred_element_type=F32)
        acc_ref[h] = alpha * acc_ref[h] + pv
        m_ref[h] = m_new

    for h in range(nh):
        scores(0, 0, h)

    def chunks(j, last):
        for u in range(FLASH_UNROLL):
            for h in range(nh):
                if not (last and u == FLASH_UNROLL - 1):
                    scores(j + u + 1, (u + 1) % 2, h)
                update(j + u, u % 2, h)

    def body(jj, carry):
        chunks(FLASH_UNROLL * jj, False)
        return carry

    lax.fori_loop(0, nchunk // FLASH_UNROLL - 1, body, 0)
    chunks(nchunk - FLASH_UNROLL, True)

    outs = []
    for h in range(nh):
        acc = acc_ref[h]
        outs.append(acc[:dv, :] / acc[dv:dv + 1, :])
    ot = outs[0] if nh == 1 else jnp.concatenate(outs, axis=0)
    o_ref[...] = ot.T.astype(o_ref.dtype)


def _flash_call(kernel, grid, nh, dv, in_specs, out_spec, args, name):
    return pl.pallas_call(
        kernel,
        grid=grid,
        in_specs=in_specs,
        out_specs=out_spec,
        out_shape=jax.ShapeDtypeStruct((BATCH, SEQ, D_MODEL), BF16),
        scratch_shapes=[
            pltpu.VMEM((nh, 1, TQ), F32),
            pltpu.VMEM((nh, dv + ONES_ROWS, TQ), F32),
            pltpu.VMEM((2, nh, TK, TQ), F32),
            pltpu.VMEM((2, nh, 1, TQ), F32),
        ],
        compiler_params=_cparams(("parallel",) * (len(grid) - 1) + ("arbitrary",)),
        name=name,
    )(*args)


def b_attention(qt, k, vt):
    grp = B_HEADS // B_KV_HEADS
    nh = FLASH_HEADS
    kernel = functools.partial(_flash_kernel, nh=nh, dv=B_HEAD_DIM, shared_kv=True)
    in_specs = [
        pl.BlockSpec((None, nh * LANES, TQ), lambda b, kv, g, i: (b, kv * (grp // nh) + g, i)),
        pl.BlockSpec((None, SEQ, LANES), lambda b, kv, g, i: (b, 0, kv)),
        pl.BlockSpec((None, LANES, SEQ), lambda b, kv, g, i: (b, kv, 0)),
    ]
    out_spec = pl.BlockSpec((None, TQ, nh * LANES), lambda b, kv, g, i: (b, i, kv * (grp // nh) + g))
    grid = (BATCH, B_KV_HEADS, grp // nh, SEQ // TQ)
    return _flash_call(kernel, grid, nh, B_HEAD_DIM, in_specs, out_spec, (qt, k, vt), "b_attention")


def c_attention(qt, k, vt):
    nh = FLASH_HEADS
    kernel = functools.partial(_flash_kernel, nh=nh, dv=C_V, shared_kv=False)
    in_specs = [
        pl.BlockSpec((None, nh * LANES, TQ), lambda b, p, i: (b, p, i)),
        pl.BlockSpec((None, SEQ, nh * LANES), lambda b, p, i: (b, 0, p)),
        pl.BlockSpec((None, nh * C_V, SEQ), lambda b, p, i: (b, p, 0)),
    ]
    out_spec = pl.BlockSpec((None, TQ, nh * C_V), lambda b, p, i: (b, i, p))
    grid = (BATCH, C_HEADS // nh, SEQ // TQ)
    return _flash_call(kernel, grid, nh, C_V, in_specs, out_spec, (qt, k, vt), "c_attention")


def _rope_lanes(x, cos, sin_a, sin_b, shift):
    return (x * cos + pltpu.roll(x, LANES - shift, 1) * sin_a + pltpu.roll(x, shift, 1) * sin_b)


def _nt_dot(a, b):
    return lax.dot_general(a, b, (((1,), (1,)), ((), ())), preferred_element_type=F32)


def _swap_rows(x, start, half, groups):
    pieces = [x[:start]] if start else []
    for g0 in range(start, start + 2 * half * groups, 2 * half):
        pieces += [x[g0 + half:g0 + 2 * half], x[g0:g0 + half]]
    if start + 2 * half * groups < x.shape[0]:
        pieces.append(x[start + 2 * half * groups:])
    return jnp.concatenate(pieces, axis=0)


def _b_prep_kernel(h_ref, g_ref, wqt_ref, wk_ref, wvt_ref, qg_ref, kn_ref, cost_ref, sint_ref,
                   cos_ref, sa_ref, sb_ref, qt_ref, k_ref, vt_ref):
    xn = _rms(h_ref[...], g_ref[...]).astype(BF16)
    cos_t, sin_t, qgain = cost_ref[...], sint_ref[...], qg_ref[...]
    cos, sa, sb = cos_ref[...], sa_ref[...], sb_ref[...]
    half_rows = B_HEADS * LANES // 2

    def q_heads(qt_half, first_head):
        for hd in range(B_HEADS // 2):
            x = qt_half[hd * LANES:(hd + 1) * LANES]
            ms = jnp.mean(x * x, axis=0, keepdims=True)
            x = (x * lax.rsqrt(ms + NORM_EPS)) * qgain
            x = x * cos_t + _swap_rows(x, 0, B_HEAD_DIM // 4, 2) * sin_t
            out_rows = slice((first_head + hd) * LANES, (first_head + hd + 1) * LANES)
            qt_ref[out_rows, :] = x.astype(BF16)

    kk = jnp.dot(xn, wk_ref[...], preferred_element_type=F32)
    qt_a = _nt_dot(wqt_ref[:half_rows, :], xn)
    for j in range(B_KV_HEADS):
        cols = slice(j * LANES, (j + 1) * LANES)
        x = _rope_lanes(_rms(kk[:, cols], kn_ref[...]), cos, sa, sb, B_HEAD_DIM // 4)
        k_ref[:, cols] = x.astype(BF16)
    qt_b = _nt_dot(wqt_ref[half_rows:, :], xn)
    q_heads(qt_a, 0)
    vt = _nt_dot(wvt_ref[...], xn)
    q_heads(qt_b, B_HEADS // 2)
    vt_ref[...] = vt.astype(BF16)


def b_prep(h, gain, wq_t, wk, wv_t, qgain_t, knorm, tables_t, tables):
    t, d = h.shape
    nrow = SEQ // TM_ROW
    row = lambda i: (i, 0)
    fixed = lambda i: (0, 0)
    tab = lambda i: (i % nrow, 0)
    tab_t = lambda i: (0, i % nrow)
    colblk = lambda i: (i // nrow, 0, i % nrow)
    nq = B_HEADS * B_HEAD_DIM
    nkv = B_KV_HEADS * B_HEAD_DIM
    return pl.pallas_call(
        _b_prep_kernel,
        grid=(t // TM_ROW,),
        in_specs=[
            pl.BlockSpec((TM_ROW, d), row),
            pl.BlockSpec((1, d), fixed),
            pl.BlockSpec((nq, d), fixed),
            pl.BlockSpec((d, nkv), fixed),
            pl.BlockSpec((nkv, d), fixed),
            pl.BlockSpec((B_HEAD_DIM, TM_ROW), fixed),
            pl.BlockSpec((1, B_HEAD_DIM), fixed),
            pl.BlockSpec((LANES, TM_ROW), tab_t),
            pl.BlockSpec((LANES, TM_ROW), tab_t),
            pl.BlockSpec((TM_ROW, LANES), tab),
            pl.BlockSpec((TM_ROW, LANES), tab),
            pl.BlockSpec((TM_ROW, LANES), tab),
        ],
        out_specs=[
            pl.BlockSpec((None, nq, TM_ROW), colblk),
            pl.BlockSpec((TM_ROW, nkv), row),
            pl.BlockSpec((None, nkv, TM_ROW), colblk),
        ],
        out_shape=[
            jax.ShapeDtypeStruct((BATCH, nq, SEQ), BF16),
            jax.ShapeDtypeStruct((t, nkv), BF16),
            jax.ShapeDtypeStruct((BATCH, nkv, SEQ), BF16),
        ],
        compiler_params=_cparams(("parallel",)),
        name="b_prep",
    )(h, gain, wq_t, wk, wv_t, qgain_t, knorm, *tables_t, *tables)


def _c_prep_kernel(h_ref, g_ref, win_ref, qn_ref, kvn_ref, wuqt_ref, wuk_ref, wuvt_ref,
                   cost_ref, sint_ref, cos_ref, sa_ref, sb_ref, q_ref, k_ref, v_ref):
    xn = _rms(h_ref[...], g_ref[...]).astype(BF16)
    c = jnp.dot(xn, win_ref[...], preferred_element_type=F32)
    cq = _rms(c[:, :C_Q_LORA], qn_ref[...]).astype(BF16)
    ckv = _rms(c[:, C_Q_LORA:C_Q_LORA + C_KV_LORA], kvn_ref[...]).astype(BF16)
    cos, sa, sb = cos_ref[...], sa_ref[...], sb_ref[...]
    cos_t, sin_t = cost_ref[...], sint_ref[...]
    half_rows = C_HEADS * LANES // 2

    def q_heads(qt_half, first_head):
        for hd in range(C_HEADS // 2):
            x = qt_half[hd * LANES:(hd + 1) * LANES]
            x = x * cos_t + _swap_rows(x, C_NOPE, C_ROPE // 2, 1) * sin_t
            out_rows = slice((first_head + hd) * LANES, (first_head + hd + 1) * LANES)
            q_ref[out_rows, :] = x.astype(BF16)

    kn = jnp.dot(ckv, wuk_ref[...], preferred_element_type=F32)
    qt_a = _nt_dot(wuqt_ref[:half_rows, :], cq)
    kr = _rope_lanes(c[:, C_Q_LORA + C_KV_LORA:], cos, sa, sb, C_ROPE // 2)
    for hd in range(C_HEADS):
        cols = slice(hd * LANES, (hd + 1) * LANES)
        k_ref[:, cols] = (kn[:, cols] + kr).astype(BF16)
    qt_b = _nt_dot(wuqt_ref[half_rows:, :], cq)
    q_heads(qt_a, 0)
    vt = _nt_dot(wuvt_ref[...], ckv)
    q_heads(qt_b, C_HEADS // 2)
    v_ref[...] = vt.astype(BF16)


def c_prep(h, gain, win, qnorm, kvnorm, wuq_t, wuk, wuv_t, tables_t, tables):
    t, d = h.shape
    nrow = SEQ // TM_ROW
    row = lambda i: (i, 0)
    fixed = lambda i: (0, 0)
    tab = lambda i: (i % nrow, 0)
    tab_t = lambda i: (0, i % nrow)
    colblk = lambda i: (i // nrow, 0, i % nrow)
    hq = C_HEADS * LANES
    return pl.pallas_call(
        _c_prep_kernel,
        grid=(t // TM_ROW,),
        in_specs=[
            pl.BlockSpec((TM_ROW, d), row),
            pl.BlockSpec((1, d), fixed),
            pl.BlockSpec((d, C_IN_PAD), fixed),
            pl.BlockSpec((1, C_Q_LORA), fixed),
            pl.BlockSpec((1, C_KV_LORA), fixed),
            pl.BlockSpec((hq, C_Q_LORA), fixed),
            pl.BlockSpec((C_KV_LORA, hq), fixed),
            pl.BlockSpec((C_HEADS * C_V, C_KV_LORA), fixed),
            pl.BlockSpec((LANES, TM_ROW), tab_t),
            pl.BlockSpec((LANES, TM_ROW), tab_t),
            pl.BlockSpec((TM_ROW, LANES), tab),
            pl.BlockSpec((TM_ROW, LANES), tab),
            pl.BlockSpec((TM_ROW, LANES), tab),
        ],
        out_specs=[
            pl.BlockSpec((None, hq, TM_ROW), colblk),
            pl.BlockSpec((TM_ROW, hq), row),
            pl.BlockSpec((None, C_HEADS * C_V, TM_ROW), colblk),
        ],
        out_shape=[
            jax.ShapeDtypeStruct((BATCH, hq, SEQ), BF16),
            jax.ShapeDtypeStruct((t, hq), BF16),
            jax.ShapeDtypeStruct((BATCH, C_HEADS * C_V, SEQ), BF16),
        ],
        compiler_params=_cparams(("parallel",)),
        name="c_prep",
    )(h, gain, win, qnorm, kvnorm, wuq_t, wuk, wuv_t, *tables_t, *tables)


def _mix_ffn_kernel(y_ref, wo_ref, gmix_ref, h_ref, gpre_ref, wg_ref, wu_ref, wd_ref, gpost_ref,
                    o_ref):
    sub = TM_ROW // FFN_SPLIT
    parts = [slice(i * sub, (i + 1) * sub) for i in range(FFN_SPLIT)]
    z = [jnp.dot(y_ref[p, :], wo_ref[...], preferred_element_type=F32) for p in parts]
    h1, gate, up, down = [], [], [], []
    for i, p in enumerate(parts):
        h1.append(h_ref[p, :] + _rms(z[i], gmix_ref[...]))
        xn = _rms(h1[i], gpre_ref[...]).astype(BF16)
        gate.append(jnp.dot(xn, wg_ref[...], preferred_element_type=F32))
        up.append(jnp.dot(xn, wu_ref[...], preferred_element_type=F32))
    for i in range(FFN_SPLIT):
        act = (gate[i] * jax.nn.sigmoid(gate[i])) * up[i]
        down.append(jnp.dot(act.astype(BF16), wd_ref[...], preferred_element_type=F32))
    for i, p in enumerate(parts):
        o_ref[p, :] = h1[i] + _rms(down[i], gpost_ref[...])


def mix_ffn(y, wo, gmix, h, gpre, wg, wu, wd, layer, gpost):
    t, d = h.shape
    row = lambda i: (i, 0)
    fixed = lambda i: (0, 0)
    layer_blk = lambda i: (layer, 0, 0)
    once = pl.Buffered(1)
    return pl.pallas_call(
        _mix_ffn_kernel,
        grid=(t // TM_ROW,),
        in_specs=[
            pl.BlockSpec((TM_ROW, d), row),
            pl.BlockSpec((d, d), fixed, pipeline_mode=once),
            pl.BlockSpec((1, d), fixed),
            pl.BlockSpec((TM_ROW, d), row),
            pl.BlockSpec((1, d), fixed),
            pl.BlockSpec((None, d, D_FF), layer_blk, pipeline_mode=once),
            pl.BlockSpec((None, d, D_FF), layer_blk, pipeline_mode=once),
            pl.BlockSpec((None, D_FF, d), layer_blk, pipeline_mode=once),
            pl.BlockSpec((1, d), fixed),
        ],
        out_specs=pl.BlockSpec((TM_ROW, d), row),
        out_shape=jax.ShapeDtypeStruct((t, d), F32),
        compiler_params=_cparams(("parallel",)),
        name="mix_ffn",
    )(y, wo, gmix, h, gpre, wg, wu, wd, gpost)


def _angles(pos, n_freq, dim, theta):
    freqs = jnp.power(jnp.float32(theta), -jnp.arange(n_freq, dtype=F32) * 2.0 / dim)
    ang = freqs[:, None] * pos[None, :]
    return lax.optimization_barrier((jnp.cos(ang), jnp.sin(ang)))


def _b_tables():
    quarter = B_HEAD_DIM // 4
    rows = SEQ // GRID_W
    rc, rs = _angles(jnp.arange(rows, dtype=F32), quarter, B_HEAD_DIM // 2, B_ROPE_THETA)
    cc, cs = _angles(jnp.arange(GRID_W, dtype=F32), quarter, B_HEAD_DIM // 2, B_ROPE_THETA)
    by_row = lambda x: jnp.repeat(x, GRID_W, axis=1)
    by_col = lambda x: jnp.tile(x, (1, rows))
    rc, rs, cc, cs = by_row(rc), by_row(rs), by_col(cc), by_col(cs)
    zero = jnp.zeros_like(rs)
    cos_t = jnp.concatenate([rc, rc, cc, cc], axis=0)
    sin_t = jnp.concatenate([-rs, rs, -cs, cs], axis=0)
    sin_a = jnp.concatenate([-rs, zero, -cs, zero], axis=0)
    sin_b = jnp.concatenate([zero, rs, zero, cs], axis=0)
    scale = B_HEAD_DIM ** -0.5 * LOG2E
    return (cos_t * scale, sin_t * scale), (cos_t.T, sin_a.T, sin_b.T)


def _c_tables():
    half = C_ROPE // 2
    c, s = _angles(jnp.arange(SEQ, dtype=F32), half, C_ROPE, C_ROPE_THETA)
    one_lo = jnp.ones((C_NOPE, SEQ), F32)
    one_hi = jnp.ones((LANES - C_NOPE - C_ROPE, SEQ), F32)
    zero = jnp.zeros_like(s)
    cos_t = jnp.concatenate([one_lo, c, c, one_hi], axis=0)
    sin_t = jnp.concatenate([0 * one_lo, -s, s, 0 * one_hi], axis=0)
    sin_a = jnp.concatenate([0 * one_lo, -s, zero, 0 * one_hi], axis=0)
    sin_b = jnp.concatenate([0 * one_lo, zero, s, 0 * one_hi], axis=0)
    scale = (C_NOPE + C_ROPE) ** -0.5 * LOG2E
    return (cos_t * scale, sin_t * scale), (cos_t.T, sin_a.T, sin_b.T)


def _alibi_slopes():
    n = A_NG * A_HEADS
    return jnp.asarray(2.0 ** (-8.0 * np.arange(1, n + 1) / n), dtype=F32)


def _mixer_a(h, gain, wqkv_stack, layer):
    qkv = norm_matmul(h, gain, wqkv_stack, layer)
    return a_attention(qkv.reshape(BATCH, SEQ, A_QKV), _alibi_slopes())


def _mixer_b(h, gain, wqkv, qnorm, knorm):
    nq = B_HEADS * B_HEAD_DIM
    nkv = B_KV_HEADS * B_HEAD_DIM
    w = wqkv.astype(BF16)
    qgain_t = jnp.broadcast_to(qnorm[:, None], (B_HEAD_DIM, TM_ROW))
    tables_t, tables = _b_tables()
    qt, k, vt = b_prep(h, gain, w[:, :nq].T, w[:, nq:nq + nkv], w[:, nq + nkv:].T, qgain_t,
                       knorm[None, :], tables_t, tables)
    return b_attention(qt, k.reshape(BATCH, SEQ, -1), vt)


def _mixer_c(h, gain, win, qnorm, kvnorm, wuq, wukv):
    win_p = jnp.zeros((D_MODEL, C_IN_PAD), F32)
    win_p = win_p.at[:, :C_Q_LORA + C_KV_LORA].set(win[:, :C_Q_LORA + C_KV_LORA])
    kr0 = C_Q_LORA + C_KV_LORA + C_NOPE
    win_p = win_p.at[:, kr0:kr0 + C_ROPE].set(win[:, C_Q_LORA + C_KV_LORA:])
    wuq_p = jnp.pad(wuq.reshape(C_Q_LORA, C_HEADS, C_NOPE + C_ROPE),
                    ((0, 0), (0, 0), (0, LANES - C_NOPE - C_ROPE))).reshape(C_Q_LORA, C_HEADS * LANES)
    wukv3 = wukv.reshape(C_KV_LORA, C_HEADS, C_NOPE + C_V)
    wuk_p = jnp.pad(wukv3[:, :, :C_NOPE],
                    ((0, 0), (0, 0), (0, LANES - C_NOPE))).reshape(C_KV_LORA, C_HEADS * LANES)
    wuv = wukv3[:, :, C_NOPE:].reshape(C_KV_LORA, C_HEADS * C_V)
    tables_t, tables = _c_tables()
    qt, k, vt = c_prep(h, gain, win_p.astype(BF16), qnorm[None, :], kvnorm[None, :],
                       wuq_p.astype(BF16).T, wuk_p.astype(BF16), wuv.astype(BF16).T,
                       tables_t, tables)
    return c_attention(qt, k.reshape(BATCH, SEQ, -1), vt)


def kernel(x, norm_mix_pre, norm_mix_post, norm_ffn_pre, norm_ffn_post, ffn_wg, ffn_wu, ffn_wd,
           a_wqkv, a_wo, b_wqkv, b_qnorm, b_knorm, b_wo,
           c_win, c_qnorm, c_kvnorm, c_wuq, c_wukv, c_wo):
    h = x.reshape(BATCH * SEQ, D_MODEL)
    a_wqkv_b = a_wqkv.astype(BF16)
    wg_b, wu_b, wd_b = ffn_wg.astype(BF16), ffn_wu.astype(BF16), ffn_wd.astype(BF16)
    for i in range(DEPTH):
        kind = i % N_MIXERS
        j = i // N_MIXERS
        gpre = norm_mix_pre[i][None, :]
        if kind == 0:
            y, wo = _mixer_a(h, gpre, a_wqkv_b, j), a_wo[j]
        elif kind == 1:
            y, wo = _mixer_b(h, gpre, b_wqkv[j], b_qnorm[j], b_knorm[j]), b_wo[j]
        else:
            y, wo = _mixer_c(h, gpre, c_win[j], c_qnorm[j], c_kvnorm[j], c_wuq[j], c_wukv[j]), c_wo[j]
        h = mix_ffn(y.reshape(BATCH * SEQ, D_MODEL), wo.astype(BF16), norm_mix_post[i][None, :], h,
                    norm_ffn_pre[i][None, :], wg_b, wu_b, wd_b, i, norm_ffn_post[i][None, :])
    return h.reshape(BATCH, SEQ, D_MODEL)
```

```python
import functools

import numpy as np
import jax
import jax.numpy as jnp
from jax import lax
from jax.experimental import pallas as pl
from jax.experimental.pallas import tpu as pltpu

F32 = jnp.float32
BF16 = jnp.bfloat16

D_MODEL = 1024
BATCH = 2
SEQ = 8192
DEPTH = 4
N_MIXERS = 3
GRID_W = 64
D_FF = 2816
NORM_EPS = 1e-6
NEG_INF = -1e30

A_GROUPS = ((128, 1), (512, 4), (2048, 16))
A_NG = 3
A_HEAD_DIM = 64
A_HEADS = 16
A_QKV = 3 * A_NG * A_HEADS * A_HEAD_DIM
A_RADIUS = 64

B_HEAD_DIM = 128
B_HEADS = 8
B_KV_HEADS = 2
B_ROPE_THETA = 10000.0
B_QKV = (B_HEADS + 2 * B_KV_HEADS) * B_HEAD_DIM

C_HEADS = 16
C_Q_LORA = 384
C_KV_LORA = 256
C_NOPE = 64
C_ROPE = 32
C_V = 64
C_ROPE_THETA = 10000.0
C_IN_PAD = 768

LANES = 128
ONES_ROWS = 16
LOG2E = 1.4426950408889634
VMEM_LIMIT = 56 * 1024 * 1024

TM_PROJ = 256
TN_PROJ = 1024
TM_ROW = 512
FFN_SPLIT = 2
TQ = 512
TK = 512
FLASH_UNROLL = 4
FLASH_HEADS = 4
A_QB = 512
A_W = A_QB + 2 * A_RADIUS
A_HB = 128
A_HW = A_HB + 2 * A_RADIUS
A_UNROLL = 2


def _cparams(sem):
    return pltpu.CompilerParams(dimension_semantics=sem, vmem_limit_bytes=VMEM_LIMIT)


def _rms(x, gain):
    ms = jnp.mean(x * x, axis=-1, keepdims=True)
    return (x * lax.rsqrt(ms + NORM_EPS)) * gain


def _norm_matmul_kernel(x_ref, g_ref, w_ref, o_ref):
    xn = _rms(x_ref[...], g_ref[...]).astype(BF16)
    for c in range(o_ref.shape[1] // TN_PROJ):
        cols = slice(c * TN_PROJ, (c + 1) * TN_PROJ)
        o_ref[:, cols] = jnp.dot(xn, w_ref[:, cols], preferred_element_type=F32)


def norm_matmul(h, gain, w_stack, layer):
    t, d = h.shape
    n = w_stack.shape[2]
    return pl.pallas_call(
        _norm_matmul_kernel,
        grid=(t // TM_PROJ,),
        in_specs=[
            pl.BlockSpec((TM_PROJ, d), lambda i: (i, 0)),
            pl.BlockSpec((1, d), lambda i: (0, 0)),
            pl.BlockSpec((None, d, n), lambda i: (layer, 0, 0), pipeline_mode=pl.Buffered(1)),
        ],
        out_specs=pl.BlockSpec((TM_PROJ, n), lambda i: (i, 0)),
        out_shape=jax.ShapeDtypeStruct((t, n), F32),
        compiler_params=_cparams(("parallel",)),
        name="a_qkv_proj",
    )(h, gain, w_stack)


def _a_attn_kernel(slopes_ref, q_ref, k_ref, v_ref, o_ref, m_ref, l_ref, acc_ref,
                   s_ref, mf_ref, bias_ref, kbuf_ref, vbuf_ref):
    hp = pl.program_id(1)
    g = pl.program_id(2)
    lane = lax.broadcasted_iota(jnp.int32, (A_QB, LANES), 1)
    lo = lane < A_HEAD_DIM
    head_lanes = (lo, jnp.logical_not(lo))
    rel = (lax.broadcasted_iota(jnp.int32, (A_HB, A_HW), 1)
           - lax.broadcasted_iota(jnp.int32, (A_HB, A_HW), 0))

    def group_body(step, gi, d):
        first = step == 0
        final = step == A_NG - 1
        cls_len = SEQ // d
        nblk = cls_len // A_QB
        nit = d * nblk
        win = min(A_W, cls_len)
        lo_w = lax.broadcasted_iota(jnp.int32, (win, LANES), 1) < A_HEAD_DIM
        head_lanes_w = (lo_w, jnp.logical_not(lo_w))

        for oi in range(3):
            dist = jnp.abs(rel - oi * A_RADIUS)
            for h2 in range(2):
                slope = slopes_ref[gi * A_HEADS + hp * 2 + h2] * (d * LOG2E)
                bias_ref[oi, h2] = jnp.where(dist <= A_RADIUS, -slope * dist.astype(F32), NEG_INF)

        def geometry(it):
            r = it // nblk
            i0 = (it % nblk) * A_QB
            if isinstance(it, int):
                kstart = min(max(i0 - A_RADIUS, 0), cls_len - win)
            else:
                kstart = jnp.clip(i0 - A_RADIUS, 0, cls_len - win)
            oi = (i0 - kstart) // A_RADIUS
            if d == 1 and isinstance(it, int):
                q_rows = pl.ds(i0, A_QB)
                k_rows = pl.ds(kstart, win)
            elif d == 1:
                q_rows = pl.ds(pl.multiple_of(i0, A_QB), A_QB)
                k_rows = pl.ds(pl.multiple_of(kstart, A_RADIUS), win)
            else:
                q_rows = pl.ds(r + i0 * d, A_QB, stride=d)
                k_rows = pl.ds(r + kstart * d, win, stride=d)
            return q_rows, k_rows, oi

        def half_window(t, oi):
            lo_row = t * A_HB + (oi - 1) * A_RADIUS
            if isinstance(oi, int):
                start = min(max(lo_row, 0), win - A_HW)
            else:
                start = pl.multiple_of(jnp.clip(lo_row, 0, win - A_HW), A_RADIUS)
            return pl.ds(start, A_HW), oi + t * (A_HB // A_RADIUS) - start // A_RADIUS

        def load_qk(it):
            q_rows, k_rows, oi = geometry(it)
            kbuf_ref[:win, :] = k_ref[k_rows, :].astype(BF16)
            return (q_ref[q_rows, :] * (A_HEAD_DIM ** -0.5 * LOG2E)).astype(BF16), oi

        def scores(qk, slot, h2):
            q, oi = qk
            qm = jnp.where(head_lanes[h2], q, jnp.zeros_like(q))
            for t in range(A_QB // A_HB):
                rows = slice(t * A_HB, (t + 1) * A_HB)
                k_rows, boff = half_window(t, oi)
                s = lax.dot_general(qm[rows], kbuf_ref[k_rows, :], (((1,), (1,)), ((), ())),
                                    preferred_element_type=F32)
                s = s + bias_ref[boff, h2]
                s_ref[slot, h2, rows] = s
                mf_ref[slot, h2, rows] = jnp.broadcast_to(jnp.max(s, axis=1, keepdims=True),
                                                           (A_HB, LANES))

        def load_v(k_rows):
            v = v_ref[k_rows, :]
            for h2 in range(2):
                vbuf_ref[h2, :win, :] = jnp.where(head_lanes_w[h2], v, 1.0).astype(BF16)

        def weighted_values(oi, slot, h2):
            outs = []
            for t in range(A_QB // A_HB):
                rows = slice(t * A_HB, (t + 1) * A_HB)
                k_rows, _ = half_window(t, oi)
                mfull = mf_ref[slot, h2, rows]
                p = jnp.exp2(s_ref[slot, h2, rows] - jnp.concatenate([mfull] * (A_HW // LANES), axis=1))
                outs.append(jnp.dot(p.astype(BF16), vbuf_ref[h2, k_rows, :], preferred_element_type=F32))
            return jnp.concatenate(outs, axis=0)

        def merge(q_rows, slot, outs):
            m_blk = jnp.where(lo, mf_ref[slot, 0], mf_ref[slot, 1])
            pv_blk = jnp.where(lo, outs[0], outs[1])
            l_blk = pltpu.roll(jnp.where(lo, outs[1], outs[0]), A_HEAD_DIM, 1)
            if first:
                m_ref[q_rows, :] = m_blk
                l_ref[q_rows, :] = l_blk
                acc_ref[q_rows, :] = pv_blk
            else:
                m_old = m_ref[q_rows, :]
                m_new = jnp.maximum(m_old, m_blk)
                a_old = jnp.exp2(m_old - m_new)
                a_blk = jnp.exp2(m_blk - m_new)
                l_new = a_old * l_ref[q_rows, :] + a_blk * l_blk
                acc_new = a_old * acc_ref[q_rows, :] + a_blk * pv_blk
                if final:
                    acc_ref[q_rows, :] = acc_new / l_new
                else:
                    m_ref[q_rows, :] = m_new
                    l_ref[q_rows, :] = l_new
                    acc_ref[q_rows, :] = acc_new

        qk0 = load_qk(0)
        for h2 in range(2):
            scores(qk0, 0, h2)

        def blocks(it0, last):
            for u in range(A_UNROLL):
                has_next = not (last and u == A_UNROLL - 1)
                qk = load_qk(it0 + u + 1) if has_next else None
                q_rows, k_rows, oi = geometry(it0 + u)
                load_v(k_rows)
                outs = []
                for h2 in range(2):
                    if has_next:
                        scores(qk, (u + 1) % 2, h2)
                    outs.append(weighted_values(oi, u % 2, h2))
                merge(q_rows, u % 2, outs)

        def body(jj, carry):
            blocks(A_UNROLL * jj, False)
            return carry

        lax.fori_loop(0, nit // A_UNROLL - 1, body, 0)
        blocks(nit - A_UNROLL, True)
        if final:
            o_ref[...] = acc_ref[...].astype(o_ref.dtype)

    for step in range(A_NG):
        gi = A_NG - 1 - step
        pl.when(g == step)(functools.partial(group_body, step, gi, A_GROUPS[gi][1]))


def a_attention(qkv, slopes):
    npairs = A_HEADS // 2

    def col(which):
        return lambda b, hp, g: (b, 0, (which * A_NG + (A_NG - 1 - g)) * npairs + hp)

    return pl.pallas_call(
        _a_attn_kernel,
        grid=(BATCH, npairs, A_NG),
        in_specs=[
            pl.BlockSpec(memory_space=pltpu.SMEM),
            pl.BlockSpec((None, SEQ, LANES), col(0)),
            pl.BlockSpec((None, SEQ, LANES), col(1)),
            pl.BlockSpec((None, SEQ, LANES), col(2)),
        ],
        out_specs=pl.BlockSpec((None, SEQ, LANES), lambda b, hp, g: (b, 0, hp)),
        out_shape=jax.ShapeDtypeStruct((BATCH, SEQ, D_MODEL), BF16),
        scratch_shapes=[
            pltpu.VMEM((SEQ, LANES), F32),
            pltpu.VMEM((SEQ, LANES), F32),
            pltpu.VMEM((SEQ, LANES), F32),
            pltpu.VMEM((2, 2, A_QB, A_HW), F32),
            pltpu.VMEM((2, 2, A_QB, LANES), F32),
            pltpu.VMEM((3, 2, A_HB, A_HW), F32),
            pltpu.VMEM((A_W, LANES), BF16),
            pltpu.VMEM((2, A_W, LANES), BF16),
        ],
        compiler_params=_cparams(("parallel", "parallel", "arbitrary")),
        name="a_attention",
    )(slopes, qkv, qkv, qkv)


def _flash_kernel(qt_ref, k_ref, vt_ref, o_ref, m_ref, acc_ref, s_ref, mc_ref, *, nh, dv, shared_kv):
    m_ref[...] = jnp.full(m_ref.shape, -jnp.inf, F32)
    acc_ref[...] = jnp.zeros(acc_ref.shape, F32)
    ones = jnp.ones((ONES_ROWS, TK), BF16)
    nchunk = SEQ // TK

    def chunk_rows(chunk):
        start = chunk * TK
        return pl.ds(start if isinstance(start, int) else pl.multiple_of(start, TK), TK)

    def scores(chunk, slot, h):
        rows = chunk_rows(chunk)
        k = k_ref[rows, :] if shared_kv else k_ref[rows, h * LANES:(h + 1) * LANES]
        st = jnp.dot(k, qt_ref[h * LANES:(h + 1) * LANES, :], preferred_element_type=F32)
        s_ref[slot, h] = st
        mc_ref[slot, h] = jnp.max(st, axis=0, keepdims=True)

    def update(chunk, slot, h):
        rows = chunk_rows(chunk)
        vt = vt_ref[:, rows] if shared_kv else vt_ref[h * dv:(h + 1) * dv, rows]
        m_prev = m_ref[h]
        m_new = jnp.maximum(m_prev, mc_ref[slot, h])
        alpha = jnp.exp2(m_prev - m_new)
        pt = jnp.exp2(s_ref[slot, h] - m_new).astype(BF16)
        pv = jnp.dot(jnp.concatenate([vt, ones], axis=0), pt, preferred_element_type=F32)
        acc_ref[h] = alpha * acc_ref[h] + pv
        m_ref[h] = m_new

    for h in range(nh):
        scores(0, 0, h)

    def chunks(j, last):
        for u in range(FLASH_UNROLL):
            for h in range(nh):
                if not (last and u == FLASH_UNROLL - 1):
                    scores(j + u + 1, (u + 1) % 2, h)
                update(j + u, u % 2, h)

    def body(jj, carry):
        chunks(FLASH_UNROLL * jj, False)
        return carry

    lax.fori_loop(0, nchunk // FLASH_UNROLL - 1, body, 0)
    chunks(nchunk - FLASH_UNROLL, True)

    outs = []
    for h in range(nh):
        acc = acc_ref[h]
        outs.append(acc[:dv, :] / acc[dv:dv + 1, :])
    ot = outs[0] if nh == 1 else jnp.concatenate(outs, axis=0)
    o_ref[...] = ot.T.astype(o_ref.dtype)


def _flash_call(kernel, grid, nh, dv, in_specs, out_spec, args, name):
    return pl.pallas_call(
        kernel,
        grid=grid,
        in_specs=in_specs,
        out_specs=out_spec,
        out_shape=jax.ShapeDtypeStruct((BATCH, SEQ, D_MODEL), BF16),
        scratch_shapes=[
            pltpu.VMEM((nh, 1, TQ), F32),
            pltpu.VMEM((nh, dv + ONES_ROWS, TQ), F32),
            pltpu.VMEM((2, nh, TK, TQ), F32),
            pltpu.VMEM((2, nh, 1, TQ), F32),
        ],
        compiler_params=_cparams(("parallel",) * (len(grid) - 1) + ("arbitrary",)),
        name=name,
    )(*args)


def b_attention(qt, k, vt):
    grp = B_HEADS // B_KV_HEADS
    nh = FLASH_HEADS
    kernel = functools.partial(_flash_kernel, nh=nh, dv=B_HEAD_DIM, shared_kv=True)
    in_specs = [
        pl.BlockSpec((None, nh * LANES, TQ), lambda b, kv, g, i: (b, kv * (grp // nh) + g, i)),
        pl.BlockSpec((None, SEQ, LANES), lambda b, kv, g, i: (b, 0, kv)),
        pl.BlockSpec((None, LANES, SEQ), lambda b, kv, g, i: (b, kv, 0)),
    ]
    out_spec = pl.BlockSpec((None, TQ, nh * LANES), lambda b, kv, g, i: (b, i, kv * (grp // nh) + g))
    grid = (BATCH, B_KV_HEADS, grp // nh, SEQ // TQ)
    return _flash_call(kernel, grid, nh, B_HEAD_DIM, in_specs, out_spec, (qt, k, vt), "b_attention")


def c_attention(qt, k, vt):
    nh = FLASH_HEADS
    kernel = functools.partial(_flash_kernel, nh=nh, dv=C_V, shared_kv=False)
    in_specs = [
        pl.BlockSpec((None, nh * LANES, TQ), lambda b, p, i: (b, p, i)),
        pl.BlockSpec((None, SEQ, nh * LANES), lambda b, p, i: (b, 0, p)),
        pl.BlockSpec((None, nh * C_V, SEQ), lambda b, p, i: (b, p, 0)),
    ]
    out_spec = pl.BlockSpec((None, TQ, nh * C_V), lambda b, p, i: (b, i, p))
    grid = (BATCH, C_HEADS // nh, SEQ // TQ)
    return _flash_call(kernel, grid, nh, C_V, in_specs, out_spec, (qt, k, vt), "c_attention")


def _rope_lanes(x, cos, sin_a, sin_b, shift):
    return (x * cos + pltpu.roll(x, LANES - shift, 1) * sin_a + pltpu.roll(x, shift, 1) * sin_b)


def _nt_dot(a, b):
    return lax.dot_general(a, b, (((1,), (1,)), ((), ())), preferred_element_type=F32)


def _swap_rows(x, start, half, groups):
    pieces = [x[:start]] if start else []
    for g0 in range(start, start + 2 * half * groups, 2 * half):
        pieces += [x[g0 + half:g0 + 2 * half], x[g0:g0 + half]]
    if start + 2 * half * groups < x.shape[0]:
        pieces.append(x[start + 2 * half * groups:])
    return jnp.concatenate(pieces, axis=0)


def _b_prep_kernel(h_ref, g_ref, wqt_ref, wk_ref, wvt_ref, qg_ref, kn_ref, cost_ref, sint_ref,
                   cos_ref, sa_ref, sb_ref, qt_ref, k_ref, vt_ref):
    xn = _rms(h_ref[...], g_ref[...]).astype(BF16)
    cos_t, sin_t, qgain = cost_ref[...], sint_ref[...], qg_ref[...]
    cos, sa, sb = cos_ref[...], sa_ref[...], sb_ref[...]
    half_rows = B_HEADS * LANES // 2

    def q_heads(qt_half, first_head):
        for hd in range(B_HEADS // 2):
            x = qt_half[hd * LANES:(hd + 1) * LANES]
            ms = jnp.mean(x * x, axis=0, keepdims=True)
            x = (x * lax.rsqrt(ms + NORM_EPS)) * qgain
            x = x * cos_t + _swap_rows(x, 0, B_HEAD_DIM // 4, 2) * sin_t
            out_rows = slice((first_head + hd) * LANES, (first_head + hd + 1) * LANES)
            qt_ref[out_rows, :] = x.astype(BF16)

    kk = jnp.dot(xn, wk_ref[...], preferred_element_type=F32)
    qt_a = _nt_dot(wqt_ref[:half_rows, :], xn)
    for j in range(B_KV_HEADS):
        cols = slice(j * LANES, (j + 1) * LANES)
        x = _rope_lanes(_rms(kk[:, cols], kn_ref[...]), cos, sa, sb, B_HEAD_DIM // 4)
        k_ref[:, cols] = x.astype(BF16)
    qt_b = _nt_dot(wqt_ref[half_rows:, :], xn)
    q_heads(qt_a, 0)
    vt = _nt_dot(wvt_ref[...], xn)
    q_heads(qt_b, B_HEADS // 2)
    vt_ref[...] = vt.astype(BF16)


def b_prep(h, gain, wq_t, wk, wv_t, qgain_t, knorm, tables_t, tables):
    t, d = h.shape
    nrow = SEQ // TM_ROW
    row = lambda i: (i, 0)
    fixed = lambda i: (0, 0)
    tab = lambda i: (i % nrow, 0)
    tab_t = lambda i: (0, i % nrow)
    colblk = lambda i: (i // nrow, 0, i % nrow)
    nq = B_HEADS * B_HEAD_DIM
    nkv = B_KV_HEADS * B_HEAD_DIM
    return pl.pallas_call(
        _b_prep_kernel,
        grid=(t // TM_ROW,),
        in_specs=[
            pl.BlockSpec((TM_ROW, d), row),
            pl.BlockSpec((1, d), fixed),
            pl.BlockSpec((nq, d), fixed),
            pl.BlockSpec((d, nkv), fixed),
            pl.BlockSpec((nkv, d), fixed),
            pl.BlockSpec((B_HEAD_DIM, TM_ROW), fixed),
            pl.BlockSpec((1, B_HEAD_DIM), fixed),
            pl.BlockSpec((LANES, TM_ROW), tab_t),
            pl.BlockSpec((LANES, TM_ROW), tab_t),
            pl.BlockSpec((TM_ROW, LANES), tab),
            pl.BlockSpec((TM_ROW, LANES), tab),
            pl.BlockSpec((TM_ROW, LANES), tab),
        ],
        out_specs=[
            pl.BlockSpec((None, nq, TM_ROW), colblk),
            pl.BlockSpec((TM_ROW, nkv), row),
            pl.BlockSpec((None, nkv, TM_ROW), colblk),
        ],
        out_shape=[
            jax.ShapeDtypeStruct((BATCH, nq, SEQ), BF16),
            jax.ShapeDtypeStruct((t, nkv), BF16),
            jax.ShapeDtypeStruct((BATCH, nkv, SEQ), BF16),
        ],
        compiler_params=_cparams(("parallel",)),
        name="b_prep",
    )(h, gain, wq_t, wk, wv_t, qgain_t, knorm, *tables_t, *tables)


def _c_prep_kernel(h_ref, g_ref, win_ref, qn_ref, kvn_ref, wuqt_ref, wuk_ref, wuvt_ref,
                   cost_ref, sint_ref, cos_ref, sa_ref, sb_ref, q_ref, k_ref, v_ref):
    xn = _rms(h_ref[...], g_ref[...]).astype(BF16)
    c = jnp.dot(xn, win_ref[...], preferred_element_type=F32)
    cq = _rms(c[:, :C_Q_LORA], qn_ref[...]).astype(BF16)
    ckv = _rms(c[:, C_Q_LORA:C_Q_LORA + C_KV_LORA], kvn_ref[...]).astype(BF16)
    cos, sa, sb = cos_ref[...], sa_ref[...], sb_ref[...]
    cos_t, sin_t = cost_ref[...], sint_ref[...]
    half_rows = C_HEADS * LANES // 2

    def q_heads(qt_half, first_head):
        for hd in range(C_HEADS // 2):
            x = qt_half[hd * LANES:(hd + 1) * LANES]
            x = x * cos_t + _swap_rows(x, C_NOPE, C_ROPE // 2, 1) * sin_t
            out_rows = slice((first_head + hd) * LANES, (first_head + hd + 1) * LANES)
            q_ref[out_rows, :] = x.astype(BF16)

    kn = jnp.dot(ckv, wuk_ref[...], preferred_element_type=F32)
    qt_a = _nt_dot(wuqt_ref[:half_rows, :], cq)
    kr = _rope_lanes(c[:, C_Q_LORA + C_KV_LORA:], cos, sa, sb, C_ROPE // 2)
    for hd in range(C_HEADS):
        cols = slice(hd * LANES, (hd + 1) * LANES)
        k_ref[:, cols] = (kn[:, cols] + kr).astype(BF16)
    qt_b = _nt_dot(wuqt_ref[half_rows:, :], cq)
    q_heads(qt_a, 0)
    vt = _nt_dot(wuvt_ref[...], ckv)
    q_heads(qt_b, C_HEADS // 2)
    v_ref[...] = vt.astype(BF16)


def c_prep(h, gain, win, qnorm, kvnorm, wuq_t, wuk, wuv_t, tables_t, tables):
    t, d = h.shape
    nrow = SEQ // TM_ROW
    row = lambda i: (i, 0)
    fixed = lambda i: (0, 0)
    tab = lambda i: (i % nrow, 0)
    tab_t = lambda i: (0, i % nrow)
    colblk = lambda i: (i // nrow, 0, i % nrow)
    hq = C_HEADS * LANES
    return pl.pallas_call(
        _c_prep_kernel,
        grid=(t // TM_ROW,),
        in_specs=[
            pl.BlockSpec((TM_ROW, d), row),
            pl.BlockSpec((1, d), fixed),
            pl.BlockSpec((d, C_IN_PAD), fixed),
            pl.BlockSpec((1, C_Q_LORA), fixed),
            pl.BlockSpec((1, C_KV_LORA), fixed),
            pl.BlockSpec((hq, C_Q_LORA), fixed),
            pl.BlockSpec((C_KV_LORA, hq), fixed),
            pl.BlockSpec((C_HEADS * C_V, C_KV_LORA), fixed),
            pl.BlockSpec((LANES, TM_ROW), tab_t),
            pl.BlockSpec((LANES, TM_ROW), tab_t),
            pl.BlockSpec((TM_ROW, LANES), tab),
            pl.BlockSpec((TM_ROW, LANES), tab),
            pl.BlockSpec((TM_ROW, LANES), tab),
        ],
        out_specs=[
            pl.BlockSpec((None, hq, TM_ROW), colblk),
            pl.BlockSpec((TM_ROW, hq), row),
            pl.BlockSpec((None, C_HEADS * C_V, TM_ROW), colblk),
        ],
        out_shape=[
            jax.ShapeDtypeStruct((BATCH, hq, SEQ), BF16),
            jax.ShapeDtypeStruct((t, hq), BF16),
            jax.ShapeDtypeStruct((BATCH, C_HEADS * C_V, SEQ), BF16),
        ],
        compiler_params=_cparams(("parallel",)),
        name="c_prep",
    )(h, gain, win, qnorm, kvnorm, wuq_t, wuk, wuv_t, *tables_t, *tables)


def _mix_ffn_kernel(y_ref, wo_ref, gmix_ref, h_ref, gpre_ref, wg_ref, wu_ref, wd_ref, gpost_ref,
                    o_ref):
    sub = TM_ROW // FFN_SPLIT
    parts = [slice(i * sub, (i + 1) * sub) for i in range(FFN_SPLIT)]
    z = [jnp.dot(y_ref[p, :], wo_ref[...], preferred_element_type=F32) for p in parts]
    h1, gate, up, down = [], [], [], []
    for i, p in enumerate(parts):
        h1.append(h_ref[p, :] + _rms(z[i], gmix_ref[...]))
        xn = _rms(h1[i], gpre_ref[...]).astype(BF16)
        gate.append(jnp.dot(xn, wg_ref[...], preferred_element_type=F32))
        up.append(jnp.dot(xn, wu_ref[...], preferred_element_type=F32))
    for i in range(FFN_SPLIT):
        act = (gate[i] * jax.nn.sigmoid(gate[i])) * up[i]
        down.append(jnp.dot(act.astype(BF16), wd_ref[...], preferred_element_type=F32))
    for i, p in enumerate(parts):
        o_ref[p, :] = h1[i] + _rms(down[i], gpost_ref[...])


def mix_ffn(y, wo, gmix, h, gpre, wg, wu, wd, layer, gpost):
    t, d = h.shape
    row = lambda i: (i, 0)
    fixed = lambda i: (0, 0)
    layer_blk = lambda i: (layer, 0, 0)
    once = pl.Buffered(1)
    return pl.pallas_call(
        _mix_ffn_kernel,
        grid=(t // TM_ROW,),
        in_specs=[
            pl.BlockSpec((TM_ROW, d), row),
            pl.BlockSpec((d, d), fixed, pipeline_mode=once),
            pl.BlockSpec((1, d), fixed),
            pl.BlockSpec((TM_ROW, d), row),
            pl.BlockSpec((1, d), fixed),
            pl.BlockSpec((None, d, D_FF), layer_blk, pipeline_mode=once),
            pl.BlockSpec((None, d, D_FF), layer_blk, pipeline_mode=once),
            pl.BlockSpec((None, D_FF, d), layer_blk, pipeline_mode=once),
            pl.BlockSpec((1, d), fixed),
        ],
        out_specs=pl.BlockSpec((TM_ROW, d), row),
        out_shape=jax.ShapeDtypeStruct((t, d), F32),
        compiler_params=_cparams(("parallel",)),
        name="mix_ffn",
    )(y, wo, gmix, h, gpre, wg, wu, wd, gpost)


def _angles(pos, n_freq, dim, theta):
    freqs = jnp.power(jnp.float32(theta), -jnp.arange(n_freq, dtype=F32) * 2.0 / dim)
    ang = freqs[:, None] * pos[None, :]
    return lax.optimization_barrier((jnp.cos(ang), jnp.sin(ang)))


def _b_tables():
    quarter = B_HEAD_DIM // 4
    rows = SEQ // GRID_W
    rc, rs = _angles(jnp.arange(rows, dtype=F32), quarter, B_HEAD_DIM // 2, B_ROPE_THETA)
    cc, cs = _angles(jnp.arange(GRID_W, dtype=F32), quarter, B_HEAD_DIM // 2, B_ROPE_THETA)
    by_row = lambda x: jnp.repeat(x, GRID_W, axis=1)
    by_col = lambda x: jnp.tile(x, (1, rows))
    rc, rs, cc, cs = by_row(rc), by_row(rs), by_col(cc), by_col(cs)
    zero = jnp.zeros_like(rs)
    cos_t = jnp.concatenate([rc, rc, cc, cc], axis=0)
    sin_t = jnp.concatenate([-rs, rs, -cs, cs], axis=0)
    sin_a = jnp.concatenate([-rs, zero, -cs, zero], axis=0)
    sin_b = jnp.concatenate([zero, rs, zero, cs], axis=0)
    scale = B_HEAD_DIM ** -0.5 * LOG2E
    return (cos_t * scale, sin_t * scale), (cos_t.T, sin_a.T, sin_b.T)


def _c_tables():
    half = C_ROPE // 2
    c, s = _angles(jnp.arange(SEQ, dtype=F32), half, C_ROPE, C_ROPE_THETA)
    one_lo = jnp.ones((C_NOPE, SEQ), F32)
    one_hi = jnp.ones((LANES - C_NOPE - C_ROPE, SEQ), F32)
    zero = jnp.zeros_like(s)
    cos_t = jnp.concatenate([one_lo, c, c, one_hi], axis=0)
    sin_t = jnp.concatenate([0 * one_lo, -s, s, 0 * one_hi], axis=0)
    sin_a = jnp.concatenate([0 * one_lo, -s, zero, 0 * one_hi], axis=0)
    sin_b = jnp.concatenate([0 * one_lo, zero, s, 0 * one_hi], axis=0)
    scale = (C_NOPE + C_ROPE) ** -0.5 * LOG2E
    return (cos_t * scale, sin_t * scale), (cos_t.T, sin_a.T, sin_b.T)


def _alibi_slopes():
    n = A_NG * A_HEADS
    return jnp.asarray(2.0 ** (-8.0 * np.arange(1, n + 1) / n), dtype=F32)


def _mixer_a(h, gain, wqkv_stack, layer):
    qkv = norm_matmul(h, gain, wqkv_stack, layer)
    return a_attention(qkv.reshape(BATCH, SEQ, A_QKV), _alibi_slopes())


def _mixer_b(h, gain, wqkv, qnorm, knorm):
    nq = B_HEADS * B_HEAD_DIM
    nkv = B_KV_HEADS * B_HEAD_DIM
    w = wqkv.astype(BF16)
    qgain_t = jnp.broadcast_to(qnorm[:, None], (B_HEAD_DIM, TM_ROW))
    tables_t, tables = _b_tables()
    qt, k, vt = b_prep(h, gain, w[:, :nq].T, w[:, nq:nq + nkv], w[:, nq + nkv:].T, qgain_t,
                       knorm[None, :], tables_t, tables)
    return b_attention(qt, k.reshape(BATCH, SEQ, -1), vt)


def _mixer_c(h, gain, win, qnorm, kvnorm, wuq, wukv):
    win_p = jnp.zeros((D_MODEL, C_IN_PAD), F32)
    win_p = win_p.at[:, :C_Q_LORA + C_KV_LORA].set(win[:, :C_Q_LORA + C_KV_LORA])
    kr0 = C_Q_LORA + C_KV_LORA + C_NOPE
    win_p = win_p.at[:, kr0:kr0 + C_ROPE].set(win[:, C_Q_LORA + C_KV_LORA:])
    wuq_p = jnp.pad(wuq.reshape(C_Q_LORA, C_HEADS, C_NOPE + C_ROPE),
                    ((0, 0), (0, 0), (0, LANES - C_NOPE - C_ROPE))).reshape(C_Q_LORA, C_HEADS * LANES)
    wukv3 = wukv.reshape(C_KV_LORA, C_HEADS, C_NOPE + C_V)
    wuk_p = jnp.pad(wukv3[:, :, :C_NOPE],
                    ((0, 0), (0, 0), (0, LANES - C_NOPE))).reshape(C_KV_LORA, C_HEADS * LANES)
    wuv = wukv3[:, :, C_NOPE:].reshape(C_KV_LORA, C_HEADS * C_V)
    tables_t, tables = _c_tables()
    qt, k, vt = c_prep(h, gain, win_p.astype(BF16), qnorm[None, :], kvnorm[None, :],
                       wuq_p.astype(BF16).T, wuk_p.astype(BF16), wuv.astype(BF16).T,
                       tables_t, tables)
    return c_attention(qt, k.reshape(BATCH, SEQ, -1), vt)


def kernel(x, norm_mix_pre, norm_mix_post, norm_ffn_pre, norm_ffn_post, ffn_wg, ffn_wu, ffn_wd,
           a_wqkv, a_wo, b_wqkv, b_qnorm, b_knorm, b_wo,
           c_win, c_qnorm, c_kvnorm, c_wuq, c_wukv, c_wo):
    h = x.reshape(BATCH * SEQ, D_MODEL)
    a_wqkv_b = a_wqkv.astype(BF16)
    wg_b, wu_b, wd_b = ffn_wg.astype(BF16), ffn_wu.astype(BF16), ffn_wd.astype(BF16)
    for i in range(DEPTH):
        kind = i % N_MIXERS
        j = i // N_MIXERS
        gpre = norm_mix_pre[i][None, :]
        if kind == 0:
            y, wo = _mixer_a(h, gpre, a_wqkv_b, j), a_wo[j]
        elif kind == 1:
            y, wo = _mixer_b(h, gpre, b_wqkv[j], b_qnorm[j], b_knorm[j]), b_wo[j]
        else:
            y, wo = _mixer_c(h, gpre, c_win[j], c_qnorm[j], c_kvnorm[j], c_wuq[j], c_wukv[j]), c_wo[j]
        h = mix_ffn(y.reshape(BATCH * SEQ, D_MODEL), wo.astype(BF16), norm_mix_post[i][None, :], h,
                    norm_ffn_pre[i][None, :], wg_b, wu_b, wd_b, i, norm_ffn_post[i][None, :])
    return h.reshape(BATCH, SEQ, D_MODEL)
```

```python
import functools

import numpy as np
import jax
import jax.numpy as jnp
from jax import lax
from jax.experimental import pallas as pl
from jax.experimental.pallas import tpu as pltpu

F32 = jnp.float32
BF16 = jnp.bfloat16

D_MODEL = 1024
BATCH = 2
SEQ = 8192
DEPTH = 4
N_MIXERS = 3
GRID_W = 64
D_FF = 2816
NORM_EPS = 1e-6
NEG_INF = -1e30

A_GROUPS = ((128, 1), (512, 4), (2048, 16))
A_NG = 3
A_HEAD_DIM = 64
A_HEADS = 16
A_QKV = 3 * A_NG * A_HEADS * A_HEAD_DIM
A_RADIUS = 64

B_HEAD_DIM = 128
B_HEADS = 8
B_KV_HEADS = 2
B_ROPE_THETA = 10000.0
B_QKV = (B_HEADS + 2 * B_KV_HEADS) * B_HEAD_DIM

C_HEADS = 16
C_Q_LORA = 384
C_KV_LORA = 256
C_NOPE = 64
C_ROPE = 32
C_V = 64
C_ROPE_THETA = 10000.0
C_IN_PAD = 768

LANES = 128
ONES_ROWS = 16
C_ONES_ROWS = LANES - C_V
LOG2E = 1.4426950408889634
VMEM_LIMIT = 56 * 1024 * 1024

TM_PROJ = 256
TN_PROJ = 1024
TM_ROW = 512
FFN_SPLIT = 2
TQ = 512
TK = 512
FLASH_UNROLL = 4
FLASH_HEADS = 4
A_QB = 512
A_W = A_QB + 2 * A_RADIUS
A_HB = 128
A_HW = A_HB + 2 * A_RADIUS
A_UNROLL = 2


def _cparams(sem):
    return pltpu.CompilerParams(dimension_semantics=sem, vmem_limit_bytes=VMEM_LIMIT)


def _rms(x, gain):
    ms = jnp.mean(x * x, axis=-1, keepdims=True)
    return (x * lax.rsqrt(ms + NORM_EPS)) * gain


def _norm_matmul_kernel(x_ref, g_ref, w_ref, o_ref):
    xn = _rms(x_ref[...], g_ref[...]).astype(BF16)
    for c in range(o_ref.shape[1] // TN_PROJ):
        cols = slice(c * TN_PROJ, (c + 1) * TN_PROJ)
        o_ref[:, cols] = jnp.dot(xn, w_ref[:, cols], preferred_element_type=F32)


def norm_matmul(h, gain, w_stack, layer):
    t, d = h.shape
    n = w_stack.shape[2]
    return pl.pallas_call(
        _norm_matmul_kernel,
        grid=(t // TM_PROJ,),
        in_specs=[
            pl.BlockSpec((TM_PROJ, d), lambda i: (i, 0)),
            pl.BlockSpec((1, d), lambda i: (0, 0)),
            pl.BlockSpec((None, d, n), lambda i: (layer, 0, 0), pipeline_mode=pl.Buffered(1)),
        ],
        out_specs=pl.BlockSpec((TM_PROJ, n), lambda i: (i, 0)),
        out_shape=jax.ShapeDtypeStruct((t, n), F32),
        compiler_params=_cparams(("parallel",)),
        name="a_qkv_proj",
    )(h, gain, w_stack)


def _a_attn_kernel(slopes_ref, q_ref, k_ref, v_ref, o_ref, m_ref, l_ref, acc_ref,
                   s_ref, mf_ref, bias_ref, kbuf_ref, vbuf_ref):
    hp = pl.program_id(1)
    g = pl.program_id(2)
    lane = lax.broadcasted_iota(jnp.int32, (A_QB, LANES), 1)
    lo = lane < A_HEAD_DIM
    head_lanes = (lo, jnp.logical_not(lo))
    rel = (lax.broadcasted_iota(jnp.int32, (A_HB, A_HW), 1)
           - lax.broadcasted_iota(jnp.int32, (A_HB, A_HW), 0))

    def group_body(step, gi, d):
        first = step == 0
        final = step == A_NG - 1
        cls_len = SEQ // d
        nblk = cls_len // A_QB
        nit = d * nblk
        win = min(A_W, cls_len)
        lo_w = lax.broadcasted_iota(jnp.int32, (win, LANES), 1) < A_HEAD_DIM
        head_lanes_w = (lo_w, jnp.logical_not(lo_w))

        for oi in range(3):
            dist = jnp.abs(rel - oi * A_RADIUS)
            for h2 in range(2):
                slope = slopes_ref[gi * A_HEADS + hp * 2 + h2] * (d * LOG2E)
                bias_ref[oi, h2] = jnp.where(dist <= A_RADIUS, -slope * dist.astype(F32), NEG_INF)

        def geometry(it):
            r = it // nblk
            i0 = (it % nblk) * A_QB
            if isinstance(it, int):
                kstart = min(max(i0 - A_RADIUS, 0), cls_len - win)
            else:
                kstart = jnp.clip(i0 - A_RADIUS, 0, cls_len - win)
            oi = (i0 - kstart) // A_RADIUS
            if d == 1 and isinstance(it, int):
                q_rows = pl.ds(i0, A_QB)
                k_rows = pl.ds(kstart, win)
            elif d == 1:
                q_rows = pl.ds(pl.multiple_of(i0, A_QB), A_QB)
                k_rows = pl.ds(pl.multiple_of(kstart, A_RADIUS), win)
            else:
                q_rows = pl.ds(r + i0 * d, A_QB, stride=d)
                k_rows = pl.ds(r + kstart * d, win, stride=d)
            return q_rows, k_rows, oi

        def half_window(t, oi):
            lo_row = t * A_HB + (oi - 1) * A_RADIUS
            if isinstance(oi, int):
                start = min(max(lo_row, 0), win - A_HW)
            else:
                start = pl.multiple_of(jnp.clip(lo_row, 0, win - A_HW), A_RADIUS)
            return pl.ds(start, A_HW), oi + t * (A_HB // A_RADIUS) - start // A_RADIUS

        def load_qk(it):
            q_rows, k_rows, oi = geometry(it)
            kbuf_ref[:win, :] = k_ref[k_rows, :].astype(BF16)
            return (q_ref[q_rows, :] * (A_HEAD_DIM ** -0.5 * LOG2E)).astype(BF16), oi

        def scores(qk, slot, h2):
            q, oi = qk
            qm = jnp.where(head_lanes[h2], q, jnp.zeros_like(q))
            for t in range(A_QB // A_HB):
                rows = slice(t * A_HB, (t + 1) * A_HB)
                k_rows, boff = half_window(t, oi)
                s = lax.dot_general(qm[rows], kbuf_ref[k_rows, :], (((1,), (1,)), ((), ())),
                                    preferred_element_type=F32)
                s = s + bias_ref[boff, h2]
                s_ref[slot, h2, rows] = s
                mf_ref[slot, h2, rows] = jnp.broadcast_to(jnp.max(s, axis=1, keepdims=True),
                                                           (A_HB, LANES))

        def load_v(k_rows):
            v = v_ref[k_rows, :]
            for h2 in range(2):
                vbuf_ref[h2, :win, :] = jnp.where(head_lanes_w[h2], v, 1.0).astype(BF16)

        def weighted_values(oi, slot, h2):
            outs = []
            for t in range(A_QB // A_HB):
                rows = slice(t * A_HB, (t + 1) * A_HB)
                k_rows, _ = half_window(t, oi)
                mfull = mf_ref[slot, h2, rows]
                p = jnp.exp2(s_ref[slot, h2, rows] - jnp.concatenate([mfull] * (A_HW // LANES), axis=1))
                outs.append(jnp.dot(p.astype(BF16), vbuf_ref[h2, k_rows, :], preferred_element_type=F32))
            return jnp.concatenate(outs, axis=0)

        def merge(q_rows, slot, outs):
            m_blk = jnp.where(lo, mf_ref[slot, 0], mf_ref[slot, 1])
            pv_blk = jnp.where(lo, outs[0], outs[1])
            l_blk = pltpu.roll(jnp.where(lo, outs[1], outs[0]), A_HEAD_DIM, 1)
            if first:
                m_ref[q_rows, :] = m_blk
                l_ref[q_rows, :] = l_blk
                acc_ref[q_rows, :] = pv_blk
            else:
                m_old = m_ref[q_rows, :]
                m_new = jnp.maximum(m_old, m_blk)
                a_old = jnp.exp2(m_old - m_new)
                a_blk = jnp.exp2(m_blk - m_new)
                l_new = a_old * l_ref[q_rows, :] + a_blk * l_blk
                acc_new = a_old * acc_ref[q_rows, :] + a_blk * pv_blk
                if final:
                    acc_ref[q_rows, :] = acc_new / l_new
                else:
                    m_ref[q_rows, :] = m_new
                    l_ref[q_rows, :] = l_new
                    acc_ref[q_rows, :] = acc_new

        qk0 = load_qk(0)
        for h2 in range(2):
            scores(qk0, 0, h2)

        def blocks(it0, last):
            for u in range(A_UNROLL):
                has_next = not (last and u == A_UNROLL - 1)
                qk = load_qk(it0 + u + 1) if has_next else None
                q_rows, k_rows, oi = geometry(it0 + u)
                load_v(k_rows)
                outs = []
                for h2 in range(2):
                    if has_next:
                        scores(qk, (u + 1) % 2, h2)
                    outs.append(weighted_values(oi, u % 2, h2))
                merge(q_rows, u % 2, outs)

        def body(jj, carry):
            blocks(A_UNROLL * jj, False)
            return carry

        lax.fori_loop(0, nit // A_UNROLL - 1, body, 0)
        blocks(nit - A_UNROLL, True)
        if final:
            o_ref[...] = acc_ref[...].astype(o_ref.dtype)

    for step in range(A_NG):
        gi = A_NG - 1 - step
        pl.when(g == step)(functools.partial(group_body, step, gi, A_GROUPS[gi][1]))


def a_attention(qkv, slopes):
    npairs = A_HEADS // 2

    def col(which):
        return lambda b, hp, g: (b, 0, (which * A_NG + (A_NG - 1 - g)) * npairs + hp)

    return pl.pallas_call(
        _a_attn_kernel,
        grid=(BATCH, npairs, A_NG),
        in_specs=[
            pl.BlockSpec(memory_space=pltpu.SMEM),
            pl.BlockSpec((None, SEQ, LANES), col(0)),
            pl.BlockSpec((None, SEQ, LANES), col(1)),
            pl.BlockSpec((None, SEQ, LANES), col(2)),
        ],
        out_specs=pl.BlockSpec((None, SEQ, LANES), lambda b, hp, g: (b, 0, hp)),
        out_shape=jax.ShapeDtypeStruct((BATCH, SEQ, D_MODEL), BF16),
        scratch_shapes=[
            pltpu.VMEM((SEQ, LANES), F32),
            pltpu.VMEM((SEQ, LANES), F32),
            pltpu.VMEM((SEQ, LANES), F32),
            pltpu.VMEM((2, 2, A_QB, A_HW), F32),
            pltpu.VMEM((2, 2, A_QB, LANES), F32),
            pltpu.VMEM((3, 2, A_HB, A_HW), F32),
            pltpu.VMEM((A_W, LANES), BF16),
            pltpu.VMEM((2, A_W, LANES), BF16),
        ],
        compiler_params=_cparams(("parallel", "parallel", "arbitrary")),
        name="a_attention",
    )(slopes, qkv, qkv, qkv)


def _flash_kernel(qt_ref, k_ref, vt_ref, o_ref, m_ref, acc_ref, s_ref, mc_ref, *, nh, dv, shared_kv):
    m_ref[...] = jnp.full(m_ref.shape, -jnp.inf, F32)
    acc_ref[...] = jnp.zeros(acc_ref.shape, F32)
    ones = jnp.ones((acc_ref.shape[1] - dv, TK), BF16)
    nchunk = SEQ // TK

    def chunk_rows(chunk):
        start = chunk * TK
        return pl.ds(start if isinstance(start, int) else pl.multiple_of(start, TK), TK)

    def scores(chunk, slot, h):
        rows = chunk_rows(chunk)
        k = k_ref[rows, :] if shared_kv else k_ref[rows, h * LANES:(h + 1) * LANES]
        st = jnp.dot(k, qt_ref[h * LANES:(h + 1) * LANES, :], preferred_element_type=F32)
        s_ref[slot, h] = st
        mc_ref[slot, h] = jnp.max(st, axis=0, keepdims=True)

    def update(chunk, slot, h):
        rows = chunk_rows(chunk)
        vt = vt_ref[:, rows] if shared_kv else vt_ref[h * dv:(h + 1) * dv, rows]
        m_prev = m_ref[h]
        m_new = jnp.maximum(m_prev, mc_ref[slot, h])
        alpha = jnp.exp2(m_prev - m_new)
        pt = jnp.exp2(s_ref[slot, h] - m_new).astype(BF16)
        pv = jnp.dot(jnp.concatenate([vt, ones], axis=0), pt, preferred_element_type=F32)
        acc_ref[h] = alpha * acc_ref[h] + pv
        m_ref[h] = m_new

    for h in range(nh):
        scores(0, 0, h)

    def chunks(j, last):
        for u in range(FLASH_UNROLL):
            for h in range(nh):
                if not (last and u == FLASH_UNROLL - 1):
                    scores(j + u + 1, (u + 1) % 2, h)
                update(j + u, u % 2, h)

    def body(jj, carry):
        chunks(FLASH_UNROLL * jj, False)
        return carry

    lax.fori_loop(0, nchunk // FLASH_UNROLL - 1, body, 0)
    chunks(nchunk - FLASH_UNROLL, True)

    outs = []
    for h in range(nh):
        acc = acc_ref[h]
        outs.append(acc[:dv, :] / acc[dv:dv + 1, :])
    ot = outs[0] if nh == 1 else jnp.concatenate(outs, axis=0)
    o_ref[...] = ot.T.astype(o_ref.dtype)


def _flash_call(kernel, grid, nh, acc_rows, in_specs, out_spec, args, name):
    return pl.pallas_call(
        kernel,
        grid=grid,
        in_specs=in_specs,
        out_specs=out_spec,
        out_shape=jax.ShapeDtypeStruct((BATCH, SEQ, D_MODEL), BF16),
        scratch_shapes=[
            pltpu.VMEM((nh, 1, TQ), F32),
            pltpu.VMEM((nh, acc_rows, TQ), F32),
            pltpu.VMEM((2, nh, TK, TQ), F32),
            pltpu.VMEM((2, nh, 1, TQ), F32),
        ],
        compiler_params=_cparams(("parallel",) * (len(grid) - 1) + ("arbitrary",)),
        name=name,
    )(*args)


def b_attention(qt, k, vt):
    grp = B_HEADS // B_KV_HEADS
    nh = FLASH_HEADS
    kernel = functools.partial(_flash_kernel, nh=nh, dv=B_HEAD_DIM, shared_kv=True)
    in_specs = [
        pl.BlockSpec((None, nh * LANES, TQ), lambda b, kv, g, i: (b, kv * (grp // nh) + g, i)),
        pl.BlockSpec((None, SEQ, LANES), lambda b, kv, g, i: (b, 0, kv)),
        pl.BlockSpec((None, LANES, SEQ), lambda b, kv, g, i: (b, kv, 0)),
    ]
    out_spec = pl.BlockSpec((None, TQ, nh * LANES), lambda b, kv, g, i: (b, i, kv * (grp // nh) + g))
    grid = (BATCH, B_KV_HEADS, grp // nh, SEQ // TQ)
    return _flash_call(kernel, grid, nh, B_HEAD_DIM + ONES_ROWS, in_specs, out_spec, (qt, k, vt),
                       "b_attention")


def c_attention(qt, k, vt):
    nh = FLASH_HEADS
    kernel = functools.partial(_flash_kernel, nh=nh, dv=C_V, shared_kv=False)
    in_specs = [
        pl.BlockSpec((None, nh * LANES, TQ), lambda b, p, i: (b, p, i)),
        pl.BlockSpec((None, SEQ, nh * LANES), lambda b, p, i: (b, 0, p)),
        pl.BlockSpec((None, nh * C_V, SEQ), lambda b, p, i: (b, p, 0)),
    ]
    out_spec = pl.BlockSpec((None, TQ, nh * C_V), lambda b, p, i: (b, i, p))
    grid = (BATCH, C_HEADS // nh, SEQ // TQ)
    return _flash_call(kernel, grid, nh, C_V + C_ONES_ROWS, in_specs, out_spec, (qt, k, vt),
                       "c_attention")


def _rope_lanes(x, cos, sin_a, sin_b, shift):
    return (x * cos + pltpu.roll(x, LANES - shift, 1) * sin_a + pltpu.roll(x, shift, 1) * sin_b)


def _nt_dot(a, b):
    return lax.dot_general(a, b, (((1,), (1,)), ((), ())), preferred_element_type=F32)


def _swap_rows(x, start, half, groups):
    pieces = [x[:start]] if start else []
    for g0 in range(start, start + 2 * half * groups, 2 * half):
        pieces += [x[g0 + half:g0 + 2 * half], x[g0:g0 + half]]
    if start + 2 * half * groups < x.shape[0]:
        pieces.append(x[start + 2 * half * groups:])
    return jnp.concatenate(pieces, axis=0)


def _b_prep_kernel(h_ref, g_ref, wqt_ref, wk_ref, wvt_ref, qg_ref, kn_ref, cost_ref, sint_ref,
                   cos_ref, sa_ref, sb_ref, qt_ref, k_ref, vt_ref):
    xn = _rms(h_ref[...], g_ref[...]).astype(BF16)
    cos_t, sin_t, qgain = cost_ref[...], sint_ref[...], qg_ref[...]
    cos, sa, sb = cos_ref[...], sa_ref[...], sb_ref[...]
    half_rows = B_HEADS * LANES // 2

    def q_heads(qt_half, first_head):
        for hd in range(B_HEADS // 2):
            x = qt_half[hd * LANES:(hd + 1) * LANES]
            ms = jnp.mean(x * x, axis=0, keepdims=True)
            x = (x * lax.rsqrt(ms + NORM_EPS)) * qgain
            x = x * cos_t + _swap_rows(x, 0, B_HEAD_DIM // 4, 2) * sin_t
            out_rows = slice((first_head + hd) * LANES, (first_head + hd + 1) * LANES)
            qt_ref[out_rows, :] = x.astype(BF16)

    kk = jnp.dot(xn, wk_ref[...], preferred_element_type=F32)
    qt_a = _nt_dot(wqt_ref[:half_rows, :], xn)
    for j in range(B_KV_HEADS):
        cols = slice(j * LANES, (j + 1) * LANES)
        x = _rope_lanes(_rms(kk[:, cols], kn_ref[...]), cos, sa, sb, B_HEAD_DIM // 4)
        k_ref[:, cols] = x.astype(BF16)
    qt_b = _nt_dot(wqt_ref[half_rows:, :], xn)
    q_heads(qt_a, 0)
    vt = _nt_dot(wvt_ref[...], xn)
    q_heads(qt_b, B_HEADS // 2)
    vt_ref[...] = vt.astype(BF16)


def b_prep(h, gain, wq_t, wk, wv_t, qgain_t, knorm, tables_t, tables):
    t, d = h.shape
    nrow = SEQ // TM_ROW
    row = lambda i: (i, 0)
    fixed = lambda i: (0, 0)
    tab = lambda i: (i % nrow, 0)
    tab_t = lambda i: (0, i % nrow)
    colblk = lambda i: (i // nrow, 0, i % nrow)
    nq = B_HEADS * B_HEAD_DIM
    nkv = B_KV_HEADS * B_HEAD_DIM
    return pl.pallas_call(
        _b_prep_kernel,
        grid=(t // TM_ROW,),
        in_specs=[
            pl.BlockSpec((TM_ROW, d), row),
            pl.BlockSpec((1, d), fixed),
            pl.BlockSpec((nq, d), fixed),
            pl.BlockSpec((d, nkv), fixed),
            pl.BlockSpec((nkv, d), fixed),
            pl.BlockSpec((B_HEAD_DIM, TM_ROW), fixed),
            pl.BlockSpec((1, B_HEAD_DIM), fixed),
            pl.BlockSpec((LANES, TM_ROW), tab_t),
            pl.BlockSpec((LANES, TM_ROW), tab_t),
            pl.BlockSpec((TM_ROW, LANES), tab),
            pl.BlockSpec((TM_ROW, LANES), tab),
            pl.BlockSpec((TM_ROW, LANES), tab),
        ],
        out_specs=[
            pl.BlockSpec((None, nq, TM_ROW), colblk),
            pl.BlockSpec((TM_ROW, nkv), row),
            pl.BlockSpec((None, nkv, TM_ROW), colblk),
        ],
        out_shape=[
            jax.ShapeDtypeStruct((BATCH, nq, SEQ), BF16),
            jax.ShapeDtypeStruct((t, nkv), BF16),
            jax.ShapeDtypeStruct((BATCH, nkv, SEQ), BF16),
        ],
        compiler_params=_cparams(("parallel",)),
        name="b_prep",
    )(h, gain, wq_t, wk, wv_t, qgain_t, knorm, *tables_t, *tables)


def _c_prep_kernel(h_ref, g_ref, win_ref, qn_ref, kvn_ref, wuqt_ref, wuk_ref, wuvt_ref,
                   cost_ref, sint_ref, cos_ref, sa_ref, sb_ref, q_ref, k_ref, v_ref):
    xn = _rms(h_ref[...], g_ref[...]).astype(BF16)
    c = jnp.dot(xn, win_ref[...], preferred_element_type=F32)
    cq = _rms(c[:, :C_Q_LORA], qn_ref[...]).astype(BF16)
    ckv = _rms(c[:, C_Q_LORA:C_Q_LORA + C_KV_LORA], kvn_ref[...]).astype(BF16)
    cos, sa, sb = cos_ref[...], sa_ref[...], sb_ref[...]
    cos_t, sin_t = cost_ref[...], sint_ref[...]
    half_rows = C_HEADS * LANES // 2

    def q_heads(qt_half, first_head):
        for hd in range(C_HEADS // 2):
            x = qt_half[hd * LANES:(hd + 1) * LANES]
            x = x * cos_t + _swap_rows(x, C_NOPE, C_ROPE // 2, 1) * sin_t
            out_rows = slice((first_head + hd) * LANES, (first_head + hd + 1) * LANES)
            q_ref[out_rows, :] = x.astype(BF16)

    kn = jnp.dot(ckv, wuk_ref[...], preferred_element_type=F32)
    qt_a = _nt_dot(wuqt_ref[:half_rows, :], cq)
    kr = _rope_lanes(c[:, C_Q_LORA + C_KV_LORA:], cos, sa, sb, C_ROPE // 2)
    for hd in range(C_HEADS):
        cols = slice(hd * LANES, (hd + 1) * LANES)
        k_ref[:, cols] = (kn[:, cols] + kr).astype(BF16)
    qt_b = _nt_dot(wuqt_ref[half_rows:, :], cq)
    q_heads(qt_a, 0)
    vt = _nt_dot(wuvt_ref[...], ckv)
    q_heads(qt_b, C_HEADS // 2)
    v_ref[...] = vt.astype(BF16)


def c_prep(h, gain, win, qnorm, kvnorm, wuq_t, wuk, wuv_t, tables_t, tables):
    t, d = h.shape
    nrow = SEQ // TM_ROW
    row = lambda i: (i, 0)
    fixed = lambda i: (0, 0)
    tab = lambda i: (i % nrow, 0)
    tab_t = lambda i: (0, i % nrow)
    colblk = lambda i: (i // nrow, 0, i % nrow)
    hq = C_HEADS * LANES
    return pl.pallas_call(
        _c_prep_kernel,
        grid=(t // TM_ROW,),
        in_specs=[
            pl.BlockSpec((TM_ROW, d), row),
            pl.BlockSpec((1, d), fixed),
            pl.BlockSpec((d, C_IN_PAD), fixed),
            pl.BlockSpec((1, C_Q_LORA), fixed),
            pl.BlockSpec((1, C_KV_LORA), fixed),
            pl.BlockSpec((hq, C_Q_LORA), fixed),
            pl.BlockSpec((C_KV_LORA, hq), fixed),
            pl.BlockSpec((C_HEADS * C_V, C_KV_LORA), fixed),
            pl.BlockSpec((LANES, TM_ROW), tab_t),
            pl.BlockSpec((LANES, TM_ROW), tab_t),
            pl.BlockSpec((TM_ROW, LANES), tab),
            pl.BlockSpec((TM_ROW, LANES), tab),
            pl.BlockSpec((TM_ROW, LANES), tab),
        ],
        out_specs=[
            pl.BlockSpec((None, hq, TM_ROW), colblk),
            pl.BlockSpec((TM_ROW, hq), row),
            pl.BlockSpec((None, C_HEADS * C_V, TM_ROW), colblk),
        ],
        out_shape=[
            jax.ShapeDtypeStruct((BATCH, hq, SEQ), BF16),
            jax.ShapeDtypeStruct((t, hq), BF16),
            jax.ShapeDtypeStruct((BATCH, C_HEADS * C_V, SEQ), BF16),
        ],
        compiler_params=_cparams(("parallel",)),
        name="c_prep",
    )(h, gain, win, qnorm, kvnorm, wuq_t, wuk, wuv_t, *tables_t, *tables)


def _mix_ffn_kernel(y_ref, wo_ref, gmix_ref, h_ref, gpre_ref, wg_ref, wu_ref, wd_ref, gpost_ref,
                    o_ref):
    sub = TM_ROW // FFN_SPLIT
    parts = [slice(i * sub, (i + 1) * sub) for i in range(FFN_SPLIT)]
    z = [jnp.dot(y_ref[p, :], wo_ref[...], preferred_element_type=F32) for p in parts]
    h1, gate, up, down = [], [], [], []
    for i, p in enumerate(parts):
        h1.append(h_ref[p, :] + _rms(z[i], gmix_ref[...]))
        xn = _rms(h1[i], gpre_ref[...]).astype(BF16)
        gate.append(jnp.dot(xn, wg_ref[...], preferred_element_type=F32))
        up.append(jnp.dot(xn, wu_ref[...], preferred_element_type=F32))
    for i in range(FFN_SPLIT):
        act = (gate[i] * jax.nn.sigmoid(gate[i])) * up[i]
        down.append(jnp.dot(act.astype(BF16), wd_ref[...], preferred_element_type=F32))
    for i, p in enumerate(parts):
        o_ref[p, :] = h1[i] + _rms(down[i], gpost_ref[...])


def mix_ffn(y, wo, gmix, h, gpre, wg, wu, wd, layer, gpost):
    t, d = h.shape
    row = lambda i: (i, 0)
    fixed = lambda i: (0, 0)
    layer_blk = lambda i: (layer, 0, 0)
    once = pl.Buffered(1)
    return pl.pallas_call(
        _mix_ffn_kernel,
        grid=(t // TM_ROW,),
        in_specs=[
            pl.BlockSpec((TM_ROW, d), row),
            pl.BlockSpec((d, d), fixed, pipeline_mode=once),
            pl.BlockSpec((1, d), fixed),
            pl.BlockSpec((TM_ROW, d), row),
            pl.BlockSpec((1, d), fixed),
            pl.BlockSpec((None, d, D_FF), layer_blk, pipeline_mode=once),
            pl.BlockSpec((None, d, D_FF), layer_blk, pipeline_mode=once),
            pl.BlockSpec((None, D_FF, d), layer_blk, pipeline_mode=once),
            pl.BlockSpec((1, d), fixed),
        ],
        out_specs=pl.BlockSpec((TM_ROW, d), row),
        out_shape=jax.ShapeDtypeStruct((t, d), F32),
        compiler_params=_cparams(("parallel",)),
        name="mix_ffn",
    )(y, wo, gmix, h, gpre, wg, wu, wd, gpost)


def _angles(pos, n_freq, dim, theta):
    freqs = jnp.power(jnp.float32(theta), -jnp.arange(n_freq, dtype=F32) * 2.0 / dim)
    ang = freqs[:, None] * pos[None, :]
    return lax.optimization_barrier((jnp.cos(ang), jnp.sin(ang)))


def _b_tables():
    quarter = B_HEAD_DIM // 4
    rows = SEQ // GRID_W
    rc, rs = _angles(jnp.arange(rows, dtype=F32), quarter, B_HEAD_DIM // 2, B_ROPE_THETA)
    cc, cs = _angles(jnp.arange(GRID_W, dtype=F32), quarter, B_HEAD_DIM // 2, B_ROPE_THETA)
    by_row = lambda x: jnp.repeat(x, GRID_W, axis=1)
    by_col = lambda x: jnp.tile(x, (1, rows))
    rc, rs, cc, cs = by_row(rc), by_row(rs), by_col(cc), by_col(cs)
    zero = jnp.zeros_like(rs)
    cos_t = jnp.concatenate([rc, rc, cc, cc], axis=0)
    sin_t = jnp.concatenate([-rs, rs, -cs, cs], axis=0)
    sin_a = jnp.concatenate([-rs, zero, -cs, zero], axis=0)
    sin_b = jnp.concatenate([zero, rs, zero, cs], axis=0)
    scale = B_HEAD_DIM ** -0.5 * LOG2E
    return (cos_t * scale, sin_t * scale), (cos_t.T, sin_a.T, sin_b.T)


def _c_tables():
    half = C_ROPE // 2
    c, s = _angles(jnp.arange(SEQ, dtype=F32), half, C_ROPE, C_ROPE_THETA)
    one_lo = jnp.ones((C_NOPE, SEQ), F32)
    one_hi = jnp.ones((LANES - C_NOPE - C_ROPE, SEQ), F32)
    zero = jnp.zeros_like(s)
    cos_t = jnp.concatenate([one_lo, c, c, one_hi], axis=0)
    sin_t = jnp.concatenate([0 * one_lo, -s, s, 0 * one_hi], axis=0)
    sin_a = jnp.concatenate([0 * one_lo, -s, zero, 0 * one_hi], axis=0)
    sin_b = jnp.concatenate([0 * one_lo, zero, s, 0 * one_hi], axis=0)
    scale = (C_NOPE + C_ROPE) ** -0.5 * LOG2E
    return (cos_t * scale, sin_t * scale), (cos_t.T, sin_a.T, sin_b.T)


def _alibi_slopes():
    n = A_NG * A_HEADS
    return jnp.asarray(2.0 ** (-8.0 * np.arange(1, n + 1) / n), dtype=F32)


def _mixer_a(h, gain, wqkv_stack, layer):
    qkv = norm_matmul(h, gain, wqkv_stack, layer)
    return a_attention(qkv.reshape(BATCH, SEQ, A_QKV), _alibi_slopes())


def _mixer_b(h, gain, wqkv, qnorm, knorm):
    nq = B_HEADS * B_HEAD_DIM
    nkv = B_KV_HEADS * B_HEAD_DIM
    w = wqkv.astype(BF16)
    qgain_t = jnp.broadcast_to(qnorm[:, None], (B_HEAD_DIM, TM_ROW))
    tables_t, tables = _b_tables()
    qt, k, vt = b_prep(h, gain, w[:, :nq].T, w[:, nq:nq + nkv], w[:, nq + nkv:].T, qgain_t,
                       knorm[None, :], tables_t, tables)
    return b_attention(qt, k.reshape(BATCH, SEQ, -1), vt)


def _mixer_c(h, gain, win, qnorm, kvnorm, wuq, wukv):
    win_p = jnp.zeros((D_MODEL, C_IN_PAD), F32)
    win_p = win_p.at[:, :C_Q_LORA + C_KV_LORA].set(win[:, :C_Q_LORA + C_KV_LORA])
    kr0 = C_Q_LORA + C_KV_LORA + C_NOPE
    win_p = win_p.at[:, kr0:kr0 + C_ROPE].set(win[:, C_Q_LORA + C_KV_LORA:])
    wuq_p = jnp.pad(wuq.reshape(C_Q_LORA, C_HEADS, C_NOPE + C_ROPE),
                    ((0, 0), (0, 0), (0, LANES - C_NOPE - C_ROPE))).reshape(C_Q_LORA, C_HEADS * LANES)
    wukv3 = wukv.reshape(C_KV_LORA, C_HEADS, C_NOPE + C_V)
    wuk_p = jnp.pad(wukv3[:, :, :C_NOPE],
                    ((0, 0), (0, 0), (0, LANES - C_NOPE))).reshape(C_KV_LORA, C_HEADS * LANES)
    wuv = wukv3[:, :, C_NOPE:].reshape(C_KV_LORA, C_HEADS * C_V)
    tables_t, tables = _c_tables()
    qt, k, vt = c_prep(h, gain, win_p.astype(BF16), qnorm[None, :], kvnorm[None, :],
                       wuq_p.astype(BF16).T, wuk_p.astype(BF16), wuv.astype(BF16).T,
                       tables_t, tables)
    return c_attention(qt, k.reshape(BATCH, SEQ, -1), vt)


def kernel(x, norm_mix_pre, norm_mix_post, norm_ffn_pre, norm_ffn_post, ffn_wg, ffn_wu, ffn_wd,
           a_wqkv, a_wo, b_wqkv, b_qnorm, b_knorm, b_wo,
           c_win, c_qnorm, c_kvnorm, c_wuq, c_wukv, c_wo):
    h = x.reshape(BATCH * SEQ, D_MODEL)
    a_wqkv_b = a_wqkv.astype(BF16)
    wg_b, wu_b, wd_b = ffn_wg.astype(BF16), ffn_wu.astype(BF16), ffn_wd.astype(BF16)
    for i in range(DEPTH):
        kind = i % N_MIXERS
        j = i // N_MIXERS
        gpre = norm_mix_pre[i][None, :]
        if kind == 0:
            y, wo = _mixer_a(h, gpre, a_wqkv_b, j), a_wo[j]
        elif kind == 1:
            y, wo = _mixer_b(h, gpre, b_wqkv[j], b_qnorm[j], b_knorm[j]), b_wo[j]
        else:
            y, wo = _mixer_c(h, gpre, c_win[j], c_qnorm[j], c_kvnorm[j], c_wuq[j], c_wukv[j]), c_wo[j]
        h = mix_ffn(y.reshape(BATCH * SEQ, D_MODEL), wo.astype(BF16), norm_mix_post[i][None, :], h,
                    norm_ffn_pre[i][None, :], wg_b, wu_b, wd_b, i, norm_ffn_post[i][None, :])
    return h.reshape(BATCH, SEQ, D_MODEL)
```

```python
import functools

import numpy as np
import jax
import jax.numpy as jnp
from jax import lax
from jax.experimental import pallas as pl
from jax.experimental.pallas import tpu as pltpu

F32 = jnp.float32
BF16 = jnp.bfloat16

D_MODEL = 1024
BATCH = 2
SEQ = 8192
DEPTH = 4
N_MIXERS = 3
GRID_W = 64
D_FF = 2816
NORM_EPS = 1e-6
NEG_INF = -1e30

A_GROUPS = ((128, 1), (512, 4), (2048, 16))
A_NG = 3
A_HEAD_DIM = 64
A_HEADS = 16
A_QKV = 3 * A_NG * A_HEADS * A_HEAD_DIM
A_RADIUS = 64

B_HEAD_DIM = 128
B_HEADS = 8
B_KV_HEADS = 2
B_ROPE_THETA = 10000.0
B_QKV = (B_HEADS + 2 * B_KV_HEADS) * B_HEAD_DIM

C_HEADS = 16
C_Q_LORA = 384
C_KV_LORA = 256
C_NOPE = 64
C_ROPE = 32
C_V = 64
C_ROPE_THETA = 10000.0
C_IN_PAD = 768

LANES = 128
ONES_ROWS = 16
C_ONES_ROWS = LANES - C_V
LOG2E = 1.4426950408889634
VMEM_LIMIT = 56 * 1024 * 1024

TM_PROJ = 256
TN_PROJ = 1024
TM_ROW = 512
FFN_SPLIT = 2
TQ = 512
TK = 512
FLASH_UNROLL = 4
FLASH_HEADS = 4
A_QB = 512
A_W = A_QB + 2 * A_RADIUS
A_HB = 128
A_HW = A_HB + 2 * A_RADIUS
A_UNROLL = 2


def _cparams(sem):
    return pltpu.CompilerParams(dimension_semantics=sem, vmem_limit_bytes=VMEM_LIMIT)


def _rms(x, gain):
    ms = jnp.mean(x * x, axis=-1, keepdims=True)
    return (x * lax.rsqrt(ms + NORM_EPS)) * gain


def _norm_matmul_kernel(x_ref, g_ref, w_ref, o_ref):
    xn = _rms(x_ref[...], g_ref[...]).astype(BF16)
    for c in range(o_ref.shape[1] // TN_PROJ):
        cols = slice(c * TN_PROJ, (c + 1) * TN_PROJ)
        o_ref[:, cols] = jnp.dot(xn, w_ref[:, cols], preferred_element_type=F32)


def norm_matmul(h, gain, w_stack, layer):
    t, d = h.shape
    n = w_stack.shape[2]
    return pl.pallas_call(
        _norm_matmul_kernel,
        grid=(t // TM_PROJ,),
        in_specs=[
            pl.BlockSpec((TM_PROJ, d), lambda i: (i, 0)),
            pl.BlockSpec((1, d), lambda i: (0, 0)),
            pl.BlockSpec((None, d, n), lambda i: (layer, 0, 0), pipeline_mode=pl.Buffered(1)),
        ],
        out_specs=pl.BlockSpec((TM_PROJ, n), lambda i: (i, 0)),
        out_shape=jax.ShapeDtypeStruct((t, n), F32),
        compiler_params=_cparams(("parallel",)),
        name="a_qkv_proj",
    )(h, gain, w_stack)


def _a_attn_kernel(slopes_ref, q_ref, k_ref, v_ref, o_ref, m_ref, l_ref, acc_ref,
                   s_ref, mf_ref, bias_ref, kbuf_ref, vbuf_ref):
    hp = pl.program_id(1)
    g = pl.program_id(2)
    lane = lax.broadcasted_iota(jnp.int32, (A_QB, LANES), 1)
    lo = lane < A_HEAD_DIM
    lo_part = lax.broadcasted_iota(jnp.int32, (A_HB, LANES), 1) < A_HEAD_DIM
    part_lanes = (lo_part, jnp.logical_not(lo_part))
    rel = (lax.broadcasted_iota(jnp.int32, (A_HB, A_HW), 1)
           - lax.broadcasted_iota(jnp.int32, (A_HB, A_HW), 0))

    def group_body(step, gi, d):
        first = step == 0
        final = step == A_NG - 1
        cls_len = SEQ // d
        nblk = cls_len // A_QB
        nit = d * nblk
        win = min(A_W, cls_len)
        lo_w = lax.broadcasted_iota(jnp.int32, (win, LANES), 1) < A_HEAD_DIM
        head_lanes_w = (lo_w, jnp.logical_not(lo_w))

        for oi in range(3):
            dist = jnp.abs(rel - oi * A_RADIUS)
            for h2 in range(2):
                slope = slopes_ref[gi * A_HEADS + hp * 2 + h2] * (d * LOG2E)
                bias_ref[oi, h2] = jnp.where(dist <= A_RADIUS, -slope * dist.astype(F32), NEG_INF)

        def geometry(it):
            r = it // nblk
            i0 = (it % nblk) * A_QB
            if isinstance(it, int):
                kstart = min(max(i0 - A_RADIUS, 0), cls_len - win)
            else:
                kstart = jnp.clip(i0 - A_RADIUS, 0, cls_len - win)
            oi = (i0 - kstart) // A_RADIUS
            if d == 1 and isinstance(it, int):
                q_rows = pl.ds(i0, A_QB)
                k_rows = pl.ds(kstart, win)
            elif d == 1:
                q_rows = pl.ds(pl.multiple_of(i0, A_QB), A_QB)
                k_rows = pl.ds(pl.multiple_of(kstart, A_RADIUS), win)
            else:
                q_rows = pl.ds(r + i0 * d, A_QB, stride=d)
                k_rows = pl.ds(r + kstart * d, win, stride=d)
            return q_rows, k_rows, oi

        def half_window(t, oi):
            lo_row = t * A_HB + (oi - 1) * A_RADIUS
            if isinstance(oi, int):
                start = min(max(lo_row, 0), win - A_HW)
            else:
                start = pl.multiple_of(jnp.clip(lo_row, 0, win - A_HW), A_RADIUS)
            return pl.ds(start, A_HW), oi + t * (A_HB // A_RADIUS) - start // A_RADIUS

        def load_qk(it):
            q_rows, k_rows, oi = geometry(it)
            kbuf_ref[:win, :] = k_ref[k_rows, :].astype(BF16)
            return (q_ref[q_rows, :] * (A_HEAD_DIM ** -0.5 * LOG2E)).astype(BF16), oi

        def scores(qk, slot, parts):
            q, oi = qk
            for t in parts:
                rows = slice(t * A_HB, (t + 1) * A_HB)
                k_rows, boff = half_window(t, oi)
                qt = q[rows]
                qstack = jnp.concatenate([jnp.where(m, qt, jnp.zeros_like(qt)) for m in part_lanes], axis=0)
                s2 = lax.dot_general(qstack, kbuf_ref[k_rows, :], (((1,), (1,)), ((), ())),
                                     preferred_element_type=F32)
                for h2 in range(2):
                    s = s2[h2 * A_HB:(h2 + 1) * A_HB] + bias_ref[boff, h2]
                    s_ref[slot, h2, rows] = s
                    mf_ref[slot, h2, rows] = jnp.broadcast_to(jnp.max(s, axis=1, keepdims=True),
                                                               (A_HB, LANES))

        def load_v(k_rows):
            v = v_ref[k_rows, :]
            for h2 in range(2):
                vbuf_ref[h2, :win, :] = jnp.where(head_lanes_w[h2], v, 1.0).astype(BF16)

        def weighted_values(oi, slot, h2):
            outs = []
            for t in range(A_QB // A_HB):
                rows = slice(t * A_HB, (t + 1) * A_HB)
                k_rows, _ = half_window(t, oi)
                mfull = mf_ref[slot, h2, rows]
                p = jnp.exp2(s_ref[slot, h2, rows] - jnp.concatenate([mfull] * (A_HW // LANES), axis=1))
                outs.append(jnp.dot(p.astype(BF16), vbuf_ref[h2, k_rows, :], preferred_element_type=F32))
            return jnp.concatenate(outs, axis=0)

        def merge(q_rows, slot, outs):
            m_blk = jnp.where(lo, mf_ref[slot, 0], mf_ref[slot, 1])
            pv_blk = jnp.where(lo, outs[0], outs[1])
            l_blk = pltpu.roll(jnp.where(lo, outs[1], outs[0]), A_HEAD_DIM, 1)
            if first:
                m_ref[q_rows, :] = m_blk
                l_ref[q_rows, :] = l_blk
                acc_ref[q_rows, :] = pv_blk
            else:
                m_old = m_ref[q_rows, :]
                m_new = jnp.maximum(m_old, m_blk)
                a_old = jnp.exp2(m_old - m_new)
                a_blk = jnp.exp2(m_blk - m_new)
                l_new = a_old * l_ref[q_rows, :] + a_blk * l_blk
                acc_new = a_old * acc_ref[q_rows, :] + a_blk * pv_blk
                if final:
                    acc_ref[q_rows, :] = acc_new / l_new
                else:
                    m_ref[q_rows, :] = m_new
                    l_ref[q_rows, :] = l_new
                    acc_ref[q_rows, :] = acc_new

        part_groups = ((0, 1), (2, 3))
        qk0 = load_qk(0)
        for parts in part_groups:
            scores(qk0, 0, parts)

        def blocks(it0, last):
            for u in range(A_UNROLL):
                has_next = not (last and u == A_UNROLL - 1)
                qk = load_qk(it0 + u + 1) if has_next else None
                q_rows, k_rows, oi = geometry(it0 + u)
                load_v(k_rows)
                outs = []
                for h2 in range(2):
                    if has_next:
                        scores(qk, (u + 1) % 2, part_groups[h2])
                    outs.append(weighted_values(oi, u % 2, h2))
                merge(q_rows, u % 2, outs)

        def body(jj, carry):
            blocks(A_UNROLL * jj, False)
            return carry

        lax.fori_loop(0, nit // A_UNROLL - 1, body, 0)
        blocks(nit - A_UNROLL, True)
        if final:
            o_ref[...] = acc_ref[...].astype(o_ref.dtype)

    for step in range(A_NG):
        gi = A_NG - 1 - step
        pl.when(g == step)(functools.partial(group_body, step, gi, A_GROUPS[gi][1]))


def a_attention(qkv, slopes):
    npairs = A_HEADS // 2

    def col(which):
        return lambda b, hp, g: (b, 0, (which * A_NG + (A_NG - 1 - g)) * npairs + hp)

    return pl.pallas_call(
        _a_attn_kernel,
        grid=(BATCH, npairs, A_NG),
        in_specs=[
            pl.BlockSpec(memory_space=pltpu.SMEM),
            pl.BlockSpec((None, SEQ, LANES), col(0)),
            pl.BlockSpec((None, SEQ, LANES), col(1)),
            pl.BlockSpec((None, SEQ, LANES), col(2)),
        ],
        out_specs=pl.BlockSpec((None, SEQ, LANES), lambda b, hp, g: (b, 0, hp)),
        out_shape=jax.ShapeDtypeStruct((BATCH, SEQ, D_MODEL), BF16),
        scratch_shapes=[
            pltpu.VMEM((SEQ, LANES), F32),
            pltpu.VMEM((SEQ, LANES), F32),
            pltpu.VMEM((SEQ, LANES), F32),
            pltpu.VMEM((2, 2, A_QB, A_HW), F32),
            pltpu.VMEM((2, 2, A_QB, LANES), F32),
            pltpu.VMEM((3, 2, A_HB, A_HW), F32),
            pltpu.VMEM((A_W, LANES), BF16),
            pltpu.VMEM((2, A_W, LANES), BF16),
        ],
        compiler_params=_cparams(("parallel", "parallel", "arbitrary")),
        name="a_attention",
    )(slopes, qkv, qkv, qkv)


def _flash_kernel(qt_ref, k_ref, vt_ref, o_ref, m_ref, acc_ref, s_ref, mc_ref, *, nh, dv, shared_kv):
    m_ref[...] = jnp.full(m_ref.shape, -jnp.inf, F32)
    acc_ref[...] = jnp.zeros(acc_ref.shape, F32)
    ones = jnp.ones((acc_ref.shape[1] - dv, TK), BF16)
    nchunk = SEQ // TK

    def chunk_rows(chunk):
        start = chunk * TK
        return pl.ds(start if isinstance(start, int) else pl.multiple_of(start, TK), TK)

    def scores(chunk, slot, h):
        rows = chunk_rows(chunk)
        k = k_ref[rows, :] if shared_kv else k_ref[rows, h * LANES:(h + 1) * LANES]
        st = jnp.dot(k, qt_ref[h * LANES:(h + 1) * LANES, :], preferred_element_type=F32)
        s_ref[slot, h] = st
        mc_ref[slot, h] = jnp.max(st, axis=0, keepdims=True)

    def update(chunk, slot, h):
        rows = chunk_rows(chunk)
        vt = vt_ref[:, rows] if shared_kv else vt_ref[h * dv:(h + 1) * dv, rows]
        m_prev = m_ref[h]
        m_new = jnp.maximum(m_prev, mc_ref[slot, h])
        alpha = jnp.exp2(m_prev - m_new)
        pt = jnp.exp2(s_ref[slot, h] - m_new).astype(BF16)
        pv = jnp.dot(jnp.concatenate([vt, ones], axis=0), pt, preferred_element_type=F32)
        acc_ref[h] = alpha * acc_ref[h] + pv
        m_ref[h] = m_new

    for h in range(nh):
        scores(0, 0, h)

    def chunks(j, last):
        for u in range(FLASH_UNROLL):
            for h in range(nh):
                if not (last and u == FLASH_UNROLL - 1):
                    scores(j + u + 1, (u + 1) % 2, h)
                update(j + u, u % 2, h)

    def body(jj, carry):
        chunks(FLASH_UNROLL * jj, False)
        return carry

    lax.fori_loop(0, nchunk // FLASH_UNROLL - 1, body, 0)
    chunks(nchunk - FLASH_UNROLL, True)

    outs = []
    for h in range(nh):
        acc = acc_ref[h]
        outs.append(acc[:dv, :] / acc[dv:dv + 1, :])
    ot = outs[0] if nh == 1 else jnp.concatenate(outs, axis=0)
    o_ref[...] = ot.T.astype(o_ref.dtype)


def _flash_call(kernel, grid, nh, acc_rows, in_specs, out_spec, args, name):
    return pl.pallas_call(
        kernel,
        grid=grid,
        in_specs=in_specs,
        out_specs=out_spec,
        out_shape=jax.ShapeDtypeStruct((BATCH, SEQ, D_MODEL), BF16),
        scratch_shapes=[
            pltpu.VMEM((nh, 1, TQ), F32),
            pltpu.VMEM((nh, acc_rows, TQ), F32),
            pltpu.VMEM((2, nh, TK, TQ), F32),
            pltpu.VMEM((2, nh, 1, TQ), F32),
        ],
        compiler_params=_cparams(("parallel",) * (len(grid) - 1) + ("arbitrary",)),
        name=name,
    )(*args)


def b_attention(qt, k, vt):
    grp = B_HEADS // B_KV_HEADS
    nh = FLASH_HEADS
    kernel = functools.partial(_flash_kernel, nh=nh, dv=B_HEAD_DIM, shared_kv=True)
    in_specs = [
        pl.BlockSpec((None, nh * LANES, TQ), lambda b, kv, g, i: (b, kv * (grp // nh) + g, i)),
        pl.BlockSpec((None, SEQ, LANES), lambda b, kv, g, i: (b, 0, kv)),
        pl.BlockSpec((None, LANES, SEQ), lambda b, kv, g, i: (b, kv, 0)),
    ]
    out_spec = pl.BlockSpec((None, TQ, nh * LANES), lambda b, kv, g, i: (b, i, kv * (grp // nh) + g))
    grid = (BATCH, B_KV_HEADS, grp // nh, SEQ // TQ)
    return _flash_call(kernel, grid, nh, B_HEAD_DIM + ONES_ROWS, in_specs, out_spec, (qt, k, vt),
                       "b_attention")


def c_attention(qt, k, vt):
    nh = FLASH_HEADS
    kernel = functools.partial(_flash_kernel, nh=nh, dv=C_V, shared_kv=False)
    in_specs = [
        pl.BlockSpec((None, nh * LANES, TQ), lambda b, p, i: (b, p, i)),
        pl.BlockSpec((None, SEQ, nh * LANES), lambda b, p, i: (b, 0, p)),
        pl.BlockSpec((None, nh * C_V, SEQ), lambda b, p, i: (b, p, 0)),
    ]
    out_spec = pl.BlockSpec((None, TQ, nh * C_V), lambda b, p, i: (b, i, p))
    grid = (BATCH, C_HEADS // nh, SEQ // TQ)
    return _flash_call(kernel, grid, nh, C_V + C_ONES_ROWS, in_specs, out_spec, (qt, k, vt),
                       "c_attention")


def _rope_lanes(x, cos, sin_a, sin_b, shift):
    return (x * cos + pltpu.roll(x, LANES - shift, 1) * sin_a + pltpu.roll(x, shift, 1) * sin_b)


def _nt_dot(a, b):
    return lax.dot_general(a, b, (((1,), (1,)), ((), ())), preferred_element_type=F32)


def _swap_rows(x, start, half, groups):
    pieces = [x[:start]] if start else []
    for g0 in range(start, start + 2 * half * groups, 2 * half):
        pieces += [x[g0 + half:g0 + 2 * half], x[g0:g0 + half]]
    if start + 2 * half * groups < x.shape[0]:
        pieces.append(x[start + 2 * half * groups:])
    return jnp.concatenate(pieces, axis=0)


def _b_prep_kernel(h_ref, g_ref, wqt_ref, wk_ref, wvt_ref, qg_ref, kn_ref, cost_ref, sint_ref,
                   cos_ref, sa_ref, sb_ref, qt_ref, k_ref, vt_ref):
    xn = _rms(h_ref[...], g_ref[...]).astype(BF16)
    cos_t, sin_t, qgain = cost_ref[...], sint_ref[...], qg_ref[...]
    cos, sa, sb = cos_ref[...], sa_ref[...], sb_ref[...]
    half_rows = B_HEADS * LANES // 2

    def q_heads(qt_half, first_head):
        for hd in range(B_HEADS // 2):
            x = qt_half[hd * LANES:(hd + 1) * LANES]
            ms = jnp.mean(x * x, axis=0, keepdims=True)
            x = (x * lax.rsqrt(ms + NORM_EPS)) * qgain
            x = x * cos_t + _swap_rows(x, 0, B_HEAD_DIM // 4, 2) * sin_t
            out_rows = slice((first_head + hd) * LANES, (first_head + hd + 1) * LANES)
            qt_ref[out_rows, :] = x.astype(BF16)

    kk = jnp.dot(xn, wk_ref[...], preferred_element_type=F32)
    qt_a = _nt_dot(wqt_ref[:half_rows, :], xn)
    for j in range(B_KV_HEADS):
        cols = slice(j * LANES, (j + 1) * LANES)
        x = _rope_lanes(_rms(kk[:, cols], kn_ref[...]), cos, sa, sb, B_HEAD_DIM // 4)
        k_ref[:, cols] = x.astype(BF16)
    qt_b = _nt_dot(wqt_ref[half_rows:, :], xn)
    q_heads(qt_a, 0)
    vt = _nt_dot(wvt_ref[...], xn)
    q_heads(qt_b, B_HEADS // 2)
    vt_ref[...] = vt.astype(BF16)


def b_prep(h, gain, wq_t, wk, wv_t, qgain_t, knorm, tables_t, tables):
    t, d = h.shape
    nrow = SEQ // TM_ROW
    row = lambda i: (i, 0)
    fixed = lambda i: (0, 0)
    tab = lambda i: (i % nrow, 0)
    tab_t = lambda i: (0, i % nrow)
    colblk = lambda i: (i // nrow, 0, i % nrow)
    nq = B_HEADS * B_HEAD_DIM
    nkv = B_KV_HEADS * B_HEAD_DIM
    return pl.pallas_call(
        _b_prep_kernel,
        grid=(t // TM_ROW,),
        in_specs=[
            pl.BlockSpec((TM_ROW, d), row),
            pl.BlockSpec((1, d), fixed),
            pl.BlockSpec((nq, d), fixed),
            pl.BlockSpec((d, nkv), fixed),
            pl.BlockSpec((nkv, d), fixed),
            pl.BlockSpec((B_HEAD_DIM, TM_ROW), fixed),
            pl.BlockSpec((1, B_HEAD_DIM), fixed),
            pl.BlockSpec((LANES, TM_ROW), tab_t),
            pl.BlockSpec((LANES, TM_ROW), tab_t),
            pl.BlockSpec((TM_ROW, LANES), tab),
            pl.BlockSpec((TM_ROW, LANES), tab),
            pl.BlockSpec((TM_ROW, LANES), tab),
        ],
        out_specs=[
            pl.BlockSpec((None, nq, TM_ROW), colblk),
            pl.BlockSpec((TM_ROW, nkv), row),
            pl.BlockSpec((None, nkv, TM_ROW), colblk),
        ],
        out_shape=[
            jax.ShapeDtypeStruct((BATCH, nq, SEQ), BF16),
            jax.ShapeDtypeStruct((t, nkv), BF16),
            jax.ShapeDtypeStruct((BATCH, nkv, SEQ), BF16),
        ],
        compiler_params=_cparams(("parallel",)),
        name="b_prep",
    )(h, gain, wq_t, wk, wv_t, qgain_t, knorm, *tables_t, *tables)


def _c_prep_kernel(h_ref, g_ref, win_ref, qn_ref, kvn_ref, wuqt_ref, wuk_ref, wuvt_ref,
                   cost_ref, sint_ref, cos_ref, sa_ref, sb_ref, q_ref, k_ref, v_ref):
    xn = _rms(h_ref[...], g_ref[...]).astype(BF16)
    c = jnp.dot(xn, win_ref[...], preferred_element_type=F32)
    cq = _rms(c[:, :C_Q_LORA], qn_ref[...]).astype(BF16)
    ckv = _rms(c[:, C_Q_LORA:C_Q_LORA + C_KV_LORA], kvn_ref[...]).astype(BF16)
    cos, sa, sb = cos_ref[...], sa_ref[...], sb_ref[...]
    cos_t, sin_t = cost_ref[...], sint_ref[...]
    half_rows = C_HEADS * LANES // 2

    def q_heads(qt_half, first_head):
        for hd in range(C_HEADS // 2):
            x = qt_half[hd * LANES:(hd + 1) * LANES]
            x = x * cos_t + _swap_rows(x, C_NOPE, C_ROPE // 2, 1) * sin_t
            out_rows = slice((first_head + hd) * LANES, (first_head + hd + 1) * LANES)
            q_ref[out_rows, :] = x.astype(BF16)

    kn = jnp.dot(ckv, wuk_ref[...], preferred_element_type=F32)
    qt_a = _nt_dot(wuqt_ref[:half_rows, :], cq)
    kr = _rope_lanes(c[:, C_Q_LORA + C_KV_LORA:], cos, sa, sb, C_ROPE // 2)
    for hd in range(C_HEADS):
        cols = slice(hd * LANES, (hd + 1) * LANES)
        k_ref[:, cols] = (kn[:, cols] + kr).astype(BF16)
    qt_b = _nt_dot(wuqt_ref[half_rows:, :], cq)
    q_heads(qt_a, 0)
    vt = _nt_dot(wuvt_ref[...], ckv)
    q_heads(qt_b, C_HEADS // 2)
    v_ref[...] = vt.astype(BF16)


def c_prep(h, gain, win, qnorm, kvnorm, wuq_t, wuk, wuv_t, tables_t, tables):
    t, d = h.shape
    nrow = SEQ // TM_ROW
    row = lambda i: (i, 0)
    fixed = lambda i: (0, 0)
    tab = lambda i: (i % nrow, 0)
    tab_t = lambda i: (0, i % nrow)
    colblk = lambda i: (i // nrow, 0, i % nrow)
    hq = C_HEADS * LANES
    return pl.pallas_call(
        _c_prep_kernel,
        grid=(t // TM_ROW,),
        in_specs=[
            pl.BlockSpec((TM_ROW, d), row),
            pl.BlockSpec((1, d), fixed),
            pl.BlockSpec((d, C_IN_PAD), fixed),
            pl.BlockSpec((1, C_Q_LORA), fixed),
            pl.BlockSpec((1, C_KV_LORA), fixed),
            pl.BlockSpec((hq, C_Q_LORA), fixed),
            pl.BlockSpec((C_KV_LORA, hq), fixed),
            pl.BlockSpec((C_HEADS * C_V, C_KV_LORA), fixed),
            pl.BlockSpec((LANES, TM_ROW), tab_t),
            pl.BlockSpec((LANES, TM_ROW), tab_t),
            pl.BlockSpec((TM_ROW, LANES), tab),
            pl.BlockSpec((TM_ROW, LANES), tab),
            pl.BlockSpec((TM_ROW, LANES), tab),
        ],
        out_specs=[
            pl.BlockSpec((None, hq, TM_ROW), colblk),
            pl.BlockSpec((TM_ROW, hq), row),
            pl.BlockSpec((None, C_HEADS * C_V, TM_ROW), colblk),
        ],
        out_shape=[
            jax.ShapeDtypeStruct((BATCH, hq, SEQ), BF16),
            jax.ShapeDtypeStruct((t, hq), BF16),
            jax.ShapeDtypeStruct((BATCH, C_HEADS * C_V, SEQ), BF16),
        ],
        compiler_params=_cparams(("parallel",)),
        name="c_prep",
    )(h, gain, win, qnorm, kvnorm, wuq_t, wuk, wuv_t, *tables_t, *tables)


def _mix_ffn_kernel(y_ref, wo_ref, gmix_ref, h_ref, gpre_ref, wg_ref, wu_ref, wd_ref, gpost_ref,
                    o_ref):
    sub = TM_ROW // FFN_SPLIT
    parts = [slice(i * sub, (i + 1) * sub) for i in range(FFN_SPLIT)]
    z = [jnp.dot(y_ref[p, :], wo_ref[...], preferred_element_type=F32) for p in parts]
    h1, gate, up, down = [], [], [], []
    for i, p in enumerate(parts):
        h1.append(h_ref[p, :] + _rms(z[i], gmix_ref[...]))
        xn = _rms(h1[i], gpre_ref[...]).astype(BF16)
        gate.append(jnp.dot(xn, wg_ref[...], preferred_element_type=F32))
        up.append(jnp.dot(xn, wu_ref[...], preferred_element_type=F32))
    for i in range(FFN_SPLIT):
        act = (gate[i] * jax.nn.sigmoid(gate[i])) * up[i]
        down.append(jnp.dot(act.astype(BF16), wd_ref[...], preferred_element_type=F32))
    for i, p in enumerate(parts):
        o_ref[p, :] = h1[i] + _rms(down[i], gpost_ref[...])


def mix_ffn(y, wo, gmix, h, gpre, wg, wu, wd, layer, gpost):
    t, d = h.shape
    row = lambda i: (i, 0)
    fixed = lambda i: (0, 0)
    layer_blk = lambda i: (layer, 0, 0)
    once = pl.Buffered(1)
    return pl.pallas_call(
        _mix_ffn_kernel,
        grid=(t // TM_ROW,),
        in_specs=[
            pl.BlockSpec((TM_ROW, d), row),
            pl.BlockSpec((d, d), fixed, pipeline_mode=once),
            pl.BlockSpec((1, d), fixed),
            pl.BlockSpec((TM_ROW, d), row),
            pl.BlockSpec((1, d), fixed),
            pl.BlockSpec((None, d, D_FF), layer_blk, pipeline_mode=once),
            pl.BlockSpec((None, d, D_FF), layer_blk, pipeline_mode=once),
            pl.BlockSpec((None, D_FF, d), layer_blk, pipeline_mode=once),
            pl.BlockSpec((1, d), fixed),
        ],
        out_specs=pl.BlockSpec((TM_ROW, d), row),
        out_shape=jax.ShapeDtypeStruct((t, d), F32),
        compiler_params=_cparams(("parallel",)),
        name="mix_ffn",
    )(y, wo, gmix, h, gpre, wg, wu, wd, gpost)


def _angles(pos, n_freq, dim, theta):
    freqs = jnp.power(jnp.float32(theta), -jnp.arange(n_freq, dtype=F32) * 2.0 / dim)
    ang = freqs[:, None] * pos[None, :]
    return lax.optimization_barrier((jnp.cos(ang), jnp.sin(ang)))


def _b_tables():
    quarter = B_HEAD_DIM // 4
    rows = SEQ // GRID_W
    rc, rs = _angles(jnp.arange(rows, dtype=F32), quarter, B_HEAD_DIM // 2, B_ROPE_THETA)
    cc, cs = _angles(jnp.arange(GRID_W, dtype=F32), quarter, B_HEAD_DIM // 2, B_ROPE_THETA)
    by_row = lambda x: jnp.repeat(x, GRID_W, axis=1)
    by_col = lambda x: jnp.tile(x, (1, rows))
    rc, rs, cc, cs = by_row(rc), by_row(rs), by_col(cc), by_col(cs)
    zero = jnp.zeros_like(rs)
    cos_t = jnp.concatenate([rc, rc, cc, cc], axis=0)
    sin_t = jnp.concatenate([-rs, rs, -cs, cs], axis=0)
    sin_a = jnp.concatenate([-rs, zero, -cs, zero], axis=0)
    sin_b = jnp.concatenate([zero, rs, zero, cs], axis=0)
    scale = B_HEAD_DIM ** -0.5 * LOG2E
    return (cos_t * scale, sin_t * scale), (cos_t.T, sin_a.T, sin_b.T)


def _c_tables():
    half = C_ROPE // 2
    c, s = _angles(jnp.arange(SEQ, dtype=F32), half, C_ROPE, C_ROPE_THETA)
    one_lo = jnp.ones((C_NOPE, SEQ), F32)
    one_hi = jnp.ones((LANES - C_NOPE - C_ROPE, SEQ), F32)
    zero = jnp.zeros_like(s)
    cos_t = jnp.concatenate([one_lo, c, c, one_hi], axis=0)
    sin_t = jnp.concatenate([0 * one_lo, -s, s, 0 * one_hi], axis=0)
    sin_a = jnp.concatenate([0 * one_lo, -s, zero, 0 * one_hi], axis=0)
    sin_b = jnp.concatenate([0 * one_lo, zero, s, 0 * one_hi], axis=0)
    scale = (C_NOPE + C_ROPE) ** -0.5 * LOG2E
    return (cos_t * scale, sin_t * scale), (cos_t.T, sin_a.T, sin_b.T)


def _alibi_slopes():
    n = A_NG * A_HEADS
    return jnp.asarray(2.0 ** (-8.0 * np.arange(1, n + 1) / n), dtype=F32)


def _mixer_a(h, gain, wqkv_stack, layer):
    qkv = norm_matmul(h, gain, wqkv_stack, layer)
    return a_attention(qkv.reshape(BATCH, SEQ, A_QKV), _alibi_slopes())


def _mixer_b(h, gain, wqkv, qnorm, knorm):
    nq = B_HEADS * B_HEAD_DIM
    nkv = B_KV_HEADS * B_HEAD_DIM
    w = wqkv.astype(BF16)
    qgain_t = jnp.broadcast_to(qnorm[:, None], (B_HEAD_DIM, TM_ROW))
    tables_t, tables = _b_tables()
    qt, k, vt = b_prep(h, gain, w[:, :nq].T, w[:, nq:nq + nkv], w[:, nq + nkv:].T, qgain_t,
                       knorm[None, :], tables_t, tables)
    return b_attention(qt, k.reshape(BATCH, SEQ, -1), vt)


def _mixer_c(h, gain, win, qnorm, kvnorm, wuq, wukv):
    win_p = jnp.zeros((D_MODEL, C_IN_PAD), F32)
    win_p = win_p.at[:, :C_Q_LORA + C_KV_LORA].set(win[:, :C_Q_LORA + C_KV_LORA])
    kr0 = C_Q_LORA + C_KV_LORA + C_NOPE
    win_p = win_p.at[:, kr0:kr0 + C_ROPE].set(win[:, C_Q_LORA + C_KV_LORA:])
    wuq_p = jnp.pad(wuq.reshape(C_Q_LORA, C_HEADS, C_NOPE + C_ROPE),
                    ((0, 0), (0, 0), (0, LANES - C_NOPE - C_ROPE))).reshape(C_Q_LORA, C_HEADS * LANES)
    wukv3 = wukv.reshape(C_KV_LORA, C_HEADS, C_NOPE + C_V)
    wuk_p = jnp.pad(wukv3[:, :, :C_NOPE],
                    ((0, 0), (0, 0), (0, LANES - C_NOPE))).reshape(C_KV_LORA, C_HEADS * LANES)
    wuv = wukv3[:, :, C_NOPE:].reshape(C_KV_LORA, C_HEADS * C_V)
    tables_t, tables = _c_tables()
    qt, k, vt = c_prep(h, gain, win_p.astype(BF16), qnorm[None, :], kvnorm[None, :],
                       wuq_p.astype(BF16).T, wuk_p.astype(BF16), wuv.astype(BF16).T,
                       tables_t, tables)
    return c_attention(qt, k.reshape(BATCH, SEQ, -1), vt)


def kernel(x, norm_mix_pre, norm_mix_post, norm_ffn_pre, norm_ffn_post, ffn_wg, ffn_wu, ffn_wd,
           a_wqkv, a_wo, b_wqkv, b_qnorm, b_knorm, b_wo,
           c_win, c_qnorm, c_kvnorm, c_wuq, c_wukv, c_wo):
    h = x.reshape(BATCH * SEQ, D_MODEL)
    a_wqkv_b = a_wqkv.astype(BF16)
    wg_b, wu_b, wd_b = ffn_wg.astype(BF16), ffn_wu.astype(BF16), ffn_wd.astype(BF16)
    for i in range(DEPTH):
        kind = i % N_MIXERS
        j = i // N_MIXERS
        gpre = norm_mix_pre[i][None, :]
        if kind == 0:
            y, wo = _mixer_a(h, gpre, a_wqkv_b, j), a_wo[j]
        elif kind == 1:
            y, wo = _mixer_b(h, gpre, b_wqkv[j], b_qnorm[j], b_knorm[j]), b_wo[j]
        else:
            y, wo = _mixer_c(h, gpre, c_win[j], c_qnorm[j], c_kvnorm[j], c_wuq[j], c_wukv[j]), c_wo[j]
        h = mix_ffn(y.reshape(BATCH * SEQ, D_MODEL), wo.astype(BF16), norm_mix_post[i][None, :], h,
                    norm_ffn_pre[i][None, :], wg_b, wu_b, wd_b, i, norm_ffn_post[i][None, :])
    return h.reshape(BATCH, SEQ, D_MODEL)
```

```python
import functools

import numpy as np
import jax
import jax.numpy as jnp
from jax import lax
from jax.experimental import pallas as pl
from jax.experimental.pallas import tpu as pltpu

F32 = jnp.float32
BF16 = jnp.bfloat16

D_MODEL = 1024
BATCH = 2
SEQ = 8192
DEPTH = 4
N_MIXERS = 3
GRID_W = 64
D_FF = 2816
NORM_EPS = 1e-6
NEG_INF = -1e30

A_GROUPS = ((128, 1), (512, 4), (2048, 16))
A_NG = 3
A_HEAD_DIM = 64
A_HEADS = 16
A_QKV = 3 * A_NG * A_HEADS * A_HEAD_DIM
A_RADIUS = 64

B_HEAD_DIM = 128
B_HEADS = 8
B_KV_HEADS = 2
B_ROPE_THETA = 10000.0
B_QKV = (B_HEADS + 2 * B_KV_HEADS) * B_HEAD_DIM

C_HEADS = 16
C_Q_LORA = 384
C_KV_LORA = 256
C_NOPE = 64
C_ROPE = 32
C_V = 64
C_ROPE_THETA = 10000.0
C_IN_PAD = 768

LANES = 128
ONES_ROWS = 16
C_ONES_ROWS = LANES - C_V
LOG2E = 1.4426950408889634
VMEM_LIMIT = 56 * 1024 * 1024

TM_PROJ = 256
TN_PROJ = 1024
TM_ROW = 512
FFN_SPLIT = 2
TQ = 512
TK = 512
FLASH_UNROLL = 4
FLASH_HEADS = 4
A_QB = 512
A_W = A_QB + 2 * A_RADIUS
A_HB = 128
A_HW = A_HB + 2 * A_RADIUS
A_UNROLL = 2


def _cparams(sem):
    return pltpu.CompilerParams(dimension_semantics=sem, vmem_limit_bytes=VMEM_LIMIT)


def _rms(x, gain):
    ms = jnp.mean(x * x, axis=-1, keepdims=True)
    return (x * lax.rsqrt(ms + NORM_EPS)) * gain


def _norm_matmul_kernel(x_ref, g_ref, w_ref, o_ref):
    xn = _rms(x_ref[...], g_ref[...]).astype(BF16)
    for c in range(o_ref.shape[1] // TN_PROJ):
        cols = slice(c * TN_PROJ, (c + 1) * TN_PROJ)
        o_ref[:, cols] = jnp.dot(xn, w_ref[:, cols], preferred_element_type=F32)


def norm_matmul(h, gain, w_stack, layer):
    t, d = h.shape
    n = w_stack.shape[2]
    return pl.pallas_call(
        _norm_matmul_kernel,
        grid=(t // TM_PROJ,),
        in_specs=[
            pl.BlockSpec((TM_PROJ, d), lambda i: (i, 0)),
            pl.BlockSpec((1, d), lambda i: (0, 0)),
            pl.BlockSpec((None, d, n), lambda i: (layer, 0, 0), pipeline_mode=pl.Buffered(1)),
        ],
        out_specs=pl.BlockSpec((TM_PROJ, n), lambda i: (i, 0)),
        out_shape=jax.ShapeDtypeStruct((t, n), F32),
        compiler_params=_cparams(("parallel",)),
        name="a_qkv_proj",
    )(h, gain, w_stack)


def _a_attn_kernel(slopes_ref, q_ref, k_ref, v_ref, o_ref, m_ref, l_ref, acc_ref,
                   s_ref, mf_ref, bias_ref, kbuf_ref, vbuf_ref):
    hp = pl.program_id(1)
    g = pl.program_id(2)
    lane = lax.broadcasted_iota(jnp.int32, (A_QB, LANES), 1)
    lo = lane < A_HEAD_DIM
    lo_part = lax.broadcasted_iota(jnp.int32, (A_HB, LANES), 1) < A_HEAD_DIM
    part_lanes = (lo_part, jnp.logical_not(lo_part))
    rel = (lax.broadcasted_iota(jnp.int32, (A_HB, A_HW), 1)
           - lax.broadcasted_iota(jnp.int32, (A_HB, A_HW), 0))

    def group_body(step, gi, d):
        first = step == 0
        final = step == A_NG - 1
        cls_len = SEQ // d
        nblk = cls_len // A_QB
        nit = d * nblk
        win = min(A_W, cls_len)
        lo_w = lax.broadcasted_iota(jnp.int32, (win, LANES), 1) < A_HEAD_DIM
        head_lanes_w = (lo_w, jnp.logical_not(lo_w))

        for oi in range(3):
            dist = jnp.abs(rel - oi * A_RADIUS)
            for h2 in range(2):
                slope = slopes_ref[gi * A_HEADS + hp * 2 + h2] * (d * LOG2E)
                bias_ref[oi, h2] = jnp.where(dist <= A_RADIUS, -slope * dist.astype(F32), NEG_INF)

        def geometry(it):
            r = it // nblk
            i0 = (it % nblk) * A_QB
            if isinstance(it, int):
                kstart = min(max(i0 - A_RADIUS, 0), cls_len - win)
            else:
                kstart = jnp.clip(i0 - A_RADIUS, 0, cls_len - win)
            oi = (i0 - kstart) // A_RADIUS
            if d == 1 and isinstance(it, int):
                q_rows = pl.ds(i0, A_QB)
                k_rows = pl.ds(kstart, win)
            elif d == 1:
                q_rows = pl.ds(pl.multiple_of(i0, A_QB), A_QB)
                k_rows = pl.ds(pl.multiple_of(kstart, A_RADIUS), win)
            else:
                q_rows = pl.ds(r + i0 * d, A_QB, stride=d)
                k_rows = pl.ds(r + kstart * d, win, stride=d)
            return q_rows, k_rows, oi

        def half_window(t, oi):
            lo_row = t * A_HB + (oi - 1) * A_RADIUS
            if isinstance(oi, int):
                start = min(max(lo_row, 0), win - A_HW)
            else:
                start = pl.multiple_of(jnp.clip(lo_row, 0, win - A_HW), A_RADIUS)
            return pl.ds(start, A_HW), oi + t * (A_HB // A_RADIUS) - start // A_RADIUS

        def load_qk(it):
            q_rows, k_rows, oi = geometry(it)
            kbuf_ref[:win, :] = k_ref[k_rows, :].astype(BF16)
            return (q_ref[q_rows, :] * (A_HEAD_DIM ** -0.5 * LOG2E)).astype(BF16), oi

        def scores(qk, slot, parts):
            q, oi = qk
            for t in parts:
                rows = slice(t * A_HB, (t + 1) * A_HB)
                k_rows, boff = half_window(t, oi)
                qt = q[rows]
                qstack = jnp.concatenate([jnp.where(m, qt, jnp.zeros_like(qt)) for m in part_lanes], axis=0)
                s2 = lax.dot_general(qstack, kbuf_ref[k_rows, :], (((1,), (1,)), ((), ())),
                                     preferred_element_type=F32)
                for h2 in range(2):
                    s = s2[h2 * A_HB:(h2 + 1) * A_HB] + bias_ref[boff, h2]
                    s_ref[slot, h2, rows] = s
                    mf_ref[slot, h2, rows] = jnp.broadcast_to(jnp.max(s, axis=1, keepdims=True),
                                                               (A_HB, LANES))

        def load_v(k_rows):
            v = v_ref[k_rows, :]
            for h2 in range(2):
                vbuf_ref[h2, :win, :] = jnp.where(head_lanes_w[h2], v, 1.0).astype(BF16)

        def weighted_values(oi, slot, parts, outs):
            for t in parts:
                rows = slice(t * A_HB, (t + 1) * A_HB)
                k_rows, _ = half_window(t, oi)
                ps = []
                for h2 in range(2):
                    mfull = mf_ref[slot, h2, rows]
                    ps.append(jnp.exp2(s_ref[slot, h2, rows]
                                       - jnp.concatenate([mfull] * (A_HW // LANES), axis=1)).astype(BF16))
                values = jnp.concatenate([vbuf_ref[0, k_rows, :], vbuf_ref[1, k_rows, :]], axis=1)
                o2 = jnp.dot(jnp.concatenate(ps, axis=0), values, preferred_element_type=F32)
                for h2 in range(2):
                    outs[h2].append(o2[h2 * A_HB:(h2 + 1) * A_HB, h2 * LANES:(h2 + 1) * LANES])

        def merge(q_rows, slot, outs):
            m_blk = jnp.where(lo, mf_ref[slot, 0], mf_ref[slot, 1])
            pv_blk = jnp.where(lo, outs[0], outs[1])
            l_blk = pltpu.roll(jnp.where(lo, outs[1], outs[0]), A_HEAD_DIM, 1)
            if first:
                m_ref[q_rows, :] = m_blk
                l_ref[q_rows, :] = l_blk
                acc_ref[q_rows, :] = pv_blk
            else:
                m_old = m_ref[q_rows, :]
                m_new = jnp.maximum(m_old, m_blk)
                a_old = jnp.exp2(m_old - m_new)
                a_blk = jnp.exp2(m_blk - m_new)
                l_new = a_old * l_ref[q_rows, :] + a_blk * l_blk
                acc_new = a_old * acc_ref[q_rows, :] + a_blk * pv_blk
                if final:
                    acc_ref[q_rows, :] = acc_new / l_new
                else:
                    m_ref[q_rows, :] = m_new
                    l_ref[q_rows, :] = l_new
                    acc_ref[q_rows, :] = acc_new

        part_groups = ((0, 1), (2, 3))
        qk0 = load_qk(0)
        for parts in part_groups:
            scores(qk0, 0, parts)

        def blocks(it0, last):
            for u in range(A_UNROLL):
                has_next = not (last and u == A_UNROLL - 1)
                qk = load_qk(it0 + u + 1) if has_next else None
                q_rows, k_rows, oi = geometry(it0 + u)
                load_v(k_rows)
                outs = ([], [])
                for parts in part_groups:
                    if has_next:
                        scores(qk, (u + 1) % 2, parts)
                    weighted_values(oi, u % 2, parts, outs)
                merge(q_rows, u % 2, [jnp.concatenate(o, axis=0) for o in outs])

        def body(jj, carry):
            blocks(A_UNROLL * jj, False)
            return carry

        lax.fori_loop(0, nit // A_UNROLL - 1, body, 0)
        blocks(nit - A_UNROLL, True)
        if final:
            o_ref[...] = acc_ref[...].astype(o_ref.dtype)

    for step in range(A_NG):
        gi = A_NG - 1 - step
        pl.when(g == step)(functools.partial(group_body, step, gi, A_GROUPS[gi][1]))


def a_attention(qkv, slopes):
    npairs = A_HEADS // 2

    def col(which):
        return lambda b, hp, g: (b, 0, (which * A_NG + (A_NG - 1 - g)) * npairs + hp)

    return pl.pallas_call(
        _a_attn_kernel,
        grid=(BATCH, npairs, A_NG),
        in_specs=[
            pl.BlockSpec(memory_space=pltpu.SMEM),
            pl.BlockSpec((None, SEQ, LANES), col(0)),
            pl.BlockSpec((None, SEQ, LANES), col(1)),
            pl.BlockSpec((None, SEQ, LANES), col(2)),
        ],
        out_specs=pl.BlockSpec((None, SEQ, LANES), lambda b, hp, g: (b, 0, hp)),
        out_shape=jax.ShapeDtypeStruct((BATCH, SEQ, D_MODEL), BF16),
        scratch_shapes=[
            pltpu.VMEM((SEQ, LANES), F32),
            pltpu.VMEM((SEQ, LANES), F32),
            pltpu.VMEM((SEQ, LANES), F32),
            pltpu.VMEM((2, 2, A_QB, A_HW), F32),
            pltpu.VMEM((2, 2, A_QB, LANES), F32),
            pltpu.VMEM((3, 2, A_HB, A_HW), F32),
            pltpu.VMEM((A_W, LANES), BF16),
            pltpu.VMEM((2, A_W, LANES), BF16),
        ],
        compiler_params=_cparams(("parallel", "parallel", "arbitrary")),
        name="a_attention",
    )(slopes, qkv, qkv, qkv)


def _flash_kernel(qt_ref, k_ref, vt_ref, o_ref, m_ref, acc_ref, s_ref, mc_ref, *, nh, dv, shared_kv):
    m_ref[...] = jnp.full(m_ref.shape, -jnp.inf, F32)
    acc_ref[...] = jnp.zeros(acc_ref.shape, F32)
    ones = jnp.ones((acc_ref.shape[1] - dv, TK), BF16)
    nchunk = SEQ // TK

    def chunk_rows(chunk):
        start = chunk * TK
        return pl.ds(start if isinstance(start, int) else pl.multiple_of(start, TK), TK)

    def scores(chunk, slot, h):
        rows = chunk_rows(chunk)
        k = k_ref[rows, :] if shared_kv else k_ref[rows, h * LANES:(h + 1) * LANES]
        st = jnp.dot(k, qt_ref[h * LANES:(h + 1) * LANES, :], preferred_element_type=F32)
        s_ref[slot, h] = st
        mc_ref[slot, h] = jnp.max(st, axis=0, keepdims=True)

    def update(chunk, slot, h):
        rows = chunk_rows(chunk)
        vt = vt_ref[:, rows] if shared_kv else vt_ref[h * dv:(h + 1) * dv, rows]
        m_prev = m_ref[h]
        m_new = jnp.maximum(m_prev, mc_ref[slot, h])
        alpha = jnp.exp2(m_prev - m_new)
        pt = jnp.exp2(s_ref[slot, h] - m_new).astype(BF16)
        pv = jnp.dot(jnp.concatenate([vt, ones], axis=0), pt, preferred_element_type=F32)
        acc_ref[h] = alpha * acc_ref[h] + pv
        m_ref[h] = m_new

    for h in range(nh):
        scores(0, 0, h)

    def chunks(j, last):
        for u in range(FLASH_UNROLL):
            for h in range(nh):
                if not (last and u == FLASH_UNROLL - 1):
                    scores(j + u + 1, (u + 1) % 2, h)
                update(j + u, u % 2, h)

    def body(jj, carry):
        chunks(FLASH_UNROLL * jj, False)
        return carry

    lax.fori_loop(0, nchunk // FLASH_UNROLL - 1, body, 0)
    chunks(nchunk - FLASH_UNROLL, True)

    outs = []
    for h in range(nh):
        acc = acc_ref[h]
        outs.append(acc[:dv, :] / acc[dv:dv + 1, :])
    ot = outs[0] if nh == 1 else jnp.concatenate(outs, axis=0)
    o_ref[...] = ot.T.astype(o_ref.dtype)


def _flash_call(kernel, grid, nh, acc_rows, in_specs, out_spec, args, name):
    return pl.pallas_call(
        kernel,
        grid=grid,
        in_specs=in_specs,
        out_specs=out_spec,
        out_shape=jax.ShapeDtypeStruct((BATCH, SEQ, D_MODEL), BF16),
        scratch_shapes=[
            pltpu.VMEM((nh, 1, TQ), F32),
            pltpu.VMEM((nh, acc_rows, TQ), F32),
            pltpu.VMEM((2, nh, TK, TQ), F32),
            pltpu.VMEM((2, nh, 1, TQ), F32),
        ],
        compiler_params=_cparams(("parallel",) * (len(grid) - 1) + ("arbitrary",)),
        name=name,
    )(*args)


def b_attention(qt, k, vt):
    grp = B_HEADS // B_KV_HEADS
    nh = FLASH_HEADS
    kernel = functools.partial(_flash_kernel, nh=nh, dv=B_HEAD_DIM, shared_kv=True)
    in_specs = [
        pl.BlockSpec((None, nh * LANES, TQ), lambda b, kv, g, i: (b, kv * (grp // nh) + g, i)),
        pl.BlockSpec((None, SEQ, LANES), lambda b, kv, g, i: (b, 0, kv)),
        pl.BlockSpec((None, LANES, SEQ), lambda b, kv, g, i: (b, kv, 0)),
    ]
    out_spec = pl.BlockSpec((None, TQ, nh * LANES), lambda b, kv, g, i: (b, i, kv * (grp // nh) + g))
    grid = (BATCH, B_KV_HEADS, grp // nh, SEQ // TQ)
    return _flash_call(kernel, grid, nh, B_HEAD_DIM + ONES_ROWS, in_specs, out_spec, (qt, k, vt),
                       "b_attention")


def c_attention(qt, k, vt):
    nh = FLASH_HEADS
    kernel = functools.partial(_flash_kernel, nh=nh, dv=C_V, shared_kv=False)
    in_specs = [
        pl.BlockSpec((None, nh * LANES, TQ), lambda b, p, i: (b, p, i)),
        pl.BlockSpec((None, SEQ, nh * LANES), lambda b, p, i: (b, 0, p)),
        pl.BlockSpec((None, nh * C_V, SEQ), lambda b, p, i: (b, p, 0)),
    ]
    out_spec = pl.BlockSpec((None, TQ, nh * C_V), lambda b, p, i: (b, i, p))
    grid = (BATCH, C_HEADS // nh, SEQ // TQ)
    return _flash_call(kernel, grid, nh, C_V + C_ONES_ROWS, in_specs, out_spec, (qt, k, vt),
                       "c_attention")


def _rope_lanes(x, cos, sin_a, sin_b, shift):
    return (x * cos + pltpu.roll(x, LANES - shift, 1) * sin_a + pltpu.roll(x, shift, 1) * sin_b)


def _nt_dot(a, b):
    return lax.dot_general(a, b, (((1,), (1,)), ((), ())), preferred_element_type=F32)


def _swap_rows(x, start, half, groups):
    pieces = [x[:start]] if start else []
    for g0 in range(start, start + 2 * half * groups, 2 * half):
        pieces += [x[g0 + half:g0 + 2 * half], x[g0:g0 + half]]
    if start + 2 * half * groups < x.shape[0]:
        pieces.append(x[start + 2 * half * groups:])
    return jnp.concatenate(pieces, axis=0)


def _b_prep_kernel(h_ref, g_ref, wqt_ref, wk_ref, wvt_ref, qg_ref, kn_ref, cost_ref, sint_ref,
                   cos_ref, sa_ref, sb_ref, qt_ref, k_ref, vt_ref):
    xn = _rms(h_ref[...], g_ref[...]).astype(BF16)
    cos_t, sin_t, qgain = cost_ref[...], sint_ref[...], qg_ref[...]
    cos, sa, sb = cos_ref[...], sa_ref[...], sb_ref[...]
    half_rows = B_HEADS * LANES // 2

    def q_heads(qt_half, first_head):
        for hd in range(B_HEADS // 2):
            x = qt_half[hd * LANES:(hd + 1) * LANES]
            ms = jnp.mean(x * x, axis=0, keepdims=True)
            x = (x * lax.rsqrt(ms + NORM_EPS)) * qgain
            x = x * cos_t + _swap_rows(x, 0, B_HEAD_DIM // 4, 2) * sin_t
            out_rows = slice((first_head + hd) * LANES, (first_head + hd + 1) * LANES)
            qt_ref[out_rows, :] = x.astype(BF16)

    kk = jnp.dot(xn, wk_ref[...], preferred_element_type=F32)
    qt_a = _nt_dot(wqt_ref[:half_rows, :], xn)
    for j in range(B_KV_HEADS):
        cols = slice(j * LANES, (j + 1) * LANES)
        x = _rope_lanes(_rms(kk[:, cols], kn_ref[...]), cos, sa, sb, B_HEAD_DIM // 4)
        k_ref[:, cols] = x.astype(BF16)
    qt_b = _nt_dot(wqt_ref[half_rows:, :], xn)
    q_heads(qt_a, 0)
    vt = _nt_dot(wvt_ref[...], xn)
    q_heads(qt_b, B_HEADS // 2)
    vt_ref[...] = vt.astype(BF16)


def b_prep(h, gain, wq_t, wk, wv_t, qgain_t, knorm, tables_t, tables):
    t, d = h.shape
    nrow = SEQ // TM_ROW
    row = lambda i: (i, 0)
    fixed = lambda i: (0, 0)
    tab = lambda i: (i % nrow, 0)
    tab_t = lambda i: (0, i % nrow)
    colblk = lambda i: (i // nrow, 0, i % nrow)
    nq = B_HEADS * B_HEAD_DIM
    nkv = B_KV_HEADS * B_HEAD_DIM
    return pl.pallas_call(
        _b_prep_kernel,
        grid=(t // TM_ROW,),
        in_specs=[
            pl.BlockSpec((TM_ROW, d), row),
            pl.BlockSpec((1, d), fixed),
            pl.BlockSpec((nq, d), fixed),
            pl.BlockSpec((d, nkv), fixed),
            pl.BlockSpec((nkv, d), fixed),
            pl.BlockSpec((B_HEAD_DIM, TM_ROW), fixed),
            pl.BlockSpec((1, B_HEAD_DIM), fixed),
            pl.BlockSpec((LANES, TM_ROW), tab_t),
            pl.BlockSpec((LANES, TM_ROW), tab_t),
            pl.BlockSpec((TM_ROW, LANES), tab),
            pl.BlockSpec((TM_ROW, LANES), tab),
            pl.BlockSpec((TM_ROW, LANES), tab),
        ],
        out_specs=[
            pl.BlockSpec((None, nq, TM_ROW), colblk),
            pl.BlockSpec((TM_ROW, nkv), row),
            pl.BlockSpec((None, nkv, TM_ROW), colblk),
        ],
        out_shape=[
            jax.ShapeDtypeStruct((BATCH, nq, SEQ), BF16),
            jax.ShapeDtypeStruct((t, nkv), BF16),
            jax.ShapeDtypeStruct((BATCH, nkv, SEQ), BF16),
        ],
        compiler_params=_cparams(("parallel",)),
        name="b_prep",
    )(h, gain, wq_t, wk, wv_t, qgain_t, knorm, *tables_t, *tables)


def _c_prep_kernel(h_ref, g_ref, win_ref, qn_ref, kvn_ref, wuqt_ref, wuk_ref, wuvt_ref,
                   cost_ref, sint_ref, cos_ref, sa_ref, sb_ref, q_ref, k_ref, v_ref):
    xn = _rms(h_ref[...], g_ref[...]).astype(BF16)
    c = jnp.dot(xn, win_ref[...], preferred_element_type=F32)
    cq = _rms(c[:, :C_Q_LORA], qn_ref[...]).astype(BF16)
    ckv = _rms(c[:, C_Q_LORA:C_Q_LORA + C_KV_LORA], kvn_ref[...]).astype(BF16)
    cos, sa, sb = cos_ref[...], sa_ref[...], sb_ref[...]
    cos_t, sin_t = cost_ref[...], sint_ref[...]
    half_rows = C_HEADS * LANES // 2

    def q_heads(qt_half, first_head):
        for hd in range(C_HEADS // 2):
            x = qt_half[hd * LANES:(hd + 1) * LANES]
            x = x * cos_t + _swap_rows(x, C_NOPE, C_ROPE // 2, 1) * sin_t
            out_rows = slice((first_head + hd) * LANES, (first_head + hd + 1) * LANES)
            q_ref[out_rows, :] = x.astype(BF16)

    kn = jnp.dot(ckv, wuk_ref[...], preferred_element_type=F32)
    qt_a = _nt_dot(wuqt_ref[:half_rows, :], cq)
    kr = _rope_lanes(c[:, C_Q_LORA + C_KV_LORA:], cos, sa, sb, C_ROPE // 2)
    for hd in range(C_HEADS):
        cols = slice(hd * LANES, (hd + 1) * LANES)
        k_ref[:, cols] = (kn[:, cols] + kr).astype(BF16)
    qt_b = _nt_dot(wuqt_ref[half_rows:, :], cq)
    q_heads(qt_a, 0)
    vt = _nt_dot(wuvt_ref[...], ckv)
    q_heads(qt_b, C_HEADS // 2)
    v_ref[...] = vt.astype(BF16)


def c_prep(h, gain, win, qnorm, kvnorm, wuq_t, wuk, wuv_t, tables_t, tables):
    t, d = h.shape
    nrow = SEQ // TM_ROW
    row = lambda i: (i, 0)
    fixed = lambda i: (0, 0)
    tab = lambda i: (i % nrow, 0)
    tab_t = lambda i: (0, i % nrow)
    colblk = lambda i: (i // nrow, 0, i % nrow)
    hq = C_HEADS * LANES
    return pl.pallas_call(
        _c_prep_kernel,
        grid=(t // TM_ROW,),
        in_specs=[
            pl.BlockSpec((TM_ROW, d), row),
            pl.BlockSpec((1, d), fixed),
            pl.BlockSpec((d, C_IN_PAD), fixed),
            pl.BlockSpec((1, C_Q_LORA), fixed),
            pl.BlockSpec((1, C_KV_LORA), fixed),
            pl.BlockSpec((hq, C_Q_LORA), fixed),
            pl.BlockSpec((C_KV_LORA, hq), fixed),
            pl.BlockSpec((C_HEADS * C_V, C_KV_LORA), fixed),
            pl.BlockSpec((LANES, TM_ROW), tab_t),
            pl.BlockSpec((LANES, TM_ROW), tab_t),
            pl.BlockSpec((TM_ROW, LANES), tab),
            pl.BlockSpec((TM_ROW, LANES), tab),
            pl.BlockSpec((TM_ROW, LANES), tab),
        ],
        out_specs=[
            pl.BlockSpec((None, hq, TM_ROW), colblk),
            pl.BlockSpec((TM_ROW, hq), row),
            pl.BlockSpec((None, C_HEADS * C_V, TM_ROW), colblk),
        ],
        out_shape=[
            jax.ShapeDtypeStruct((BATCH, hq, SEQ), BF16),
            jax.ShapeDtypeStruct((t, hq), BF16),
            jax.ShapeDtypeStruct((BATCH, C_HEADS * C_V, SEQ), BF16),
        ],
        compiler_params=_cparams(("parallel",)),
        name="c_prep",
    )(h, gain, win, qnorm, kvnorm, wuq_t, wuk, wuv_t, *tables_t, *tables)


def _mix_ffn_kernel(y_ref, wo_ref, gmix_ref, h_ref, gpre_ref, wg_ref, wu_ref, wd_ref, gpost_ref,
                    o_ref):
    sub = TM_ROW // FFN_SPLIT
    parts = [slice(i * sub, (i + 1) * sub) for i in range(FFN_SPLIT)]
    z = [jnp.dot(y_ref[p, :], wo_ref[...], preferred_element_type=F32) for p in parts]
    h1, gate, up, down = [], [], [], []
    for i, p in enumerate(parts):
        h1.append(h_ref[p, :] + _rms(z[i], gmix_ref[...]))
        xn = _rms(h1[i], gpre_ref[...]).astype(BF16)
        gate.append(jnp.dot(xn, wg_ref[...], preferred_element_type=F32))
        up.append(jnp.dot(xn, wu_ref[...], preferred_element_type=F32))
    for i in range(FFN_SPLIT):
        act = (gate[i] * jax.nn.sigmoid(gate[i])) * up[i]
        down.append(jnp.dot(act.astype(BF16), wd_ref[...], preferred_element_type=F32))
    for i, p in enumerate(parts):
        o_ref[p, :] = h1[i] + _rms(down[i], gpost_ref[...])


def mix_ffn(y, wo, gmix, h, gpre, wg, wu, wd, layer, gpost):
    t, d = h.shape
    row = lambda i: (i, 0)
    fixed = lambda i: (0, 0)
    layer_blk = lambda i: (layer, 0, 0)
    once = pl.Buffered(1)
    return pl.pallas_call(
        _mix_ffn_kernel,
        grid=(t // TM_ROW,),
        in_specs=[
            pl.BlockSpec((TM_ROW, d), row),
            pl.BlockSpec((d, d), fixed, pipeline_mode=once),
            pl.BlockSpec((1, d), fixed),
            pl.BlockSpec((TM_ROW, d), row),
            pl.BlockSpec((1, d), fixed),
            pl.BlockSpec((None, d, D_FF), layer_blk, pipeline_mode=once),
            pl.BlockSpec((None, d, D_FF), layer_blk, pipeline_mode=once),
            pl.BlockSpec((None, D_FF, d), layer_blk, pipeline_mode=once),
            pl.BlockSpec((1, d), fixed),
        ],
        out_specs=pl.BlockSpec((TM_ROW, d), row),
        out_shape=jax.ShapeDtypeStruct((t, d), F32),
        compiler_params=_cparams(("parallel",)),
        name="mix_ffn",
    )(y, wo, gmix, h, gpre, wg, wu, wd, gpost)


def _angles(pos, n_freq, dim, theta):
    freqs = jnp.power(jnp.float32(theta), -jnp.arange(n_freq, dtype=F32) * 2.0 / dim)
    ang = freqs[:, None] * pos[None, :]
    return lax.optimization_barrier((jnp.cos(ang), jnp.sin(ang)))


def _b_tables():
    quarter = B_HEAD_DIM // 4
    rows = SEQ // GRID_W
    rc, rs = _angles(jnp.arange(rows, dtype=F32), quarter, B_HEAD_DIM // 2, B_ROPE_THETA)
    cc, cs = _angles(jnp.arange(GRID_W, dtype=F32), quarter, B_HEAD_DIM // 2, B_ROPE_THETA)
    by_row = lambda x: jnp.repeat(x, GRID_W, axis=1)
    by_col = lambda x: jnp.tile(x, (1, rows))
    rc, rs, cc, cs = by_row(rc), by_row(rs), by_col(cc), by_col(cs)
    zero = jnp.zeros_like(rs)
    cos_t = jnp.concatenate([rc, rc, cc, cc], axis=0)
    sin_t = jnp.concatenate([-rs, rs, -cs, cs], axis=0)
    sin_a = jnp.concatenate([-rs, zero, -cs, zero], axis=0)
    sin_b = jnp.concatenate([zero, rs, zero, cs], axis=0)
    scale = B_HEAD_DIM ** -0.5 * LOG2E
    return (cos_t * scale, sin_t * scale), (cos_t.T, sin_a.T, sin_b.T)


def _c_tables():
    half = C_ROPE // 2
    c, s = _angles(jnp.arange(SEQ, dtype=F32), half, C_ROPE, C_ROPE_THETA)
    one_lo = jnp.ones((C_NOPE, SEQ), F32)
    one_hi = jnp.ones((LANES - C_NOPE - C_ROPE, SEQ), F32)
    zero = jnp.zeros_like(s)
    cos_t = jnp.concatenate([one_lo, c, c, one_hi], axis=0)
    sin_t = jnp.concatenate([0 * one_lo, -s, s, 0 * one_hi], axis=0)
    sin_a = jnp.concatenate([0 * one_lo, -s, zero, 0 * one_hi], axis=0)
    sin_b = jnp.concatenate([0 * one_lo, zero, s, 0 * one_hi], axis=0)
    scale = (C_NOPE + C_ROPE) ** -0.5 * LOG2E
    return (cos_t * scale, sin_t * scale), (cos_t.T, sin_a.T, sin_b.T)


def _alibi_slopes():
    n = A_NG * A_HEADS
    return jnp.asarray(2.0 ** (-8.0 * np.arange(1, n + 1) / n), dtype=F32)


def _mixer_a(h, gain, wqkv_stack, layer):
    qkv = norm_matmul(h, gain, wqkv_stack, layer)
    return a_attention(qkv.reshape(BATCH, SEQ, A_QKV), _alibi_slopes())


def _mixer_b(h, gain, wqkv, qnorm, knorm):
    nq = B_HEADS * B_HEAD_DIM
    nkv = B_KV_HEADS * B_HEAD_DIM
    w = wqkv.astype(BF16)
    qgain_t = jnp.broadcast_to(qnorm[:, None], (B_HEAD_DIM, TM_ROW))
    tables_t, tables = _b_tables()
    qt, k, vt = b_prep(h, gain, w[:, :nq].T, w[:, nq:nq + nkv], w[:, nq + nkv:].T, qgain_t,
                       knorm[None, :], tables_t, tables)
    return b_attention(qt, k.reshape(BATCH, SEQ, -1), vt)


def _mixer_c(h, gain, win, qnorm, kvnorm, wuq, wukv):
    win_p = jnp.zeros((D_MODEL, C_IN_PAD), F32)
    win_p = win_p.at[:, :C_Q_LORA + C_KV_LORA].set(win[:, :C_Q_LORA + C_KV_LORA])
    kr0 = C_Q_LORA + C_KV_LORA + C_NOPE
    win_p = win_p.at[:, kr0:kr0 + C_ROPE].set(win[:, C_Q_LORA + C_KV_LORA:])
    wuq_p = jnp.pad(wuq.reshape(C_Q_LORA, C_HEADS, C_NOPE + C_ROPE),
                    ((0, 0), (0, 0), (0, LANES - C_NOPE - C_ROPE))).reshape(C_Q_LORA, C_HEADS * LANES)
    wukv3 = wukv.reshape(C_KV_LORA, C_HEADS, C_NOPE + C_V)
    wuk_p = jnp.pad(wukv3[:, :, :C_NOPE],
                    ((0, 0), (0, 0), (0, LANES - C_NOPE))).reshape(C_KV_LORA, C_HEADS * LANES)
    wuv = wukv3[:, :, C_NOPE:].reshape(C_KV_LORA, C_HEADS * C_V)
    tables_t, tables = _c_tables()
    qt, k, vt = c_prep(h, gain, win_p.astype(BF16), qnorm[None, :], kvnorm[None, :],
                       wuq_p.astype(BF16).T, wuk_p.astype(BF16), wuv.astype(BF16).T,
                       tables_t, tables)
    return c_attention(qt, k.reshape(BATCH, SEQ, -1), vt)


def kernel(x, norm_mix_pre, norm_mix_post, norm_ffn_pre, norm_ffn_post, ffn_wg, ffn_wu, ffn_wd,
           a_wqkv, a_wo, b_wqkv, b_qnorm, b_knorm, b_wo,
           c_win, c_qnorm, c_kvnorm, c_wuq, c_wukv, c_wo):
    h = x.reshape(BATCH * SEQ, D_MODEL)
    a_wqkv_b = a_wqkv.astype(BF16)
    wg_b, wu_b, wd_b = ffn_wg.astype(BF16), ffn_wu.astype(BF16), ffn_wd.astype(BF16)
    for i in range(DEPTH):
        kind = i % N_MIXERS
        j = i // N_MIXERS
        gpre = norm_mix_pre[i][None, :]
        if kind == 0:
            y, wo = _mixer_a(h, gpre, a_wqkv_b, j), a_wo[j]
        elif kind == 1:
            y, wo = _mixer_b(h, gpre, b_wqkv[j], b_qnorm[j], b_knorm[j]), b_wo[j]
        else:
            y, wo = _mixer_c(h, gpre, c_win[j], c_qnorm[j], c_kvnorm[j], c_wuq[j], c_wukv[j]), c_wo[j]
        h = mix_ffn(y.reshape(BATCH * SEQ, D_MODEL), wo.astype(BF16), norm_mix_post[i][None, :], h,
                    norm_ffn_pre[i][None, :], wg_b, wu_b, wd_b, i, norm_ffn_post[i][None, :])
    return h.reshape(BATCH, SEQ, D_MODEL)
```

```python
import functools

import numpy as np
import jax
import jax.numpy as jnp
from jax import lax
from jax.experimental import pallas as pl
from jax.experimental.pallas import tpu as pltpu

F32 = jnp.float32
BF16 = jnp.bfloat16

D_MODEL = 1024
BATCH = 2
SEQ = 8192
DEPTH = 4
N_MIXERS = 3
GRID_W = 64
D_FF = 2816
NORM_EPS = 1e-6
NEG_INF = -1e30

A_GROUPS = ((128, 1), (512, 4), (2048, 16))
A_NG = 3
A_HEAD_DIM = 64
A_HEADS = 16
A_QKV = 3 * A_NG * A_HEADS * A_HEAD_DIM
A_RADIUS = 64

B_HEAD_DIM = 128
B_HEADS = 8
B_KV_HEADS = 2
B_ROPE_THETA = 10000.0
B_QKV = (B_HEADS + 2 * B_KV_HEADS) * B_HEAD_DIM

C_HEADS = 16
C_Q_LORA = 384
C_KV_LORA = 256
C_NOPE = 64
C_ROPE = 32
C_V = 64
C_ROPE_THETA = 10000.0
C_IN_PAD = 768

LANES = 128
ONES_ROWS = 16
C_ONES_ROWS = LANES - C_V
LOG2E = 1.4426950408889634
VMEM_LIMIT = 56 * 1024 * 1024

TM_PROJ = 256
TN_PROJ = 1024
TM_ROW = 512
FFN_SPLIT = 2
TQ = 512
TK = 512
FLASH_UNROLL = 4
FLASH_HEADS = 4
A_QB = 512
A_W = A_QB + 2 * A_RADIUS
A_HB = 128
A_HW = A_HB + 2 * A_RADIUS
A_UNROLL = 2


def _cparams(sem):
    return pltpu.CompilerParams(dimension_semantics=sem, vmem_limit_bytes=VMEM_LIMIT)


def _rms(x, gain):
    ms = jnp.mean(x * x, axis=-1, keepdims=True)
    return (x * lax.rsqrt(ms + NORM_EPS)) * gain


def _norm_matmul_kernel(x_ref, g_ref, w_ref, o_ref):
    xn = _rms(x_ref[...], g_ref[...]).astype(BF16)
    for c in range(o_ref.shape[1] // TN_PROJ):
        cols = slice(c * TN_PROJ, (c + 1) * TN_PROJ)
        o_ref[:, cols] = jnp.dot(xn, w_ref[:, cols], preferred_element_type=F32)


def norm_matmul(h, gain, w_stack, layer):
    t, d = h.shape
    n = w_stack.shape[2]
    return pl.pallas_call(
        _norm_matmul_kernel,
        grid=(t // TM_PROJ,),
        in_specs=[
            pl.BlockSpec((TM_PROJ, d), lambda i: (i, 0)),
            pl.BlockSpec((1, d), lambda i: (0, 0)),
            pl.BlockSpec((None, d, n), lambda i: (layer, 0, 0), pipeline_mode=pl.Buffered(1)),
        ],
        out_specs=pl.BlockSpec((TM_PROJ, n), lambda i: (i, 0)),
        out_shape=jax.ShapeDtypeStruct((t, n), F32),
        compiler_params=pltpu.CompilerParams(dimension_semantics=("parallel",), vmem_limit_bytes=VMEM_LIMIT,
                                             allow_input_fusion=[False, False, True]),
        name="a_qkv_proj",
    )(h, gain, w_stack)


def _a_attn_kernel(slopes_ref, q_ref, k_ref, v_ref, o_ref, m_ref, l_ref, acc_ref,
                   s_ref, mf_ref, bias_ref, kbuf_ref, vbuf_ref):
    hp = pl.program_id(1)
    g = pl.program_id(2)
    lane = lax.broadcasted_iota(jnp.int32, (A_QB, LANES), 1)
    lo = lane < A_HEAD_DIM
    lo_part = lax.broadcasted_iota(jnp.int32, (A_HB, LANES), 1) < A_HEAD_DIM
    part_lanes = (lo_part, jnp.logical_not(lo_part))
    rel = (lax.broadcasted_iota(jnp.int32, (A_HB, A_HW), 1)
           - lax.broadcasted_iota(jnp.int32, (A_HB, A_HW), 0))

    def group_body(step, gi, d):
        first = step == 0
        final = step == A_NG - 1
        cls_len = SEQ // d
        nblk = cls_len // A_QB
        nit = d * nblk
        win = min(A_W, cls_len)
        lo_w = lax.broadcasted_iota(jnp.int32, (win, LANES), 1) < A_HEAD_DIM
        head_lanes_w = (lo_w, jnp.logical_not(lo_w))

        for oi in range(3):
            dist = jnp.abs(rel - oi * A_RADIUS)
            for h2 in range(2):
                slope = slopes_ref[gi * A_HEADS + hp * 2 + h2] * (d * LOG2E)
                bias_ref[oi, h2] = jnp.where(dist <= A_RADIUS, -slope * dist.astype(F32), NEG_INF)

        def geometry(it):
            r = it // nblk
            i0 = (it % nblk) * A_QB
            if isinstance(it, int):
                kstart = min(max(i0 - A_RADIUS, 0), cls_len - win)
            else:
                kstart = jnp.clip(i0 - A_RADIUS, 0, cls_len - win)
            oi = (i0 - kstart) // A_RADIUS
            if d == 1 and isinstance(it, int):
                q_rows = pl.ds(i0, A_QB)
                k_rows = pl.ds(kstart, win)
            elif d == 1:
                q_rows = pl.ds(pl.multiple_of(i0, A_QB), A_QB)
                k_rows = pl.ds(pl.multiple_of(kstart, A_RADIUS), win)
            else:
                q_rows = pl.ds(r + i0 * d, A_QB, stride=d)
                k_rows = pl.ds(r + kstart * d, win, stride=d)
            return q_rows, k_rows, oi

        def half_window(t, oi):
            lo_row = t * A_HB + (oi - 1) * A_RADIUS
            if isinstance(oi, int):
                start = min(max(lo_row, 0), win - A_HW)
            else:
                start = pl.multiple_of(jnp.clip(lo_row, 0, win - A_HW), A_RADIUS)
            return pl.ds(start, A_HW), oi + t * (A_HB // A_RADIUS) - start // A_RADIUS

        def load_qk(it):
            q_rows, k_rows, oi = geometry(it)
            kbuf_ref[:win, :] = k_ref[k_rows, :].astype(BF16)
            return (q_ref[q_rows, :] * (A_HEAD_DIM ** -0.5 * LOG2E)).astype(BF16), oi

        def scores(qk, slot, parts):
            q, oi = qk
            for t in parts:
                rows = slice(t * A_HB, (t + 1) * A_HB)
                k_rows, boff = half_window(t, oi)
                qt = q[rows]
                qstack = jnp.concatenate([jnp.where(m, qt, jnp.zeros_like(qt)) for m in part_lanes], axis=0)
                s2 = lax.dot_general(qstack, kbuf_ref[k_rows, :], (((1,), (1,)), ((), ())),
                                     preferred_element_type=F32)
                for h2 in range(2):
                    s = s2[h2 * A_HB:(h2 + 1) * A_HB] + bias_ref[boff, h2]
                    s_ref[slot, h2, rows] = s
                    mf_ref[slot, h2, rows] = jnp.broadcast_to(jnp.max(s, axis=1, keepdims=True),
                                                               (A_HB, LANES))

        def load_v(k_rows):
            v = v_ref[k_rows, :]
            for h2 in range(2):
                vbuf_ref[h2, :win, :] = jnp.where(head_lanes_w[h2], v, 1.0).astype(BF16)

        def weighted_values(oi, slot, parts, outs):
            for t in parts:
                rows = slice(t * A_HB, (t + 1) * A_HB)
                k_rows, _ = half_window(t, oi)
                ps = []
                for h2 in range(2):
                    mfull = mf_ref[slot, h2, rows]
                    ps.append(jnp.exp2(s_ref[slot, h2, rows]
                                       - jnp.concatenate([mfull] * (A_HW // LANES), axis=1)).astype(BF16))
                values = jnp.concatenate([vbuf_ref[0, k_rows, :], vbuf_ref[1, k_rows, :]], axis=1)
                o2 = jnp.dot(jnp.concatenate(ps, axis=0), values, preferred_element_type=F32)
                for h2 in range(2):
                    outs[h2].append(o2[h2 * A_HB:(h2 + 1) * A_HB, h2 * LANES:(h2 + 1) * LANES])

        def merge(q_rows, slot, outs):
            m_blk = jnp.where(lo, mf_ref[slot, 0], mf_ref[slot, 1])
            pv_blk = jnp.where(lo, outs[0], outs[1])
            l_blk = pltpu.roll(jnp.where(lo, outs[1], outs[0]), A_HEAD_DIM, 1)
            if first:
                m_ref[q_rows, :] = m_blk
                l_ref[q_rows, :] = l_blk
                acc_ref[q_rows, :] = pv_blk
            else:
                m_old = m_ref[q_rows, :]
                m_new = jnp.maximum(m_old, m_blk)
                a_old = jnp.exp2(m_old - m_new)
                a_blk = jnp.exp2(m_blk - m_new)
                l_new = a_old * l_ref[q_rows, :] + a_blk * l_blk
                acc_new = a_old * acc_ref[q_rows, :] + a_blk * pv_blk
                if final:
                    acc_ref[q_rows, :] = acc_new / l_new
                else:
                    m_ref[q_rows, :] = m_new
                    l_ref[q_rows, :] = l_new
                    acc_ref[q_rows, :] = acc_new

        part_groups = ((0, 1), (2, 3))
        qk0 = load_qk(0)
        for parts in part_groups:
            scores(qk0, 0, parts)

        def blocks(it0, last):
            for u in range(A_UNROLL):
                has_next = not (last and u == A_UNROLL - 1)
                qk = load_qk(it0 + u + 1) if has_next else None
                q_rows, k_rows, oi = geometry(it0 + u)
                load_v(k_rows)
                outs = ([], [])
                for parts in part_groups:
                    if has_next:
                        scores(qk, (u + 1) % 2, parts)
                    weighted_values(oi, u % 2, parts, outs)
                merge(q_rows, u % 2, [jnp.concatenate(o, axis=0) for o in outs])

        def body(jj, carry):
            blocks(A_UNROLL * jj, False)
            return carry

        lax.fori_loop(0, nit // A_UNROLL - 1, body, 0)
        blocks(nit - A_UNROLL, True)
        if final:
            o_ref[...] = acc_ref[...].astype(o_ref.dtype)

    for step in range(A_NG):
        gi = A_NG - 1 - step
        pl.when(g == step)(functools.partial(group_body, step, gi, A_GROUPS[gi][1]))


def a_attention(qkv, slopes):
    npairs = A_HEADS // 2

    def col(which):
        return lambda b, hp, g: (b, 0, (which * A_NG + (A_NG - 1 - g)) * npairs + hp)

    return pl.pallas_call(
        _a_attn_kernel,
        grid=(BATCH, npairs, A_NG),
        in_specs=[
            pl.BlockSpec(memory_space=pltpu.SMEM),
            pl.BlockSpec((None, SEQ, LANES), col(0)),
            pl.BlockSpec((None, SEQ, LANES), col(1)),
            pl.BlockSpec((None, SEQ, LANES), col(2)),
        ],
        out_specs=pl.BlockSpec((None, SEQ, LANES), lambda b, hp, g: (b, 0, hp)),
        out_shape=jax.ShapeDtypeStruct((BATCH, SEQ, D_MODEL), BF16),
        scratch_shapes=[
            pltpu.VMEM((SEQ, LANES), F32),
            pltpu.VMEM((SEQ, LANES), F32),
            pltpu.VMEM((SEQ, LANES), F32),
            pltpu.VMEM((2, 2, A_QB, A_HW), F32),
            pltpu.VMEM((2, 2, A_QB, LANES), F32),
            pltpu.VMEM((3, 2, A_HB, A_HW), F32),
            pltpu.VMEM((A_W, LANES), BF16),
            pltpu.VMEM((2, A_W, LANES), BF16),
        ],
        compiler_params=_cparams(("parallel", "parallel", "arbitrary")),
        name="a_attention",
    )(slopes, qkv, qkv, qkv)


def _flash_kernel(qt_ref, k_ref, vt_ref, o_ref, m_ref, acc_ref, s_ref, mc_ref, *, nh, dv, shared_kv):
    m_ref[...] = jnp.full(m_ref.shape, -jnp.inf, F32)
    acc_ref[...] = jnp.zeros(acc_ref.shape, F32)
    ones = jnp.ones((acc_ref.shape[1] - dv, TK), BF16)
    nchunk = SEQ // TK

    def chunk_rows(chunk):
        start = chunk * TK
        return pl.ds(start if isinstance(start, int) else pl.multiple_of(start, TK), TK)

    def scores(chunk, slot, h):
        rows = chunk_rows(chunk)
        k = k_ref[rows, :] if shared_kv else k_ref[rows, h * LANES:(h + 1) * LANES]
        st = jnp.dot(k, qt_ref[h * LANES:(h + 1) * LANES, :], preferred_element_type=F32)
        s_ref[slot, h] = st
        mc_ref[slot, h] = jnp.max(st, axis=0, keepdims=True)

    def update(chunk, slot, h):
        rows = chunk_rows(chunk)
        vt = vt_ref[:, rows] if shared_kv else vt_ref[h * dv:(h + 1) * dv, rows]
        m_prev = m_ref[h]
        m_new = jnp.maximum(m_prev, mc_ref[slot, h])
        alpha = jnp.exp2(m_prev - m_new)
        pt = jnp.exp2(s_ref[slot, h] - m_new).astype(BF16)
        pv = jnp.dot(jnp.concatenate([vt, ones], axis=0), pt, preferred_element_type=F32)
        acc_ref[h] = alpha * acc_ref[h] + pv
        m_ref[h] = m_new

    for h in range(nh):
        scores(0, 0, h)

    def chunks(j, last):
        for u in range(FLASH_UNROLL):
            for h in range(nh):
                if not (last and u == FLASH_UNROLL - 1):
                    scores(j + u + 1, (u + 1) % 2, h)
                update(j + u, u % 2, h)

    def body(jj, carry):
        chunks(FLASH_UNROLL * jj, False)
        return carry

    lax.fori_loop(0, nchunk // FLASH_UNROLL - 1, body, 0)
    chunks(nchunk - FLASH_UNROLL, True)

    outs = []
    for h in range(nh):
        acc = acc_ref[h]
        outs.append(acc[:dv, :] / acc[dv:dv + 1, :])
    ot = outs[0] if nh == 1 else jnp.concatenate(outs, axis=0)
    o_ref[...] = ot.T.astype(o_ref.dtype)


def _flash_call(kernel, grid, nh, acc_rows, in_specs, out_spec, args, name):
    return pl.pallas_call(
        kernel,
        grid=grid,
        in_specs=in_specs,
        out_specs=out_spec,
        out_shape=jax.ShapeDtypeStruct((BATCH, SEQ, D_MODEL), BF16),
        scratch_shapes=[
            pltpu.VMEM((nh, 1, TQ), F32),
            pltpu.VMEM((nh, acc_rows, TQ), F32),
            pltpu.VMEM((2, nh, TK, TQ), F32),
            pltpu.VMEM((2, nh, 1, TQ), F32),
        ],
        compiler_params=_cparams(("parallel",) * (len(grid) - 1) + ("arbitrary",)),
        name=name,
    )(*args)


def b_attention(qt, k, vt):
    grp = B_HEADS // B_KV_HEADS
    nh = FLASH_HEADS
    kernel = functools.partial(_flash_kernel, nh=nh, dv=B_HEAD_DIM, shared_kv=True)
    in_specs = [
        pl.BlockSpec((None, nh * LANES, TQ), lambda b, kv, g, i: (b, kv * (grp // nh) + g, i)),
        pl.BlockSpec((None, SEQ, LANES), lambda b, kv, g, i: (b, 0, kv)),
        pl.BlockSpec((None, LANES, SEQ), lambda b, kv, g, i: (b, kv, 0)),
    ]
    out_spec = pl.BlockSpec((None, TQ, nh * LANES), lambda b, kv, g, i: (b, i, kv * (grp // nh) + g))
    grid = (BATCH, B_KV_HEADS, grp // nh, SEQ // TQ)
    return _flash_call(kernel, grid, nh, B_HEAD_DIM + ONES_ROWS, in_specs, out_spec, (qt, k, vt),
                       "b_attention")


def c_attention(qt, k, vt):
    nh = FLASH_HEADS
    kernel = functools.partial(_flash_kernel, nh=nh, dv=C_V, shared_kv=False)
    in_specs = [
        pl.BlockSpec((None, nh * LANES, TQ), lambda b, p, i: (b, p, i)),
        pl.BlockSpec((None, SEQ, nh * LANES), lambda b, p, i: (b, 0, p)),
        pl.BlockSpec((None, nh * C_V, SEQ), lambda b, p, i: (b, p, 0)),
    ]
    out_spec = pl.BlockSpec((None, TQ, nh * C_V), lambda b, p, i: (b, i, p))
    grid = (BATCH, C_HEADS // nh, SEQ // TQ)
    return _flash_call(kernel, grid, nh, C_V + C_ONES_ROWS, in_specs, out_spec, (qt, k, vt),
                       "c_attention")


def _rope_lanes(x, cos, sin_a, sin_b, shift):
    return (x * cos + pltpu.roll(x, LANES - shift, 1) * sin_a + pltpu.roll(x, shift, 1) * sin_b)


def _nt_dot(a, b):
    return lax.dot_general(a, b, (((1,), (1,)), ((), ())), preferred_element_type=F32)


def _swap_rows(x, start, half, groups):
    pieces = [x[:start]] if start else []
    for g0 in range(start, start + 2 * half * groups, 2 * half):
        pieces += [x[g0 + half:g0 + 2 * half], x[g0:g0 + half]]
    if start + 2 * half * groups < x.shape[0]:
        pieces.append(x[start + 2 * half * groups:])
    return jnp.concatenate(pieces, axis=0)


def _b_prep_kernel(h_ref, g_ref, wqt_ref, wk_ref, wvt_ref, qg_ref, kn_ref, cost_ref, sint_ref,
                   cos_ref, sa_ref, sb_ref, qt_ref, k_ref, vt_ref):
    xn = _rms(h_ref[...], g_ref[...]).astype(BF16)
    cos_t, sin_t, qgain = cost_ref[...], sint_ref[...], qg_ref[...]
    cos, sa, sb = cos_ref[...], sa_ref[...], sb_ref[...]
    half_rows = B_HEADS * LANES // 2

    def q_heads(qt_half, first_head):
        for hd in range(B_HEADS // 2):
            x = qt_half[hd * LANES:(hd + 1) * LANES]
            ms = jnp.mean(x * x, axis=0, keepdims=True)
            x = (x * lax.rsqrt(ms + NORM_EPS)) * qgain
            x = x * cos_t + _swap_rows(x, 0, B_HEAD_DIM // 4, 2) * sin_t
            out_rows = slice((first_head + hd) * LANES, (first_head + hd + 1) * LANES)
            qt_ref[out_rows, :] = x.astype(BF16)

    kk = jnp.dot(xn, wk_ref[...], preferred_element_type=F32)
    qt_a = _nt_dot(wqt_ref[:half_rows, :], xn)
    for j in range(B_KV_HEADS):
        cols = slice(j * LANES, (j + 1) * LANES)
        x = _rope_lanes(_rms(kk[:, cols], kn_ref[...]), cos, sa, sb, B_HEAD_DIM // 4)
        k_ref[:, cols] = x.astype(BF16)
    qt_b = _nt_dot(wqt_ref[half_rows:, :], xn)
    q_heads(qt_a, 0)
    vt = _nt_dot(wvt_ref[...], xn)
    q_heads(qt_b, B_HEADS // 2)
    vt_ref[...] = vt.astype(BF16)


def b_prep(h, gain, wq_t, wk, wv_t, qgain_t, knorm, tables_t, tables):
    t, d = h.shape
    nrow = SEQ // TM_ROW
    row = lambda i: (i, 0)
    fixed = lambda i: (0, 0)
    tab = lambda i: (i % nrow, 0)
    tab_t = lambda i: (0, i % nrow)
    colblk = lambda i: (i // nrow, 0, i % nrow)
    nq = B_HEADS * B_HEAD_DIM
    nkv = B_KV_HEADS * B_HEAD_DIM
    return pl.pallas_call(
        _b_prep_kernel,
        grid=(t // TM_ROW,),
        in_specs=[
            pl.BlockSpec((TM_ROW, d), row),
            pl.BlockSpec((1, d), fixed),
            pl.BlockSpec((nq, d), fixed),
            pl.BlockSpec((d, nkv), fixed),
            pl.BlockSpec((nkv, d), fixed),
            pl.BlockSpec((B_HEAD_DIM, TM_ROW), fixed),
            pl.BlockSpec((1, B_HEAD_DIM), fixed),
            pl.BlockSpec((LANES, TM_ROW), tab_t),
            pl.BlockSpec((LANES, TM_ROW), tab_t),
            pl.BlockSpec((TM_ROW, LANES), tab),
            pl.BlockSpec((TM_ROW, LANES), tab),
            pl.BlockSpec((TM_ROW, LANES), tab),
        ],
        out_specs=[
            pl.BlockSpec((None, nq, TM_ROW), colblk),
            pl.BlockSpec((TM_ROW, nkv), row),
            pl.BlockSpec((None, nkv, TM_ROW), colblk),
        ],
        out_shape=[
            jax.ShapeDtypeStruct((BATCH, nq, SEQ), BF16),
            jax.ShapeDtypeStruct((t, nkv), BF16),
            jax.ShapeDtypeStruct((BATCH, nkv, SEQ), BF16),
        ],
        compiler_params=_cparams(("parallel",)),
        name="b_prep",
    )(h, gain, wq_t, wk, wv_t, qgain_t, knorm, *tables_t, *tables)


def _c_prep_kernel(h_ref, g_ref, win_ref, qn_ref, kvn_ref, wuqt_ref, wuk_ref, wuvt_ref,
                   cost_ref, sint_ref, cos_ref, sa_ref, sb_ref, q_ref, k_ref, v_ref):
    xn = _rms(h_ref[...], g_ref[...]).astype(BF16)
    c = jnp.dot(xn, win_ref[...], preferred_element_type=F32)
    cq = _rms(c[:, :C_Q_LORA], qn_ref[...]).astype(BF16)
    ckv = _rms(c[:, C_Q_LORA:C_Q_LORA + C_KV_LORA], kvn_ref[...]).astype(BF16)
    cos, sa, sb = cos_ref[...], sa_ref[...], sb_ref[...]
    cos_t, sin_t = cost_ref[...], sint_ref[...]
    half_rows = C_HEADS * LANES // 2

    def q_heads(qt_half, first_head):
        for hd in range(C_HEADS // 2):
            x = qt_half[hd * LANES:(hd + 1) * LANES]
            x = x * cos_t + _swap_rows(x, C_NOPE, C_ROPE // 2, 1) * sin_t
            out_rows = slice((first_head + hd) * LANES, (first_head + hd + 1) * LANES)
            q_ref[out_rows, :] = x.astype(BF16)

    kn = jnp.dot(ckv, wuk_ref[...], preferred_element_type=F32)
    qt_a = _nt_dot(wuqt_ref[:half_rows, :], cq)
    kr = _rope_lanes(c[:, C_Q_LORA + C_KV_LORA:], cos, sa, sb, C_ROPE // 2)
    for hd in range(C_HEADS):
        cols = slice(hd * LANES, (hd + 1) * LANES)
        k_ref[:, cols] = (kn[:, cols] + kr).astype(BF16)
    qt_b = _nt_dot(wuqt_ref[half_rows:, :], cq)
    q_heads(qt_a, 0)
    vt = _nt_dot(wuvt_ref[...], ckv)
    q_heads(qt_b, C_HEADS // 2)
    v_ref[...] = vt.astype(BF16)


def c_prep(h, gain, win, qnorm, kvnorm, wuq_t, wuk, wuv_t, tables_t, tables):
    t, d = h.shape
    nrow = SEQ // TM_ROW
    row = lambda i: (i, 0)
    fixed = lambda i: (0, 0)
    tab = lambda i: (i % nrow, 0)
    tab_t = lambda i: (0, i % nrow)
    colblk = lambda i: (i // nrow, 0, i % nrow)
    hq = C_HEADS * LANES
    return pl.pallas_call(
        _c_prep_kernel,
        grid=(t // TM_ROW,),
        in_specs=[
            pl.BlockSpec((TM_ROW, d), row),
            pl.BlockSpec((1, d), fixed),
            pl.BlockSpec((d, C_IN_PAD), fixed),
            pl.BlockSpec((1, C_Q_LORA), fixed),
            pl.BlockSpec((1, C_KV_LORA), fixed),
            pl.BlockSpec((hq, C_Q_LORA), fixed),
            pl.BlockSpec((C_KV_LORA, hq), fixed),
            pl.BlockSpec((C_HEADS * C_V, C_KV_LORA), fixed),
            pl.BlockSpec((LANES, TM_ROW), tab_t),
            pl.BlockSpec((LANES, TM_ROW), tab_t),
            pl.BlockSpec((TM_ROW, LANES), tab),
            pl.BlockSpec((TM_ROW, LANES), tab),
            pl.BlockSpec((TM_ROW, LANES), tab),
        ],
        out_specs=[
            pl.BlockSpec((None, hq, TM_ROW), colblk),
            pl.BlockSpec((TM_ROW, hq), row),
            pl.BlockSpec((None, C_HEADS * C_V, TM_ROW), colblk),
        ],
        out_shape=[
            jax.ShapeDtypeStruct((BATCH, hq, SEQ), BF16),
            jax.ShapeDtypeStruct((t, hq), BF16),
            jax.ShapeDtypeStruct((BATCH, C_HEADS * C_V, SEQ), BF16),
        ],
        compiler_params=_cparams(("parallel",)),
        name="c_prep",
    )(h, gain, win, qnorm, kvnorm, wuq_t, wuk, wuv_t, *tables_t, *tables)


def _mix_ffn_kernel(y_ref, wo_ref, gmix_ref, h_ref, gpre_ref, wg_ref, wu_ref, wd_ref, gpost_ref,
                    o_ref):
    sub = TM_ROW // FFN_SPLIT
    parts = [slice(i * sub, (i + 1) * sub) for i in range(FFN_SPLIT)]
    z = [jnp.dot(y_ref[p, :], wo_ref[...], preferred_element_type=F32) for p in parts]
    h1, gate, up, down = [], [], [], []
    for i, p in enumerate(parts):
        h1.append(h_ref[p, :] + _rms(z[i], gmix_ref[...]))
        xn = _rms(h1[i], gpre_ref[...]).astype(BF16)
        gate.append(jnp.dot(xn, wg_ref[...], preferred_element_type=F32))
        up.append(jnp.dot(xn, wu_ref[...], preferred_element_type=F32))
    for i in range(FFN_SPLIT):
        act = (gate[i] * jax.nn.sigmoid(gate[i])) * up[i]
        down.append(jnp.dot(act.astype(BF16), wd_ref[...], preferred_element_type=F32))
    for i, p in enumerate(parts):
        o_ref[p, :] = h1[i] + _rms(down[i], gpost_ref[...])


def mix_ffn(y, wo, gmix, h, gpre, wg, wu, wd, layer, gpost):
    t, d = h.shape
    row = lambda i: (i, 0)
    fixed = lambda i: (0, 0)
    layer_blk = lambda i: (layer, 0, 0)
    once = pl.Buffered(1)
    return pl.pallas_call(
        _mix_ffn_kernel,
        grid=(t // TM_ROW,),
        in_specs=[
            pl.BlockSpec((TM_ROW, d), row),
            pl.BlockSpec((d, d), fixed, pipeline_mode=once),
            pl.BlockSpec((1, d), fixed),
            pl.BlockSpec((TM_ROW, d), row),
            pl.BlockSpec((1, d), fixed),
            pl.BlockSpec((None, d, D_FF), layer_blk, pipeline_mode=once),
            pl.BlockSpec((None, d, D_FF), layer_blk, pipeline_mode=once),
            pl.BlockSpec((None, D_FF, d), layer_blk, pipeline_mode=once),
            pl.BlockSpec((1, d), fixed),
        ],
        out_specs=pl.BlockSpec((TM_ROW, d), row),
        out_shape=jax.ShapeDtypeStruct((t, d), F32),
        compiler_params=pltpu.CompilerParams(
            dimension_semantics=("parallel",), vmem_limit_bytes=VMEM_LIMIT,
            allow_input_fusion=[False, True, False, False, False, True, True, True, False]),
        name="mix_ffn",
    )(y, wo, gmix, h, gpre, wg, wu, wd, gpost)


def _angles(pos, n_freq, dim, theta):
    freqs = jnp.power(jnp.float32(theta), -jnp.arange(n_freq, dtype=F32) * 2.0 / dim)
    ang = freqs[:, None] * pos[None, :]
    return lax.optimization_barrier((jnp.cos(ang), jnp.sin(ang)))


def _b_tables():
    quarter = B_HEAD_DIM // 4
    rows = SEQ // GRID_W
    rc, rs = _angles(jnp.arange(rows, dtype=F32), quarter, B_HEAD_DIM // 2, B_ROPE_THETA)
    cc, cs = _angles(jnp.arange(GRID_W, dtype=F32), quarter, B_HEAD_DIM // 2, B_ROPE_THETA)
    by_row = lambda x: jnp.repeat(x, GRID_W, axis=1)
    by_col = lambda x: jnp.tile(x, (1, rows))
    rc, rs, cc, cs = by_row(rc), by_row(rs), by_col(cc), by_col(cs)
    zero = jnp.zeros_like(rs)
    cos_t = jnp.concatenate([rc, rc, cc, cc], axis=0)
    sin_t = jnp.concatenate([-rs, rs, -cs, cs], axis=0)
    sin_a = jnp.concatenate([-rs, zero, -cs, zero], axis=0)
    sin_b = jnp.concatenate([zero, rs, zero, cs], axis=0)
    scale = B_HEAD_DIM ** -0.5 * LOG2E
    return (cos_t * scale, sin_t * scale), (cos_t.T, sin_a.T, sin_b.T)


def _c_tables():
    half = C_ROPE // 2
    c, s = _angles(jnp.arange(SEQ, dtype=F32), half, C_ROPE, C_ROPE_THETA)
    one_lo = jnp.ones((C_NOPE, SEQ), F32)
    one_hi = jnp.ones((LANES - C_NOPE - C_ROPE, SEQ), F32)
    zero = jnp.zeros_like(s)
    cos_t = jnp.concatenate([one_lo, c, c, one_hi], axis=0)
    sin_t = jnp.concatenate([0 * one_lo, -s, s, 0 * one_hi], axis=0)
    sin_a = jnp.concatenate([0 * one_lo, -s, zero, 0 * one_hi], axis=0)
    sin_b = jnp.concatenate([0 * one_lo, zero, s, 0 * one_hi], axis=0)
    scale = (C_NOPE + C_ROPE) ** -0.5 * LOG2E
    return (cos_t * scale, sin_t * scale), (cos_t.T, sin_a.T, sin_b.T)


def _alibi_slopes():
    n = A_NG * A_HEADS
    return jnp.asarray(2.0 ** (-8.0 * np.arange(1, n + 1) / n), dtype=F32)


def _mixer_a(h, gain, wqkv_stack, layer):
    qkv = norm_matmul(h, gain, wqkv_stack, layer)
    return a_attention(qkv.reshape(BATCH, SEQ, A_QKV), _alibi_slopes())


def _mixer_b(h, gain, wqkv, qnorm, knorm):
    nq = B_HEADS * B_HEAD_DIM
    nkv = B_KV_HEADS * B_HEAD_DIM
    w = wqkv.astype(BF16)
    qgain_t = jnp.broadcast_to(qnorm[:, None], (B_HEAD_DIM, TM_ROW))
    tables_t, tables = _b_tables()
    qt, k, vt = b_prep(h, gain, w[:, :nq].T, w[:, nq:nq + nkv], w[:, nq + nkv:].T, qgain_t,
                       knorm[None, :], tables_t, tables)
    return b_attention(qt, k.reshape(BATCH, SEQ, -1), vt)


def _mixer_c(h, gain, win, qnorm, kvnorm, wuq, wukv):
    win_p = jnp.zeros((D_MODEL, C_IN_PAD), F32)
    win_p = win_p.at[:, :C_Q_LORA + C_KV_LORA].set(win[:, :C_Q_LORA + C_KV_LORA])
    kr0 = C_Q_LORA + C_KV_LORA + C_NOPE
    win_p = win_p.at[:, kr0:kr0 + C_ROPE].set(win[:, C_Q_LORA + C_KV_LORA:])
    wuq_p = jnp.pad(wuq.reshape(C_Q_LORA, C_HEADS, C_NOPE + C_ROPE),
                    ((0, 0), (0, 0), (0, LANES - C_NOPE - C_ROPE))).reshape(C_Q_LORA, C_HEADS * LANES)
    wukv3 = wukv.reshape(C_KV_LORA, C_HEADS, C_NOPE + C_V)
    wuk_p = jnp.pad(wukv3[:, :, :C_NOPE],
                    ((0, 0), (0, 0), (0, LANES - C_NOPE))).reshape(C_KV_LORA, C_HEADS * LANES)
    wuv = wukv3[:, :, C_NOPE:].reshape(C_KV_LORA, C_HEADS * C_V)
    tables_t, tables = _c_tables()
    qt, k, vt = c_prep(h, gain, win_p.astype(BF16), qnorm[None, :], kvnorm[None, :],
                       wuq_p.astype(BF16).T, wuk_p.astype(BF16), wuv.astype(BF16).T,
                       tables_t, tables)
    return c_attention(qt, k.reshape(BATCH, SEQ, -1), vt)


def kernel(x, norm_mix_pre, norm_mix_post, norm_ffn_pre, norm_ffn_post, ffn_wg, ffn_wu, ffn_wd,
           a_wqkv, a_wo, b_wqkv, b_qnorm, b_knorm, b_wo,
           c_win, c_qnorm, c_kvnorm, c_wuq, c_wukv, c_wo):
    h = x.reshape(BATCH * SEQ, D_MODEL)
    a_wqkv_b = a_wqkv.astype(BF16)
    wg_b, wu_b, wd_b = ffn_wg.astype(BF16), ffn_wu.astype(BF16), ffn_wd.astype(BF16)
    for i in range(DEPTH):
        kind = i % N_MIXERS
        j = i // N_MIXERS
        gpre = norm_mix_pre[i][None, :]
        if kind == 0:
            y, wo = _mixer_a(h, gpre, a_wqkv_b, j), a_wo[j]
        elif kind == 1:
            y, wo = _mixer_b(h, gpre, b_wqkv[j], b_qnorm[j], b_knorm[j]), b_wo[j]
        else:
            y, wo = _mixer_c(h, gpre, c_win[j], c_qnorm[j], c_kvnorm[j], c_wuq[j], c_wukv[j]), c_wo[j]
        h = mix_ffn(y.reshape(BATCH * SEQ, D_MODEL), wo.astype(BF16), norm_mix_post[i][None, :], h,
                    norm_ffn_pre[i][None, :], wg_b, wu_b, wd_b, i, norm_ffn_post[i][None, :])
    return h.reshape(BATCH, SEQ, D_MODEL)
```
